```python
import jax, jax.numpy as jnp
from jax import lax
import numpy as np

D_MODEL = 2048
BATCH = 16
SEQ = 2048
DEPTH = 4

MEM_LEN = 256
EPS = 1e-6

ATTN_HEAD_DIM = 64
ATTN_Q_HEADS = D_MODEL // 128
ATTN_KV_HEADS = ATTN_Q_HEADS // 4
ATTN_Q_DIM = ATTN_Q_HEADS * ATTN_HEAD_DIM
ATTN_KV_DIM = ATTN_KV_HEADS * ATTN_HEAD_DIM
WINDOW = 128

SGU_CHUNK = 128
SGU_GROUP_DIM = 128
SGU_GROUPS = D_MODEL // 256
SGU_WIDTH = SGU_GROUPS * SGU_GROUP_DIM

HGRN_HEAD_DIM = 128
HGRN_HEADS = D_MODEL // 256
HGRN_WIDTH = HGRN_HEADS * HGRN_HEAD_DIM
HGRN_CHUNK = 64

X_HEADS = 4
X_HEAD_DIM = 128
X_DIM = X_HEADS * X_HEAD_DIM

D_FF = -(-8 * D_MODEL // (3 * 256)) * 256

IN_SIZES = (ATTN_Q_DIM, ATTN_KV_DIM, ATTN_KV_DIM, SGU_WIDTH, SGU_WIDTH,
            HGRN_WIDTH, HGRN_WIDTH, HGRN_WIDTH, HGRN_WIDTH)
IN_WIDTH = sum(IN_SIZES)
N_BRANCH = 3

kernel_name = 'hybrid_swa_sgu_hgrn2_gated_trunk'


def _split_points():
    pts, acc = [], 0
    for s in IN_SIZES[:-1]:
        acc += s
        pts.append(acc)
    return pts


def rms_norm(x, g):
    xf = x.astype(jnp.float32)
    y = xf * lax.rsqrt(jnp.mean(xf * xf, axis=-1, keepdims=True) + EPS)
    return (y * g.astype(jnp.float32)).astype(x.dtype)


def layer_norm(x, g, b):
    xf = x.astype(jnp.float32)
    mu = jnp.mean(xf, axis=-1, keepdims=True)
    xc = xf - mu
    y = xc * lax.rsqrt(jnp.mean(xc * xc, axis=-1, keepdims=True) + EPS)
    return (y * g.astype(jnp.float32) + b.astype(jnp.float32)).astype(x.dtype)


def sliding_window_attention(q, k, v, sinks):
    B, S = q.shape[0], q.shape[1]
    nb = S // WINDOW
    G = ATTN_Q_HEADS // ATTN_KV_HEADS
    qb = q.reshape(B, nb, WINDOW, ATTN_KV_HEADS, G, ATTN_HEAD_DIM)

    def band(t):
        tb = t.reshape(B, nb, WINDOW, ATTN_KV_HEADS, ATTN_HEAD_DIM)
        prev = jnp.pad(tb, ((0, 0), (1, 0), (0, 0), (0, 0), (0, 0)))[:, :-1]
        return jnp.concatenate([prev, tb], axis=2)

    kb, vb = band(k), band(v)
    s = jnp.einsum('bnqkgd,bnskd->bnkgqs', qb, kb,
                   preferred_element_type=jnp.float32) * (ATTN_HEAD_DIM ** -0.5)
    t_idx = jnp.arange(WINDOW)[:, None] + WINDOW
    s_idx = jnp.arange(2 * WINDOW)[None, :]
    diff = t_idx - s_idx
    blk = jnp.arange(nb)[:, None, None]
    valid = (diff >= 0) & (diff < WINDOW) & (blk * WINDOW + s_idx - WINDOW >= 0)
    s = jnp.where(valid[None, :, None, None, :, :], s, -jnp.inf)
    sink = sinks.astype(jnp.float32).reshape(1, 1, ATTN_KV_HEADS, G, 1, 1)
    m = jnp.maximum(jnp.max(s, axis=-1, keepdims=True), sink)
    p = jnp.exp(s - m)
    p = p / (jnp.sum(p, axis=-1, keepdims=True) + jnp.exp(sink - m))
    o = jnp.einsum('bnkgqs,bnskd->bnqkgd', p.astype(v.dtype), vb)
    return o.reshape(B, S, ATTN_Q_DIM)


def chunked_spatial_gating(u, v, ln_g, ln_b, w_s, b_s):
    B, S = u.shape[0], u.shape[1]
    nc = S // SGU_CHUNK
    vn = layer_norm(v, ln_g, ln_b).reshape(B, nc, SGU_CHUNK, SGU_GROUPS, SGU_GROUP_DIM)
    w = w_s * jnp.tril(jnp.ones((SGU_CHUNK, SGU_CHUNK), dtype=w_s.dtype))[None]
    mixed = jnp.einsum('gts,bcsge->bctge', w, vn) + jnp.transpose(b_s)[None, None, :, :, None]
    return u * mixed.reshape(B, S, SGU_WIDTH)


def hgrn2(f_logit, i_in, q_in, g_out, lb, norm_g):
    B, S = q_in.shape[0], q_in.shape[1]
    H, dk, C = HGRN_HEADS, HGRN_HEAD_DIM, HGRN_CHUNK
    nc = S // C
    f = lb + (1.0 - lb) * jax.nn.sigmoid(f_logit.astype(jnp.float32))
    log_f = jnp.log(f)
    k = 1.0 - f
    qf = jax.nn.silu(q_in.astype(jnp.float32)) * (dk ** -0.5)
    vf = i_in.astype(jnp.float32)

    def to_chunks(t):
        return t.reshape(B, nc, C, H, dk).transpose(1, 0, 3, 2, 4)

    causal = jnp.tril(jnp.ones((C, C), dtype=bool))

    def step(state, inp):
        qc, kc, vc, lfc = inp
        b = jnp.cumsum(lfc, axis=2)
        o_inter = jnp.einsum('bhtd,bhde->bhte', qc * jnp.exp(b), state)
        diff = b[:, :, :, None, :] - b[:, :, None, :, :]
        decay = jnp.exp(jnp.where(causal[None, None, :, :, None], diff, -jnp.inf))
        a = jnp.einsum('bhtd,bhsd,bhtsd->bhts', qc, kc, decay)
        o_intra = jnp.einsum('bhts,bhse->bhte', a, vc)
        b_last = b[:, :, -1:, :]
        state = jnp.exp(b_last[:, :, 0, :])[..., None] * state + \
            jnp.einsum('bhsd,bhse->bhde', kc * jnp.exp(b_last - b), vc)
        return state, o_inter + o_intra

    s0 = jnp.zeros((B, H, dk, dk), jnp.float32)
    _, o = lax.scan(step, s0, (to_chunks(qf), to_chunks(k), to_chunks(vf), to_chunks(log_f)))
    o = o.transpose(1, 0, 3, 2, 4).reshape(B, S, H, dk)
    o = o * lax.rsqrt(jnp.mean(o * o, axis=-1, keepdims=True) + EPS) * norm_g.astype(jnp.float32)
    o = o.reshape(B, S, HGRN_WIDTH) * jax.nn.silu(g_out.astype(jnp.float32))
    return o.astype(q_in.dtype)


def memory_cross_attention(h, mem_n, w_q, w_kv, w_o):
    B, S = h.shape[0], h.shape[1]
    M = mem_n.shape[1]
    q = (h @ w_q).reshape(B, S, X_HEADS, X_HEAD_DIM)
    k, v = jnp.split(mem_n @ w_kv, 2, axis=-1)
    k = k.reshape(B, M, X_HEADS, X_HEAD_DIM)
    v = v.reshape(B, M, X_HEADS, X_HEAD_DIM)
    s = jnp.einsum('bqhd,bmhd->bhqm', q, k, preferred_element_type=jnp.float32) * (X_HEAD_DIM ** -0.5)
    p = jax.nn.softmax(s, axis=-1).astype(v.dtype)
    o = jnp.einsum('bhqm,bmhd->bqhd', p, v).reshape(B, S, X_DIM)
    return o @ w_o


def setup_inputs(seed: int = 0) -> dict:
    key = jax.random.key(seed)
    ks = jax.random.split(key, 32)
    L, D = DEPTH, D_MODEL
    f32 = jnp.float32

    def dense(k, shape, fan_in):
        return jax.random.normal(k, shape, f32) * (fan_in ** -0.5)

    def gain(k, shape):
        return 1.0 + 0.02 * jax.random.normal(k, shape, f32)

    return {
        'x': jax.random.normal(ks[0], (BATCH, SEQ, D), f32),
        'mem': jax.random.normal(ks[1], (BATCH, MEM_LEN, D), f32),
        'norm_mix': gain(ks[2], (L, D)),
        'w_in': dense(ks[3], (L, D, IN_WIDTH), D),
        'w_gate': dense(ks[4], (L, D, N_BRANCH * D), D),
        'sinks': 0.5 * jax.random.normal(ks[5], (L, ATTN_Q_HEADS), f32),
        'sgu_ln_g': gain(ks[6], (L, SGU_WIDTH)),
        'sgu_ln_b': 0.02 * jax.random.normal(ks[7], (L, SGU_WIDTH), f32),
        'sgu_w': dense(ks[8], (L, SGU_GROUPS, SGU_CHUNK, SGU_CHUNK), SGU_CHUNK),
        'sgu_b': 1.0 + 0.02 * jax.random.normal(ks[9], (L, SGU_GROUPS, SGU_CHUNK), f32),
        'hgrn_lb': 0.5 * jax.random.normal(ks[10], (L, HGRN_WIDTH), f32),
        'hgrn_norm': gain(ks[11], (L, HGRN_HEAD_DIM)),
        'w_br_a': dense(ks[12], (L, ATTN_Q_DIM, D), ATTN_Q_DIM),
        'w_br_b': dense(ks[13], (L, SGU_WIDTH, D), SGU_WIDTH),
        'w_br_c': dense(ks[14], (L, HGRN_WIDTH, D), HGRN_WIDTH),
        'w_out': dense(ks[15], (L, D, D), D),
        'norm_x': gain(ks[16], (L, D)),
        'mem_norm': gain(ks[17], (D,)),
        'w_xq': dense(ks[18], (L, D, X_DIM), D),
        'w_xkv': dense(ks[19], (L, D, 2 * X_DIM), D),
        'w_xo': dense(ks[20], (L, X_DIM, D), X_DIM),
        'norm_ffn': gain(ks[21], (L, D)),
        'w_ffn_in': dense(ks[22], (L, D, 2 * D_FF), D),
        'w_ffn_out': dense(ks[23], (L, D_FF, D), D_FF),
        'final_norm': gain(ks[24], (D,)),
    }


def reference(x, mem, norm_mix, w_in, w_gate, sinks, sgu_ln_g, sgu_ln_b, sgu_w, sgu_b,
              hgrn_lb, hgrn_norm, w_br_a, w_br_b, w_br_c, w_out, norm_x, mem_norm,
              w_xq, w_xkv, w_xo, norm_ffn, w_ffn_in, w_ffn_out, final_norm):
    B, S = x.shape[0], x.shape[1]
    sm = jax.nn.softmax(hgrn_lb.astype(jnp.float32), axis=0)
    lb_all = jnp.cumsum(sm, axis=0) - sm[0:1]
    mem_n = rms_norm(mem, mem_norm)
    pts = _split_points()
    for l in range(DEPTH):
        h = rms_norm(x, norm_mix[l])
        qa, ka, va, ub, vb, fc, ic, qc, gc = jnp.split(h @ w_in[l], pts, axis=-1)
        y_a = sliding_window_attention(
            qa.reshape(B, S, ATTN_Q_HEADS, ATTN_HEAD_DIM),
            ka.reshape(B, S, ATTN_KV_HEADS, ATTN_HEAD_DIM),
            va.reshape(B, S, ATTN_KV_HEADS, ATTN_HEAD_DIM), sinks[l])
        y_b = chunked_spatial_gating(jax.nn.gelu(ub), jax.nn.gelu(vb),
                                     sgu_ln_g[l], sgu_ln_b[l], sgu_w[l], sgu_b[l])
        y_c = hgrn2(fc, ic, qc, gc, lb_all[l], hgrn_norm[l])
        gate_a, gate_b, gate_c = jnp.split(jax.nn.sigmoid(h @ w_gate[l]), N_BRANCH, axis=-1)
        merged = gate_a * (y_a @ w_br_a[l]) + gate_b * (y_b @ w_br_b[l]) + gate_c * (y_c @ w_br_c[l])
        x = x + merged @ w_out[l]
        x = x + memory_cross_attention(rms_norm(x, norm_x[l]), mem_n, w_xq[l], w_xkv[l], w_xo[l])
        gt, up = jnp.split(rms_norm(x, norm_ffn[l]) @ w_ffn_in[l], 2, axis=-1)
        x = x + (jax.nn.silu(gt) * up) @ w_ffn_out[l]
    return rms_norm(x, final_norm)
```

```python
import functools

import jax
import jax.numpy as jnp
from jax import lax
from jax.experimental import pallas as pl
from jax.experimental.pallas import tpu as pltpu

F32 = jnp.float32
BF16 = jnp.bfloat16
EPS = 1e-6

LANES = 128
V7X_VMEM_LIMIT = 56 * 1024 * 1024

ATTN_HEAD_DIM = 64
ATTN_Q_HEADS = 16
ATTN_KV_HEADS = 4
WINDOW = 128
SGU_CHUNK = 128
SGU_GROUPS = 8
HGRN_HEADS = 8
HGRN_HEAD_DIM = 128
HGRN_CHUNK = 64
HGRN_SUB = 16
X_HEADS = 4
X_HEAD_DIM = 128

COL_QA, COL_UB, COL_VB, COL_FC, COL_IC, COL_QC, COL_GC, COL_KV, COL_GATE = (
    0, 1024, 2048, 3072, 4096, 5120, 6144, 7168, 7680)


def _params(sem):
    return pltpu.CompilerParams(dimension_semantics=sem, vmem_limit_bytes=V7X_VMEM_LIMIT)


def _rms_rows(x_ref, g_ref, h_ref, rows):
    slab = 64
    g = g_ref[...]

    def body(i, c):
        r = pl.ds(pl.multiple_of(i * slab, slab), slab)
        x = x_ref[r, :]
        ms = jnp.mean(x * x, axis=-1, keepdims=True)
        h_ref[r, :] = (x * lax.rsqrt(ms + EPS) * g).astype(h_ref.dtype)
        return c

    lax.fori_loop(0, rows // slab, body, 0)


def _norm_matmul_kernel(x_ref, g_ref, w_ref, o_ref, h_ref, *, sig_from, tm):
    j = pl.program_id(1)

    @pl.when(j == 0)
    def _():
        _rms_rows(x_ref, g_ref, h_ref, tm)

    acc = jnp.dot(h_ref[...], w_ref[...], preferred_element_type=F32)
    if sig_from is None:
        o_ref[...] = acc.astype(o_ref.dtype)
    else:
        @pl.when(j < sig_from)
        def _():
            o_ref[...] = acc.astype(o_ref.dtype)

        @pl.when(j >= sig_from)
        def _():
            o_ref[...] = jax.nn.sigmoid(acc).astype(o_ref.dtype)


def _norm_matmul(x, g, w, layer, *, tm, tn, sig_cols_from=None):
    n, d = x.shape
    nout = w.shape[-1]
    sig_from = None if sig_cols_from is None else sig_cols_from // tn
    kern = functools.partial(_norm_matmul_kernel, sig_from=sig_from, tm=tm)
    if w.ndim == 3:
        w_spec = pl.BlockSpec((None, d, tn), lambda i, j: (layer, 0, j))
    else:
        w_spec = pl.BlockSpec((d, tn), lambda i, j: (0, j))
    return pl.pallas_call(
        kern,
        grid=(n // tm, nout // tn),
        in_specs=[pl.BlockSpec((tm, d), lambda i, j: (i, 0)),
                  pl.BlockSpec((1, d), lambda i, j: (0, 0)),
                  w_spec],
        out_specs=pl.BlockSpec((tm, tn), lambda i, j: (i, j)),
        out_shape=jax.ShapeDtypeStruct((n, nout), BF16),
        scratch_shapes=[pltpu.VMEM((tm, d), BF16)],
        compiler_params=_params(("parallel", "arbitrary")),
        name="norm_matmul",
    )(x, g, w)


def _swap_halves(t):
    return jnp.concatenate([t[:, 64:], t[:, :64]], axis=1)


def _swa_kernel(sink_ref, q_ref, kvc_ref, kvp_ref, o_ref, kv_buf, *, tq):
    t_idx = pl.program_id(1)
    kv_buf[0:WINDOW, :] = kvp_ref[...]
    kv_buf[WINDOW:, :] = kvc_ref[...]

    row = lax.broadcasted_iota(jnp.int32, (WINDOW, 2 * WINDOW), 0)
    col = lax.broadcasted_iota(jnp.int32, (WINDOW, 2 * WINDOW), 1)
    band = (col > row) & (col <= row + WINDOW)
    lane = lax.broadcasted_iota(jnp.int32, (2 * WINDOW, LANES), 1)
    lo = lane < 64
    out_lane = lax.broadcasted_iota(jnp.int32, (WINDOW, LANES), 1)
    scale = ATTN_HEAD_DIM ** -0.5

    def blk(i, c):
        r0 = pl.multiple_of(i * WINDOW, WINDOW)
        cmin = jnp.where((t_idx == 0) & (i == 0), WINDOW, 0)
        valid = band & (col >= cmin)
        for m in range(2):
            kt = kv_buf[pl.ds(r0, 2 * WINDOW), m * LANES:(m + 1) * LANES]
            vt = kv_buf[pl.ds(r0, 2 * WINDOW), 256 + m * LANES:256 + (m + 1) * LANES]
            ks, vs = _swap_halves(kt), _swap_halves(vt)
            zero = jnp.zeros_like(kt)
            for e in range(2):
                kvh = 2 * m + e
                k_lo = jnp.where(lo, kt if e == 0 else ks, zero)
                k_hi = jnp.where(lo, zero, ks if e == 0 else kt)
                v_lo = jnp.where(lo, vt if e == 0 else vs, zero)
                v_hi = jnp.where(lo, zero, vs if e == 0 else vt)
                kcat = jnp.concatenate([k_lo, k_hi], axis=0)
                vcat = jnp.concatenate([v_lo, v_hi], axis=0)
                for pp in range(2):
                    p = 2 * kvh + pp
                    qt = q_ref[pl.ds(r0, WINDOW), p * LANES:(p + 1) * LANES]
                    s = lax.dot_general(qt, kcat, (((1,), (1,)), ((), ())),
                                        preferred_element_type=F32) * scale
                    probs, invs = [], []
                    for hh in range(2):
                        sink = sink_ref[2 * p + hh]
                        sh = jnp.where(valid, s[:, hh * 256:(hh + 1) * 256], -jnp.inf)
                        mx = jnp.maximum(jnp.max(sh, axis=-1, keepdims=True), sink)
                        pe = jnp.exp(sh - mx)
                        den = jnp.sum(pe, axis=-1, keepdims=True) + jnp.exp(sink - mx)
                        probs.append(pe.astype(BF16))
                        invs.append(1.0 / den)
                    pcat = jnp.concatenate(probs, axis=1)
                    o = jnp.dot(pcat, vcat, preferred_element_type=F32)
                    o = o * jnp.where(out_lane < 64, invs[0], invs[1])
                    o_ref[pl.ds(r0, WINDOW), p * LANES:(p + 1) * LANES] = o.astype(o_ref.dtype)
        return c

    lax.fori_loop(0, tq // WINDOW, blk, 0)


def _swa(proj, sinks_l, *, batch, seq, tq):
    n = proj.shape[0]
    nt = seq // tq
    qw = ATTN_Q_HEADS * ATTN_HEAD_DIM
    kvw = 2 * ATTN_KV_HEADS * ATTN_HEAD_DIM
    kv_blk = COL_KV // kvw
    bpt = tq // WINDOW

    def prev_map(b, t):
        return (jnp.maximum(b * (seq // WINDOW) + t * bpt - 1, 0), kv_blk)

    return pl.pallas_call(
        functools.partial(_swa_kernel, tq=tq),
        grid=(batch, nt),
        in_specs=[pl.BlockSpec(memory_space=pltpu.SMEM),
                  pl.BlockSpec((tq, qw), lambda b, t: (b * nt + t, COL_QA // qw)),
                  pl.BlockSpec((tq, kvw), lambda b, t: (b * nt + t, kv_blk)),
                  pl.BlockSpec((WINDOW, kvw), prev_map)],
        out_specs=pl.BlockSpec((tq, qw), lambda b, t: (b * nt + t, 0)),
        out_shape=jax.ShapeDtypeStruct((n, qw), BF16),
        scratch_shapes=[pltpu.VMEM((tq + WINDOW, kvw), BF16)],
        compiler_params=_params(("parallel", "parallel")),
        name="swa",
    )(sinks_l, proj, proj, proj)


def _sgu_kernel(u_ref, v_ref, lng_ref, lnb_ref, w_ref, bt_ref, o_ref, *, ts):
    row = lax.broadcasted_iota(jnp.int32, (SGU_CHUNK, SGU_CHUNK), 0)
    col = lax.broadcasted_iota(jnp.int32, (SGU_CHUNK, SGU_CHUNK), 1)
    tril = row >= col
    lng = lng_ref[...]
    lnb = lnb_ref[...]

    def chunk(c, carry):
        r = pl.ds(pl.multiple_of(c * SGU_CHUNK, SGU_CHUNK), SGU_CHUNK)
        v = jax.nn.gelu(v_ref[r, :].astype(F32))
        mu = jnp.mean(v, axis=-1, keepdims=True)
        vc = v - mu
        var = jnp.mean(vc * vc, axis=-1, keepdims=True)
        vn = (vc * lax.rsqrt(var + EPS) * lng + lnb).astype(BF16)
        for g in range(SGU_GROUPS):
            cs = slice(g * LANES, (g + 1) * LANES)
            w = jnp.where(tril, w_ref[g], 0.0).astype(BF16)
            mixed = jnp.dot(w, vn[:, cs], preferred_element_type=F32) + bt_ref[:, g:g + 1]
            u = jax.nn.gelu(u_ref[r, cs].astype(F32))
            o_ref[r, cs] = (u * mixed).astype(o_ref.dtype)
        return carry

    lax.fori_loop(0, ts // SGU_CHUNK, chunk, 0)


def _sgu(proj, ln_g, ln_b, w, bt, layer, *, ts):
    n = proj.shape[0]
    width = SGU_GROUPS * LANES
    return pl.pallas_call(
        functools.partial(_sgu_kernel, ts=ts),
        grid=(n // ts,),
        in_specs=[pl.BlockSpec((ts, width), lambda i: (i, COL_UB // width)),
                  pl.BlockSpec((ts, width), lambda i: (i, COL_VB // width)),
                  pl.BlockSpec((None, 1, width), lambda i: (layer, 0, 0)),
                  pl.BlockSpec((None, 1, width), lambda i: (layer, 0, 0)),
                  pl.BlockSpec((None, SGU_GROUPS, SGU_CHUNK, SGU_CHUNK), lambda i: (layer, 0, 0, 0)),
                  pl.BlockSpec((None, SGU_CHUNK, SGU_GROUPS), lambda i: (layer, 0, 0))],
        out_specs=pl.BlockSpec((ts, width), lambda i: (i, 0)),
        out_shape=jax.ShapeDtypeStruct((n, width), BF16),
        compiler_params=_params(("parallel",)),
        name="sgu",
    )(proj, proj, ln_g, ln_b, w, bt)


def _hgrn_kernel(f_ref, i_ref, q_ref, g_ref, lb_ref, ng_ref, o_ref, state_ref, *, tt):
    C, SUB = HGRN_CHUNK, HGRN_SUB

    @pl.when(pl.program_id(2) == 0)
    def _():
        state_ref[...] = jnp.zeros_like(state_ref)

    lb = lb_ref[...]
    ng = ng_ref[...]
    r64 = lax.broadcasted_iota(jnp.int32, (C, C), 0)
    c64 = lax.broadcasted_iota(jnp.int32, (C, C), 1)
    tril = jnp.where(r64 >= c64, 1.0, 0.0).astype(BF16)
    t_i = lax.broadcasted_iota(jnp.int32, (C, LANES), 0)
    l_i = lax.broadcasted_iota(jnp.int32, (C, LANES), 1)
    own_half = ((l_i // C) == ((t_i // SUB) % 2)) & ((l_i % C) <= t_i)
    valid0 = own_half & (t_i < 2 * SUB)
    valid1 = own_half & (t_i >= 2 * SUB)
    sub_of_row = t_i // SUB

    def chunk(c, carry):
        r = pl.ds(pl.multiple_of(c * C, C), C)
        f = lb + (1.0 - lb) * jax.nn.sigmoid(f_ref[r, :].astype(F32))
        lf = jnp.log(f)
        k = 1.0 - f
        qf = jax.nn.silu(q_ref[r, :].astype(F32)) * (HGRN_HEAD_DIM ** -0.5)
        v = i_ref[r, :]
        hi = lf.astype(BF16)
        lo = (lf - hi.astype(F32)).astype(BF16)
        b = (jnp.dot(tril, hi, preferred_element_type=F32)
             + jnp.dot(tril, lo, preferred_element_type=F32))
        b_last = b[C - 1:C, :]
        refs = [jnp.zeros((1, LANES), F32)] + [b[SUB * i - 1:SUB * i, :] for i in range(1, C // SUB)]
        bref = refs[0]
        for i in range(1, C // SUB):
            bref = jnp.where(sub_of_row >= i, refs[i], bref)
        q_rel = (qf * jnp.exp(b - bref)).astype(BF16)
        q_abs = (qf * jnp.exp(b)).astype(BF16)
        kcat = jnp.concatenate([(k * jnp.exp(rf - b)).astype(BF16) for rf in refs], axis=0)
        sc = lax.dot_general(q_rel, kcat, (((1,), (1,)), ((), ())), preferred_element_type=F32)
        a = jnp.where(valid0, sc[:, :LANES], 0.0) + jnp.where(valid1, sc[:, LANES:], 0.0)
        vv = jnp.concatenate([v, v], axis=0)
        st = state_ref[...]
        o = (lax.dot_general(q_abs, st.astype(BF16), (((1,), (1,)), ((), ())), preferred_element_type=F32)
             + jnp.dot(a.astype(BF16), vv, preferred_element_type=F32))
        k_end = (k * jnp.exp(b_last - b)).astype(BF16)
        state_ref[...] = st * jnp.exp(b_last) + lax.dot_general(
            v, k_end, (((0,), (0,)), ((), ())), preferred_element_type=F32)
        on = o * lax.rsqrt(jnp.mean(o * o, axis=-1, keepdims=True) + EPS) * ng
        o_ref[r, :] = (on * jax.nn.silu(g_ref[r, :].astype(F32))).astype(o_ref.dtype)
        return carry

    lax.fori_loop(0, tt // C, chunk, 0)


def _hgrn(proj, lb_l, norm_g, layer, *, batch, seq, tt):
    n = proj.shape[0]
    nt = seq // tt
    width = HGRN_HEADS * HGRN_HEAD_DIM

    def col(off):
        return lambda b, h, t: (b * nt + t, off // LANES + h)

    return pl.pallas_call(
        functools.partial(_hgrn_kernel, tt=tt),
        grid=(batch, HGRN_HEADS, nt),
        in_specs=[pl.BlockSpec((tt, LANES), col(COL_FC)),
                  pl.BlockSpec((tt, LANES), col(COL_IC)),
                  pl.BlockSpec((tt, LANES), col(COL_QC)),
                  pl.BlockSpec((tt, LANES), col(COL_GC)),
                  pl.BlockSpec((1, LANES), lambda b, h, t: (0, h)),
                  pl.BlockSpec((None, 1, LANES), lambda b, h, t: (layer, 0, 0))],
        out_specs=pl.BlockSpec((tt, LANES), lambda b, h, t: (b * nt + t, h)),
        out_shape=jax.ShapeDtypeStruct((n, width), BF16),
        scratch_shapes=[pltpu.VMEM((HGRN_HEAD_DIM, HGRN_HEAD_DIM), F32)],
        compiler_params=_params(("parallel", "parallel", "arbitrary")),
        name="hgrn2",
    )(proj, proj, proj, proj, lb_l, norm_g)


def _merge_kernel(x_ref, ya_ref, yb_ref, yc_ref, ga_ref, gb_ref, gc_ref,
                  wa_ref, wb_ref, wc_ref, wo_ref, o_ref, acc_ref):
    j = pl.program_id(1)

    @pl.when(j == 0)
    def _():
        acc_ref[...] = jnp.zeros_like(acc_ref)

    m = (ga_ref[...].astype(F32) * jnp.dot(ya_ref[...], wa_ref[...], preferred_element_type=F32)
         + gb_ref[...].astype(F32) * jnp.dot(yb_ref[...], wb_ref[...], preferred_element_type=F32)
         + gc_ref[...].astype(F32) * jnp.dot(yc_ref[...], wc_ref[...], preferred_element_type=F32))
    acc_ref[...] += jnp.dot(m.astype(BF16), wo_ref[...], preferred_element_type=F32)

    @pl.when(j == pl.num_programs(1) - 1)
    def _():
        o_ref[...] = x_ref[...] + acc_ref[...]


def _merge(x, ya, yb, yc, proj, wa, wb, wc, wo, layer, *, tm, tk):
    n, d = x.shape
    wbr = ya.shape[1]
    g0 = COL_GATE // tk
    gstep = d // tk

    def gate(br):
        return pl.BlockSpec((tm, tk), lambda i, j: (i, g0 + br * gstep + j))

    y_spec = pl.BlockSpec((tm, wbr), lambda i, j: (i, 0))
    wbr_spec = pl.BlockSpec((None, wbr, tk), lambda i, j: (layer, 0, j))
    return pl.pallas_call(
        _merge_kernel,
        grid=(n // tm, d // tk),
        in_specs=[pl.BlockSpec((tm, d), lambda i, j: (i, 0)),
                  y_spec, y_spec, y_spec, gate(0), gate(1), gate(2),
                  wbr_spec, wbr_spec, wbr_spec,
                  pl.BlockSpec((None, tk, d), lambda i, j: (layer, j, 0))],
        out_specs=pl.BlockSpec((tm, d), lambda i, j: (i, 0)),
        out_shape=jax.ShapeDtypeStruct((n, d), F32),
        scratch_shapes=[pltpu.VMEM((tm, d), F32)],
        compiler_params=_params(("parallel", "arbitrary")),
        name="merge",
    )(x, ya, yb, yc, proj, proj, proj, wa, wb, wc, wo)


def _xattn_kernel(x_ref, g_ref, wq_ref, kv_ref, wo_ref, o_ref, h_ref, *, tm):
    _rms_rows(x_ref, g_ref, h_ref, tm)
    q = jnp.dot(h_ref[...], wq_ref[...], preferred_element_type=F32).astype(BF16)
    xw = X_HEADS * X_HEAD_DIM
    scale = X_HEAD_DIM ** -0.5
    outs = []
    for hd in range(X_HEADS):
        cs = slice(hd * X_HEAD_DIM, (hd + 1) * X_HEAD_DIM)
        kh = kv_ref[:, cs]
        vh = kv_ref[:, xw + hd * X_HEAD_DIM:xw + (hd + 1) * X_HEAD_DIM]
        s = lax.dot_general(q[:, cs], kh, (((1,), (1,)), ((), ())), preferred_element_type=F32) * scale
        mx = jnp.max(s, axis=-1, keepdims=True)
        pe = jnp.exp(s - mx)
        inv = 1.0 / jnp.sum(pe, axis=-1, keepdims=True)
        o = jnp.dot(pe.astype(BF16), vh, preferred_element_type=F32) * inv
        outs.append(o.astype(BF16))
    oc = jnp.concatenate(outs, axis=1)
    o_ref[...] = x_ref[...] + jnp.dot(oc, wo_ref[...], preferred_element_type=F32)


def _xattn(x, g, wq, kv, wo, layer, *, seq, mem_len, tm):
    n, d = x.shape
    xw = X_HEADS * X_HEAD_DIM
    tiles_per_seq = seq // tm
    return pl.pallas_call(
        functools.partial(_xattn_kernel, tm=tm),
        grid=(n // tm,),
        in_specs=[pl.BlockSpec((tm, d), lambda i: (i, 0)),
                  pl.BlockSpec((None, 1, d), lambda i: (layer, 0, 0)),
                  pl.BlockSpec((None, d, xw), lambda i: (layer, 0, 0)),
                  pl.BlockSpec((mem_len, 2 * xw), lambda i: (i // tiles_per_seq, layer)),
                  pl.BlockSpec((None, xw, d), lambda i: (layer, 0, 0))],
        out_specs=pl.BlockSpec((tm, d), lambda i: (i, 0)),
        out_shape=jax.ShapeDtypeStruct((n, d), F32),
        scratch_shapes=[pltpu.VMEM((tm, d), BF16)],
        compiler_params=_params(("parallel",)),
        name="xattn",
    )(x, g, wq, kv, wo)


def _ffn_kernel(x_ref, g_ref, wg_ref, wu_ref, wo_ref, o_ref, h_ref, acc_ref, *, tm):
    f = pl.program_id(1)

    @pl.when(f == 0)
    def _():
        _rms_rows(x_ref, g_ref, h_ref, tm)
        acc_ref[...] = jnp.zeros_like(acc_ref)

    h = h_ref[...]
    gt = jnp.dot(h, wg_ref[...], preferred_element_type=F32)
    up = jnp.dot(h, wu_ref[...], preferred_element_type=F32)
    a = (jax.nn.silu(gt) * up).astype(BF16)
    acc_ref[...] += jnp.dot(a, wo_ref[...], preferred_element_type=F32)

    @pl.when(f == pl.num_programs(1) - 1)
    def _():
        o_ref[...] = x_ref[...] + acc_ref[...]


def _ffn(x, g, w_in, w_out, layer, *, tm, tf):
    n, d = x.shape
    dff = w_out.shape[1]
    nf = dff // tf
    return pl.pallas_call(
        functools.partial(_ffn_kernel, tm=tm),
        grid=(n // tm, nf),
        in_specs=[pl.BlockSpec((tm, d), lambda i, f: (i, 0)),
                  pl.BlockSpec((None, 1, d), lambda i, f: (layer, 0, 0)),
                  pl.BlockSpec((None, d, tf), lambda i, f: (layer, 0, f)),
                  pl.BlockSpec((None, d, tf), lambda i, f: (layer, 0, nf + f)),
                  pl.BlockSpec((None, tf, d), lambda i, f: (layer, f, 0))],
        out_specs=pl.BlockSpec((tm, d), lambda i, f: (i, 0)),
        out_shape=jax.ShapeDtypeStruct((n, d), F32),
        scratch_shapes=[pltpu.VMEM((tm, d), BF16), pltpu.VMEM((tm, d), F32)],
        compiler_params=_params(("parallel", "arbitrary")),
        name="ffn",
    )(x, g, w_in, w_in, w_out)


def _final_norm_kernel(x_ref, g_ref, o_ref, *, tm):
    _rms_rows(x_ref, g_ref, o_ref, tm)


def _final_norm(x, g, *, tm):
    n, d = x.shape
    return pl.pallas_call(
        functools.partial(_final_norm_kernel, tm=tm),
        grid=(n // tm,),
        in_specs=[pl.BlockSpec((tm, d), lambda i: (i, 0)),
                  pl.BlockSpec((1, d), lambda i: (0, 0))],
        out_specs=pl.BlockSpec((tm, d), lambda i: (i, 0)),
        out_shape=jax.ShapeDtypeStruct((n, d), F32),
        compiler_params=_params(("parallel",)),
        name="final_norm",
    )(x, g)


def kernel(x, mem, norm_mix, w_in, w_gate, sinks, sgu_ln_g, sgu_ln_b, sgu_w, sgu_b, hgrn_lb, hgrn_norm,
           w_br_a, w_br_b, w_br_c, w_out, norm_x, mem_norm, w_xq, w_xkv, w_xo, norm_ffn, w_ffn_in,
           w_ffn_out, final_norm):
    batch, seq, d = x.shape
    depth = w_in.shape[0]
    mem_len = mem.shape[1]
    n = batch * seq

    (wqa, wka, wva, wub, wvb, wfc, wic, wqc, wgc) = jnp.split(
        w_in, [1024, 1280, 1536, 2560, 3584, 4608, 5632, 6656], axis=-1)
    w_all = jnp.concatenate([wqa, wub, wvb, wfc, wic, wqc, wgc, wka, wva, w_gate], axis=-1).astype(BF16)
    wa, wb, wc, wo = (w.astype(BF16) for w in (w_br_a, w_br_b, w_br_c, w_out))
    wxq, wxo = w_xq.astype(BF16), w_xo.astype(BF16)
    wxkv = jnp.concatenate([w_xkv[l] for l in range(depth)], axis=-1).astype(BF16)
    wfi, wfo = w_ffn_in.astype(BF16), w_ffn_out.astype(BF16)
    sgu_bt = jnp.swapaxes(sgu_b, 1, 2)
    sm = jax.nn.softmax(hgrn_lb.astype(F32), axis=0)
    lb_all = jnp.cumsum(sm, axis=0) - sm[0:1]
    sgu_ln_g, sgu_ln_b, hgrn_norm, norm_x, norm_ffn = (
        p[:, None, :] for p in (sgu_ln_g, sgu_ln_b, hgrn_norm, norm_x, norm_ffn))

    xf = x.reshape(n, d)
    kv_all = _norm_matmul(mem.reshape(batch * mem_len, d), mem_norm.reshape(1, d), wxkv, 0,
                          tm=1024, tn=512)
    for l in range(depth):
        proj = _norm_matmul(xf, norm_mix[l:l + 1], w_all, l, tm=1024, tn=512, sig_cols_from=COL_GATE)
        ya = _swa(proj, sinks[l], batch=batch, seq=seq, tq=512)
        yb = _sgu(proj, sgu_ln_g, sgu_ln_b, sgu_w, sgu_bt, l, ts=512)
        yc = _hgrn(proj, lb_all[l:l + 1], hgrn_norm, l, batch=batch, seq=seq, tt=512)
        xf = _merge(xf, ya, yb, yc, proj, wa, wb, wc, wo, l, tm=512, tk=512)
        xf = _xattn(xf, norm_x, wxq, kv_all, wxo, l, seq=seq, mem_len=mem_len, tm=512)
        xf = _ffn(xf, norm_ffn, wfi, wfo, l, tm=512, tf=512)
    return _final_norm(xf, final_norm.reshape(1, d), tm=512).reshape(batch, seq, d)
```

```python
import functools

import jax
import jax.numpy as jnp
from jax import lax
from jax.experimental import pallas as pl
from jax.experimental.pallas import tpu as pltpu

F32 = jnp.float32
BF16 = jnp.bfloat16
EPS = 1e-6

LANES = 128
V7X_VMEM_LIMIT = 56 * 1024 * 1024

ATTN_HEAD_DIM = 64
ATTN_Q_HEADS = 16
ATTN_KV_HEADS = 4
WINDOW = 128
SGU_CHUNK = 128
SGU_GROUPS = 8
HGRN_HEADS = 8
HGRN_HEAD_DIM = 128
HGRN_CHUNK = 64
HGRN_SUB = 16
X_HEADS = 4
X_HEAD_DIM = 128

COL_QA, COL_UB, COL_VB, COL_FC, COL_IC, COL_QC, COL_GC, COL_GATE, COL_KV, COL_END = (
    0, 1024, 2048, 3072, 4096, 5120, 6144, 7168, 13312, 13824)
SIG_SUB = 512


def _exact_div(a, b):
    assert a % b == 0, (a, b)
    return a // b


def _params(sem):
    return pltpu.CompilerParams(dimension_semantics=sem, vmem_limit_bytes=V7X_VMEM_LIMIT)


def _rms_rows(x_ref, g_ref, h_ref, rows):
    slab = 64
    g = g_ref[...]

    def body(i, c):
        r = pl.ds(pl.multiple_of(i * slab, slab), slab)
        x = x_ref[r, :]
        ms = jnp.mean(x * x, axis=-1, keepdims=True)
        h_ref[r, :] = (x * lax.rsqrt(ms + EPS) * g).astype(h_ref.dtype)
        return c

    lax.fori_loop(0, _exact_div(rows, slab), body, 0)


def _norm_matmul_kernel(x_ref, g_ref, w_ref, o_ref, h_ref, *, sig_cols, tm, tn):
    j = pl.program_id(1)

    @pl.when(j == 0)
    def _():
        _rms_rows(x_ref, g_ref, h_ref, tm)

    acc = jnp.dot(h_ref[...], w_ref[...], preferred_element_type=F32)
    if sig_cols is None:
        o_ref[...] = acc.astype(o_ref.dtype)
    else:
        lo, hi = sig_cols
        for s in range(_exact_div(tn, SIG_SUB)):
            cs = slice(s * SIG_SUB, (s + 1) * SIG_SUB)
            c0 = j * tn + s * SIG_SUB
            blk = acc[:, cs]
            o_ref[:, cs] = jnp.where((c0 >= lo) & (c0 < hi), jax.nn.sigmoid(blk), blk).astype(o_ref.dtype)


def _norm_matmul(x, g, w, layer, *, tm, tn, sig_cols=None):
    n, d = x.shape
    nout = w.shape[-1]
    if sig_cols is not None:
        _exact_div(sig_cols[0], SIG_SUB), _exact_div(sig_cols[1], SIG_SUB)
    kern = functools.partial(_norm_matmul_kernel, sig_cols=sig_cols, tm=tm, tn=tn)
    if w.ndim == 3:
        w_spec = pl.BlockSpec((None, d, tn), lambda i, j: (layer, 0, j))
    else:
        w_spec = pl.BlockSpec((d, tn), lambda i, j: (0, j))
    return pl.pallas_call(
        kern,
        grid=(_exact_div(n, tm), _exact_div(nout, tn)),
        in_specs=[pl.BlockSpec((tm, d), lambda i, j: (i, 0)),
                  pl.BlockSpec((1, d), lambda i, j: (0, 0)),
                  w_spec],
        out_specs=pl.BlockSpec((tm, tn), lambda i, j: (i, j)),
        out_shape=jax.ShapeDtypeStruct((n, nout), BF16),
        scratch_shapes=[pltpu.VMEM((tm, d), BF16)],
        compiler_params=_params(("parallel", "arbitrary")),
        name="norm_matmul",
    )(x, g, w)


def _swap_halves(t):
    return jnp.concatenate([t[:, 64:], t[:, :64]], axis=1)


def _swa_kernel(sink_ref, q_ref, kvc_ref, kvp_ref, o_ref, kv_buf, *, tq):
    t_idx = pl.program_id(1)
    kv_buf[0:WINDOW, :] = kvp_ref[...]
    kv_buf[WINDOW:, :] = kvc_ref[...]

    row = lax.broadcasted_iota(jnp.int32, (WINDOW, 2 * WINDOW), 0)
    col = lax.broadcasted_iota(jnp.int32, (WINDOW, 2 * WINDOW), 1)
    band = (col > row) & (col <= row + WINDOW)
    lane = lax.broadcasted_iota(jnp.int32, (2 * WINDOW, LANES), 1)
    lo = lane < 64
    out_lane = lax.broadcasted_iota(jnp.int32, (WINDOW, LANES), 1)
    scale = ATTN_HEAD_DIM ** -0.5

    def blk(i, c):
        r0 = pl.multiple_of(i * WINDOW, WINDOW)
        cmin = jnp.where((t_idx == 0) & (i == 0), WINDOW, 0)
        valid = band & (col >= cmin)
        kcats, vcats = [], []
        for m in range(2):
            kt = kv_buf[pl.ds(r0, 2 * WINDOW), m * LANES:(m + 1) * LANES]
            vt = kv_buf[pl.ds(r0, 2 * WINDOW), 256 + m * LANES:256 + (m + 1) * LANES]
            ks, vs = _swap_halves(kt), _swap_halves(vt)
            zero = jnp.zeros_like(kt)
            for e in range(2):
                k_lo = jnp.where(lo, kt if e == 0 else ks, zero)
                k_hi = jnp.where(lo, zero, ks if e == 0 else kt)
                v_lo = jnp.where(lo, vt if e == 0 else vs, zero)
                v_hi = jnp.where(lo, zero, vs if e == 0 else vt)
                kcats.append(jnp.concatenate([k_lo, k_hi], axis=0))
                vcats.append(jnp.concatenate([v_lo, v_hi], axis=0))

        def score(p):
            qt = q_ref[pl.ds(r0, WINDOW), p * LANES:(p + 1) * LANES]
            return lax.dot_general(qt, kcats[p // 2], (((1,), (1,)), ((), ())),
                                   preferred_element_type=F32) * scale

        def attend(p, s):
            probs, invs = [], []
            for hh in range(2):
                sink = sink_ref[2 * p + hh]
                sh = jnp.where(valid, s[:, hh * 256:(hh + 1) * 256], -jnp.inf)
                mx = jnp.maximum(jnp.max(sh, axis=-1, keepdims=True), sink)
                pe = jnp.exp(sh - mx)
                den = jnp.sum(pe, axis=-1, keepdims=True) + jnp.exp(sink - mx)
                probs.append(pe.astype(BF16))
                invs.append(1.0 / den)
            pcat = jnp.concatenate(probs, axis=1)
            o = jnp.dot(pcat, vcats[p // 2], preferred_element_type=F32)
            o = o * jnp.where(out_lane < 64, invs[0], invs[1])
            o_ref[pl.ds(r0, WINDOW), p * LANES:(p + 1) * LANES] = o.astype(o_ref.dtype)

        n_tiles = ATTN_Q_HEADS // 2
        s_next = score(0)
        for p in range(n_tiles):
            s_cur = s_next
            if p + 1 < n_tiles:
                s_next = score(p + 1)
            attend(p, s_cur)
        return c

    lax.fori_loop(0, _exact_div(tq, WINDOW), blk, 0)


def _swa(proj, sinks_l, *, batch, seq, tq):
    n = proj.shape[0]
    nt = _exact_div(seq, tq)
    qw = ATTN_Q_HEADS * ATTN_HEAD_DIM
    kvw = 2 * ATTN_KV_HEADS * ATTN_HEAD_DIM
    kv_blk = _exact_div(COL_KV, kvw)
    bpt = _exact_div(tq, WINDOW)

    def prev_map(b, t):
        return (jnp.maximum(b * _exact_div(seq, WINDOW) + t * bpt - 1, 0), kv_blk)

    return pl.pallas_call(
        functools.partial(_swa_kernel, tq=tq),
        grid=(batch, nt),
        in_specs=[pl.BlockSpec(memory_space=pltpu.SMEM),
                  pl.BlockSpec((tq, qw), lambda b, t: (b * nt + t, _exact_div(COL_QA, qw))),
                  pl.BlockSpec((tq, kvw), lambda b, t: (b * nt + t, kv_blk)),
                  pl.BlockSpec((WINDOW, kvw), prev_map)],
        out_specs=pl.BlockSpec((tq, qw), lambda b, t: (b * nt + t, 0)),
        out_shape=jax.ShapeDtypeStruct((n, qw), BF16),
        scratch_shapes=[pltpu.VMEM((tq + WINDOW, kvw), BF16)],
        compiler_params=_params(("parallel", "parallel")),
        name="swa",
    )(sinks_l, proj, proj, proj)


def _sgu_kernel(u_ref, v_ref, lng_ref, lnb_ref, w_ref, bt_ref, o_ref, *, ts):
    row = lax.broadcasted_iota(jnp.int32, (SGU_CHUNK, SGU_CHUNK), 0)
    col = lax.broadcasted_iota(jnp.int32, (SGU_CHUNK, SGU_CHUNK), 1)
    tril = row >= col
    lng = lng_ref[...]
    lnb = lnb_ref[...]

    def chunk(c, carry):
        r = pl.ds(pl.multiple_of(c * SGU_CHUNK, SGU_CHUNK), SGU_CHUNK)
        v = jax.nn.gelu(v_ref[r, :].astype(F32))
        mu = jnp.mean(v, axis=-1, keepdims=True)
        vc = v - mu
        var = jnp.mean(vc * vc, axis=-1, keepdims=True)
        vn = (vc * lax.rsqrt(var + EPS) * lng + lnb).astype(BF16)
        for g in range(SGU_GROUPS):
            cs = slice(g * LANES, (g + 1) * LANES)
            w = jnp.where(tril, w_ref[g], 0.0).astype(BF16)
            mixed = jnp.dot(w, vn[:, cs], preferred_element_type=F32) + bt_ref[:, g:g + 1]
            u = jax.nn.gelu(u_ref[r, cs].astype(F32))
            o_ref[r, cs] = (u * mixed).astype(o_ref.dtype)
        return carry

    lax.fori_loop(0, _exact_div(ts, SGU_CHUNK), chunk, 0)


def _sgu(proj, ln_g, ln_b, w, bt, layer, *, ts):
    n = proj.shape[0]
    width = SGU_GROUPS * LANES
    return pl.pallas_call(
        functools.partial(_sgu_kernel, ts=ts),
        grid=(_exact_div(n, ts),),
        in_specs=[pl.BlockSpec((ts, width), lambda i: (i, _exact_div(COL_UB, width))),
                  pl.BlockSpec((ts, width), lambda i: (i, _exact_div(COL_VB, width))),
                  pl.BlockSpec((None, 1, width), lambda i: (layer, 0, 0)),
                  pl.BlockSpec((None, 1, width), lambda i: (layer, 0, 0)),
                  pl.BlockSpec((None, SGU_GROUPS, SGU_CHUNK, SGU_CHUNK), lambda i: (layer, 0, 0, 0)),
                  pl.BlockSpec((None, SGU_CHUNK, SGU_GROUPS), lambda i: (layer, 0, 0))],
        out_specs=pl.BlockSpec((ts, width), lambda i: (i, 0)),
        out_shape=jax.ShapeDtypeStruct((n, width), BF16),
        compiler_params=_params(("parallel",)),
        name="sgu",
    )(proj, proj, ln_g, ln_b, w, bt)


def _hgrn_kernel(f_ref, i_ref, q_ref, g_ref, lb_ref, ng_ref, o_ref, state_ref, *, tt, hb):
    C, SUB = HGRN_CHUNK, HGRN_SUB

    @pl.when(pl.program_id(2) == 0)
    def _():
        state_ref[...] = jnp.zeros_like(state_ref)

    ng = ng_ref[...]
    r64 = lax.broadcasted_iota(jnp.int32, (C, C), 0)
    c64 = lax.broadcasted_iota(jnp.int32, (C, C), 1)
    tril = jnp.where(r64 >= c64, 1.0, 0.0).astype(BF16)
    t_i = lax.broadcasted_iota(jnp.int32, (C, LANES), 0)
    l_i = lax.broadcasted_iota(jnp.int32, (C, LANES), 1)
    own_half = ((l_i // C) == ((t_i // SUB) % 2)) & ((l_i % C) <= t_i)
    valid0 = own_half & (t_i < 2 * SUB)
    valid1 = own_half & (t_i >= 2 * SUB)
    sub_of_row = t_i // SUB

    def gates(r, h):
        cs = slice(h * LANES, (h + 1) * LANES)
        lb = lb_ref[:, cs]
        f = lb + (1.0 - lb) * jax.nn.sigmoid(f_ref[r, cs].astype(F32))
        lf = jnp.log(f)
        k = 1.0 - f
        qf = jax.nn.silu(q_ref[r, cs].astype(F32)) * (HGRN_HEAD_DIM ** -0.5)
        hi = lf.astype(BF16)
        lo = (lf - hi.astype(F32)).astype(BF16)
        b = (jnp.dot(tril, hi, preferred_element_type=F32)
             + jnp.dot(tril, lo, preferred_element_type=F32))
        return k, qf, b

    def scores(k, qf, b):
        refs = [jnp.zeros((1, LANES), F32)] + [b[SUB * i - 1:SUB * i, :] for i in range(1, _exact_div(C, SUB))]
        bref = refs[0]
        for i in range(1, _exact_div(C, SUB)):
            bref = jnp.where(sub_of_row >= i, refs[i], bref)
        q_rel = (qf * jnp.exp(b - bref)).astype(BF16)
        kcat = jnp.concatenate([(k * jnp.exp(rf - b)).astype(BF16) for rf in refs], axis=0)
        sc = lax.dot_general(q_rel, kcat, (((1,), (1,)), ((), ())), preferred_element_type=F32)
        return jnp.where(valid0, sc[:, :LANES], 0.0) + jnp.where(valid1, sc[:, LANES:], 0.0)

    def outputs(r, h, k, qf, b, a):
        cs = slice(h * LANES, (h + 1) * LANES)
        v = i_ref[r, cs]
        b_last = b[C - 1:C, :]
        q_abs = (qf * jnp.exp(b)).astype(BF16)
        vv = jnp.concatenate([v, v], axis=0)
        st = state_ref[h]
        o = (lax.dot_general(q_abs, st.astype(BF16), (((1,), (1,)), ((), ())), preferred_element_type=F32)
             + jnp.dot(a.astype(BF16), vv, preferred_element_type=F32))
        k_end = (k * jnp.exp(b_last - b)).astype(BF16)
        state_ref[h] = st * jnp.exp(b_last) + lax.dot_general(
            v, k_end, (((0,), (0,)), ((), ())), preferred_element_type=F32)
        on = o * lax.rsqrt(jnp.mean(o * o, axis=-1, keepdims=True) + EPS) * ng
        o_ref[r, cs] = (on * jax.nn.silu(g_ref[r, cs].astype(F32))).astype(o_ref.dtype)

    def chunk(c, carry):
        r = pl.ds(pl.multiple_of(c * C, C), C)
        kqb = [gates(r, h) for h in range(hb)]
        a = [scores(*t) for t in kqb]
        for h in range(hb):
            outputs(r, h, *kqb[h], a[h])
        return carry

    lax.fori_loop(0, _exact_div(tt, C), chunk, 0)


def _hgrn(proj, lb_l, norm_g, layer, *, batch, seq, tt, hb):
    n = proj.shape[0]
    nt = _exact_div(seq, tt)
    width = HGRN_HEADS * HGRN_HEAD_DIM
    bw = hb * LANES

    def col(off):
        return lambda b, h, t: (b * nt + t, _exact_div(off, bw) + h)

    return pl.pallas_call(
        functools.partial(_hgrn_kernel, tt=tt, hb=hb),
        grid=(batch, _exact_div(HGRN_HEADS, hb), nt),
        in_specs=[pl.BlockSpec((tt, bw), col(COL_FC)),
                  pl.BlockSpec((tt, bw), col(COL_IC)),
                  pl.BlockSpec((tt, bw), col(COL_QC)),
                  pl.BlockSpec((tt, bw), col(COL_GC)),
                  pl.BlockSpec((1, bw), lambda b, h, t: (0, h)),
                  pl.BlockSpec((None, 1, LANES), lambda b, h, t: (layer, 0, 0))],
        out_specs=pl.BlockSpec((tt, bw), lambda b, h, t: (b * nt + t, h)),
        out_shape=jax.ShapeDtypeStruct((n, width), BF16),
        scratch_shapes=[pltpu.VMEM((hb, HGRN_HEAD_DIM, HGRN_HEAD_DIM), F32)],
        compiler_params=_params(("parallel", "parallel", "arbitrary")),
        name="hgrn2",
    )(proj, proj, proj, proj, lb_l, norm_g)


def _merge_kernel(x_ref, ya_ref, yb_ref, yc_ref, ga_ref, gb_ref, gc_ref,
                  wa_ref, wb_ref, wc_ref, wo_ref, o_ref):
    @pl.when(pl.program_id(1) == 0)
    def _():
        o_ref[...] = x_ref[...]

    m = (ga_ref[...].astype(F32) * jnp.dot(ya_ref[...], wa_ref[...], preferred_element_type=F32)
         + gb_ref[...].astype(F32) * jnp.dot(yb_ref[...], wb_ref[...], preferred_element_type=F32)
         + gc_ref[...].astype(F32) * jnp.dot(yc_ref[...], wc_ref[...], preferred_element_type=F32))
    o_ref[...] += jnp.dot(m.astype(BF16), wo_ref[...], preferred_element_type=F32)


def _merge(x, ya, yb, yc, proj, wa, wb, wc, wo, layer, *, tm, tk):
    n, d = x.shape
    wbr = ya.shape[1]
    g0 = _exact_div(COL_GATE, tk)
    gstep = _exact_div(d, tk)

    def gate(br):
        return pl.BlockSpec((tm, tk), lambda i, j: (i, g0 + br * gstep + j))

    y_spec = pl.BlockSpec((tm, wbr), lambda i, j: (i, 0))
    wbr_spec = pl.BlockSpec((None, wbr, tk), lambda i, j: (layer, 0, j))
    return pl.pallas_call(
        _merge_kernel,
        grid=(_exact_div(n, tm), gstep),
        in_specs=[pl.BlockSpec((tm, d), lambda i, j: (i, 0)),
                  y_spec, y_spec, y_spec, gate(0), gate(1), gate(2),
                  wbr_spec, wbr_spec, wbr_spec,
                  pl.BlockSpec((None, tk, d), lambda i, j: (layer, j, 0))],
        out_specs=pl.BlockSpec((tm, d), lambda i, j: (i, 0)),
        out_shape=jax.ShapeDtypeStruct((n, d), F32),
        compiler_params=_params(("parallel", "arbitrary")),
        name="merge",
    )(x, ya, yb, yc, proj, proj, proj, wa, wb, wc, wo)


def _xattn_kernel(x_ref, g_ref, wq_ref, kv_ref, wo_ref, o_ref, h_ref, *, tm):
    _rms_rows(x_ref, g_ref, h_ref, tm)
    q = jnp.dot(h_ref[...], wq_ref[...], preferred_element_type=F32).astype(BF16)
    xw = X_HEADS * X_HEAD_DIM
    scale = X_HEAD_DIM ** -0.5

    def score(hd):
        cs = slice(hd * X_HEAD_DIM, (hd + 1) * X_HEAD_DIM)
        return lax.dot_general(q[:, cs], kv_ref[:, cs], (((1,), (1,)), ((), ())),
                               preferred_element_type=F32) * scale

    def attend(hd, s):
        vh = kv_ref[:, xw + hd * X_HEAD_DIM:xw + (hd + 1) * X_HEAD_DIM]
        mx = jnp.max(s, axis=-1, keepdims=True)
        pe = jnp.exp(s - mx)
        inv = 1.0 / jnp.sum(pe, axis=-1, keepdims=True)
        return (jnp.dot(pe.astype(BF16), vh, preferred_element_type=F32) * inv).astype(BF16)

    outs = []
    s_next = score(0)
    for hd in range(X_HEADS):
        s_cur = s_next
        if hd + 1 < X_HEADS:
            s_next = score(hd + 1)
        outs.append(attend(hd, s_cur))
    oc = jnp.concatenate(outs, axis=1)
    o_ref[...] = x_ref[...] + jnp.dot(oc, wo_ref[...], preferred_element_type=F32)


def _xattn(x, g, wq, kv, wo, layer, *, seq, mem_len, tm):
    n, d = x.shape
    xw = X_HEADS * X_HEAD_DIM
    tiles_per_seq = _exact_div(seq, tm)
    return pl.pallas_call(
        functools.partial(_xattn_kernel, tm=tm),
        grid=(_exact_div(n, tm),),
        in_specs=[pl.BlockSpec((tm, d), lambda i: (i, 0)),
                  pl.BlockSpec((None, 1, d), lambda i: (layer, 0, 0)),
                  pl.BlockSpec((None, d, xw), lambda i: (layer, 0, 0)),
                  pl.BlockSpec((mem_len, 2 * xw), lambda i: (i // tiles_per_seq, layer)),
                  pl.BlockSpec((None, xw, d), lambda i: (layer, 0, 0))],
        out_specs=pl.BlockSpec((tm, d), lambda i: (i, 0)),
        out_shape=jax.ShapeDtypeStruct((n, d), F32),
        scratch_shapes=[pltpu.VMEM((tm, d), BF16)],
        compiler_params=_params(("parallel",)),
        name="xattn",
    )(x, g, wq, kv, wo)


def _ffn_kernel(x_ref, g_ref, wg_ref, wu_ref, wo_ref, o_ref, h_ref, *, tm):
    @pl.when(pl.program_id(1) == 0)
    def _():
        _rms_rows(x_ref, g_ref, h_ref, tm)
        o_ref[...] = x_ref[...]

    h = h_ref[...]
    gt = jnp.dot(h, wg_ref[...], preferred_element_type=F32)
    up = jnp.dot(h, wu_ref[...], preferred_element_type=F32)
    a = (jax.nn.silu(gt) * up).astype(BF16)
    o_ref[...] += jnp.dot(a, wo_ref[...], preferred_element_type=F32)


def _ffn(x, g, w_in, w_out, layer, *, tm, tf):
    n, d = x.shape
    dff = w_out.shape[1]
    nf = _exact_div(dff, tf)
    return pl.pallas_call(
        functools.partial(_ffn_kernel, tm=tm),
        grid=(_exact_div(n, tm), nf),
        in_specs=[pl.BlockSpec((tm, d), lambda i, f: (i, 0)),
                  pl.BlockSpec((None, 1, d), lambda i, f: (layer, 0, 0)),
                  pl.BlockSpec((None, d, tf), lambda i, f: (layer, 0, f)),
                  pl.BlockSpec((None, d, tf), lambda i, f: (layer, 0, nf + f)),
                  pl.BlockSpec((None, tf, d), lambda i, f: (layer, f, 0))],
        out_specs=pl.BlockSpec((tm, d), lambda i, f: (i, 0)),
        out_shape=jax.ShapeDtypeStruct((n, d), F32),
        scratch_shapes=[pltpu.VMEM((tm, d), BF16)],
        compiler_params=_params(("parallel", "arbitrary")),
        name="ffn",
    )(x, g, w_in, w_in, w_out)


def _final_norm_kernel(x_ref, g_ref, o_ref, *, tm):
    _rms_rows(x_ref, g_ref, o_ref, tm)


def _final_norm(x, g, *, tm):
    n, d = x.shape
    return pl.pallas_call(
        functools.partial(_final_norm_kernel, tm=tm),
        grid=(_exact_div(n, tm),),
        in_specs=[pl.BlockSpec((tm, d), lambda i: (i, 0)),
                  pl.BlockSpec((1, d), lambda i: (0, 0))],
        out_specs=pl.BlockSpec((tm, d), lambda i: (i, 0)),
        out_shape=jax.ShapeDtypeStruct((n, d), F32),
        compiler_params=_params(("parallel",)),
        name="final_norm",
    )(x, g)


def kernel(x, mem, norm_mix, w_in, w_gate, sinks, sgu_ln_g, sgu_ln_b, sgu_w, sgu_b, hgrn_lb, hgrn_norm,
           w_br_a, w_br_b, w_br_c, w_out, norm_x, mem_norm, w_xq, w_xkv, w_xo, norm_ffn, w_ffn_in,
           w_ffn_out, final_norm):
    batch, seq, d = x.shape
    depth = w_in.shape[0]
    mem_len = mem.shape[1]
    n = batch * seq

    (wqa, wka, wva, wub, wvb, wfc, wic, wqc, wgc) = jnp.split(
        w_in, [1024, 1280, 1536, 2560, 3584, 4608, 5632, 6656], axis=-1)
    w_all = jnp.concatenate([wqa, wub, wvb, wfc, wic, wqc, wgc, w_gate, wka, wva], axis=-1).astype(BF16)
    wa, wb, wc, wo = (w.astype(BF16) for w in (w_br_a, w_br_b, w_br_c, w_out))
    wxq, wxo = w_xq.astype(BF16), w_xo.astype(BF16)
    wxkv = jnp.concatenate([w_xkv[l] for l in range(depth)], axis=-1).astype(BF16)
    wfi, wfo = w_ffn_in.astype(BF16), w_ffn_out.astype(BF16)
    sgu_bt = jnp.swapaxes(sgu_b, 1, 2)
    sm = jax.nn.softmax(hgrn_lb.astype(F32), axis=0)
    lb_all = jnp.cumsum(sm, axis=0) - sm[0:1]
    sgu_ln_g, sgu_ln_b, hgrn_norm, norm_x, norm_ffn = (
        p[:, None, :] for p in (sgu_ln_g, sgu_ln_b, hgrn_norm, norm_x, norm_ffn))

    xf = x.reshape(n, d)
    kv_all = _norm_matmul(mem.reshape(batch * mem_len, d), mem_norm.reshape(1, d), wxkv, 0,
                          tm=1024, tn=512)
    for l in range(depth):
        proj = _norm_matmul(xf, norm_mix[l:l + 1], w_all, l, tm=1024, tn=1536, sig_cols=(COL_GATE, COL_KV))
        ya = _swa(proj, sinks[l], batch=batch, seq=seq, tq=512)
        yb = _sgu(proj, sgu_ln_g, sgu_ln_b, sgu_w, sgu_bt, l, ts=512)
        yc = _hgrn(proj, lb_all[l:l + 1], hgrn_norm, l, batch=batch, seq=seq, tt=512, hb=8)
        xf = _merge(xf, ya, yb, yc, proj, wa, wb, wc, wo, l, tm=512, tk=1024)
        xf = _xattn(xf, norm_x, wxq, kv_all, wxo, l, seq=seq, mem_len=mem_len, tm=512)
        xf = _ffn(xf, norm_ffn, wfi, wfo, l, tm=1024, tf=512)
    return _final_norm(xf, final_norm.reshape(1, d), tm=512).reshape(batch, seq, d)
```

```python
import functools

import jax
import jax.numpy as jnp
from jax import lax
from jax.experimental import pallas as pl
from jax.experimental.pallas import tpu as pltpu

F32 = jnp.float32
BF16 = jnp.bfloat16
EPS = 1e-6

LANES = 128
V7X_VMEM_LIMIT = 56 * 1024 * 1024

ATTN_HEAD_DIM = 64
ATTN_Q_HEADS = 16
ATTN_KV_HEADS = 4
WINDOW = 128
SGU_CHUNK = 128
SGU_GROUPS = 8
HGRN_HEADS = 8
HGRN_HEAD_DIM = 128
HGRN_CHUNK = 64
HGRN_SUB = 16
X_HEADS = 4
X_HEAD_DIM = 128

COL_QA, COL_UB, COL_VB, COL_FC, COL_IC, COL_QC, COL_GC, COL_GATE, COL_KV, COL_END = (
    0, 1024, 2048, 3072, 4096, 5120, 6144, 7168, 13312, 13824)
ACT_SUB = 512
FFN_TF = 512


def _exact_div(a, b):
    assert a % b == 0, (a, b)
    return a // b


def _params(sem):
    return pltpu.CompilerParams(dimension_semantics=sem, vmem_limit_bytes=V7X_VMEM_LIMIT)


def _rms_rows(x_ref, g_ref, h_ref, rows):
    slab = 64
    g = g_ref[...]

    def body(i, c):
        r = pl.ds(pl.multiple_of(i * slab, slab), slab)
        x = x_ref[r, :]
        ms = jnp.mean(x * x, axis=-1, keepdims=True)
        h_ref[r, :] = (x * lax.rsqrt(ms + EPS) * g).astype(h_ref.dtype)
        return c

    lax.fori_loop(0, _exact_div(rows, slab), body, 0, unroll=2)


def _in_range(c0, lo_hi):
    _exact_div(lo_hi[0], ACT_SUB), _exact_div(lo_hi[1], ACT_SUB)
    return (c0 >= lo_hi[0]) & (c0 < lo_hi[1])


def _projection_activation(x, c0):
    is_sig = _in_range(c0, (COL_GATE, COL_KV))
    mul = jnp.where(_in_range(c0, (COL_QA, COL_UB)), ATTN_HEAD_DIM ** -0.5, 1.0)
    return jnp.where(is_sig, 0.5 + 0.5 * jnp.tanh(0.5 * x), x * mul)


def _norm_matmul_kernel(x_ref, g_ref, w_ref, o_ref, h_ref, *, activate, tm, tn):
    j = pl.program_id(1)

    @pl.when(j == 0)
    def _():
        _rms_rows(x_ref, g_ref, h_ref, tm)

    if not activate:
        o_ref[...] = jnp.dot(h_ref[...], w_ref[...], preferred_element_type=F32).astype(o_ref.dtype)
    else:
        h = h_ref[...]
        for s in range(_exact_div(tn, ACT_SUB)):
            cs = slice(s * ACT_SUB, (s + 1) * ACT_SUB)
            acc = jnp.dot(h, w_ref[:, cs], preferred_element_type=F32)
            o_ref[:, cs] = _projection_activation(acc, j * tn + s * ACT_SUB).astype(o_ref.dtype)


def _norm_matmul(x, g, w, layer, *, tm, tn, activate=False):
    n, d = x.shape
    nout = w.shape[-1]
    kern = functools.partial(_norm_matmul_kernel, activate=activate, tm=tm, tn=tn)
    if w.ndim == 3:
        w_spec = pl.BlockSpec((None, d, tn), lambda i, j: (layer, 0, j))
    else:
        w_spec = pl.BlockSpec((d, tn), lambda i, j: (0, j))
    return pl.pallas_call(
        kern,
        grid=(_exact_div(n, tm), _exact_div(nout, tn)),
        in_specs=[pl.BlockSpec((tm, d), lambda i, j: (i, 0)),
                  pl.BlockSpec((1, d), lambda i, j: (0, 0)),
                  w_spec],
        out_specs=pl.BlockSpec((tm, tn), lambda i, j: (i, j)),
        out_shape=jax.ShapeDtypeStruct((n, nout), BF16),
        scratch_shapes=[pltpu.VMEM((tm, d), BF16)],
        compiler_params=_params(("parallel", "arbitrary")),
        name="norm_matmul",
    )(x, g, w)


def _swap_halves(t):
    return jnp.concatenate([t[:, 64:], t[:, :64]], axis=1)


def _swa_kernel(sink_ref, q_ref, kvc_ref, kvp_ref, o_ref, kv_buf, *, tq):
    t_idx = pl.program_id(1)
    kv_buf[0:WINDOW, :] = kvp_ref[...]
    kv_buf[WINDOW:, :] = kvc_ref[...]

    row = lax.broadcasted_iota(jnp.int32, (WINDOW, 2 * WINDOW), 0)
    col = lax.broadcasted_iota(jnp.int32, (WINDOW, 2 * WINDOW), 1)
    band = (col > row) & (col <= row + WINDOW)
    lane = lax.broadcasted_iota(jnp.int32, (2 * WINDOW, LANES), 1)
    lo = lane < 64
    out_lane = lax.broadcasted_iota(jnp.int32, (WINDOW, LANES), 1)

    def blk(i, c):
        r0 = pl.multiple_of(i * WINDOW, WINDOW)
        cmin = jnp.where((t_idx == 0) & (i == 0), WINDOW, 0)
        valid = band & (col >= cmin)
        kcats, vcats = [], []
        for m in range(2):
            kt = kv_buf[pl.ds(r0, 2 * WINDOW), m * LANES:(m + 1) * LANES]
            vt = kv_buf[pl.ds(r0, 2 * WINDOW), 256 + m * LANES:256 + (m + 1) * LANES]
            ks, vs = _swap_halves(kt), _swap_halves(vt)
            zero = jnp.zeros_like(kt)
            for e in range(2):
                k_lo = jnp.where(lo, kt if e == 0 else ks, zero)
                k_hi = jnp.where(lo, zero, ks if e == 0 else kt)
                v_lo = jnp.where(lo, vt if e == 0 else vs, zero)
                v_hi = jnp.where(lo, zero, vs if e == 0 else vt)
                kcats.append(jnp.concatenate([k_lo, k_hi], axis=0))
                vcats.append(jnp.concatenate([v_lo, v_hi], axis=0))

        def score(p):
            qt = q_ref[pl.ds(r0, WINDOW), p * LANES:(p + 1) * LANES]
            return lax.dot_general(qt, kcats[p // 2], (((1,), (1,)), ((), ())),
                                   preferred_element_type=F32)

        def attend(p, s):
            probs, invs = [], []
            for hh in range(2):
                sink = sink_ref[2 * p + hh]
                sh = jnp.where(valid, s[:, hh * 256:(hh + 1) * 256], -jnp.inf)
                mx = jnp.maximum(jnp.max(sh, axis=-1, keepdims=True), sink)
                pe = jnp.exp(sh - mx)
                den = jnp.sum(pe, axis=-1, keepdims=True) + jnp.exp(sink - mx)
                probs.append(pe.astype(BF16))
                invs.append(1.0 / den)
            pcat = jnp.concatenate(probs, axis=1)
            o = jnp.dot(pcat, vcats[p // 2], preferred_element_type=F32)
            o = o * jnp.where(out_lane < 64, invs[0], invs[1])
            o_ref[pl.ds(r0, WINDOW), p * LANES:(p + 1) * LANES] = o.astype(o_ref.dtype)

        n_tiles = ATTN_Q_HEADS // 2
        s_next = score(0)
        for p in range(n_tiles):
            s_cur = s_next
            if p + 1 < n_tiles:
                s_next = score(p + 1)
            attend(p, s_cur)
        return c

    lax.fori_loop(0, _exact_div(tq, WINDOW), blk, 0)


def _swa(proj, sinks_l, *, batch, seq, tq):
    n = proj.shape[0]
    nt = _exact_div(seq, tq)
    qw = ATTN_Q_HEADS * ATTN_HEAD_DIM
    kvw = 2 * ATTN_KV_HEADS * ATTN_HEAD_DIM
    kv_blk = _exact_div(COL_KV, kvw)
    bpt = _exact_div(tq, WINDOW)

    def prev_map(b, t):
        return (jnp.maximum(b * _exact_div(seq, WINDOW) + t * bpt - 1, 0), kv_blk)

    return pl.pallas_call(
        functools.partial(_swa_kernel, tq=tq),
        grid=(batch, nt),
        in_specs=[pl.BlockSpec(memory_space=pltpu.SMEM),
                  pl.BlockSpec((tq, qw), lambda b, t: (b * nt + t, _exact_div(COL_QA, qw))),
                  pl.BlockSpec((tq, kvw), lambda b, t: (b * nt + t, kv_blk)),
                  pl.BlockSpec((WINDOW, kvw), prev_map)],
        out_specs=pl.BlockSpec((tq, qw), lambda b, t: (b * nt + t, 0)),
        out_shape=jax.ShapeDtypeStruct((n, qw), BF16),
        scratch_shapes=[pltpu.VMEM((tq + WINDOW, kvw), BF16)],
        compiler_params=_params(("parallel", "parallel")),
        name="swa",
    )(sinks_l, proj, proj, proj)


def _sgu_kernel(u_ref, v_ref, lng_ref, lnb_ref, w_ref, bt_ref, o_ref, *, ts):
    row = lax.broadcasted_iota(jnp.int32, (SGU_CHUNK, SGU_CHUNK), 0)
    col = lax.broadcasted_iota(jnp.int32, (SGU_CHUNK, SGU_CHUNK), 1)
    tril = row >= col
    lng = lng_ref[...]
    lnb = lnb_ref[...]

    def chunk(c, carry):
        r = pl.ds(pl.multiple_of(c * SGU_CHUNK, SGU_CHUNK), SGU_CHUNK)
        v = jax.nn.gelu(v_ref[r, :].astype(F32))
        mu = jnp.mean(v, axis=-1, keepdims=True)
        vc = v - mu
        var = jnp.mean(vc * vc, axis=-1, keepdims=True)
        vn = (vc * lax.rsqrt(var + EPS) * lng + lnb).astype(BF16)
        for g in range(SGU_GROUPS):
            cs = slice(g * LANES, (g + 1) * LANES)
            w = jnp.where(tril, w_ref[g], 0.0).astype(BF16)
            mixed = jnp.dot(w, vn[:, cs], preferred_element_type=F32) + bt_ref[:, g:g + 1]
            u = jax.nn.gelu(u_ref[r, cs].astype(F32))
            o_ref[r, cs] = (u * mixed).astype(o_ref.dtype)
        return carry

    lax.fori_loop(0, _exact_div(ts, SGU_CHUNK), chunk, 0)


def _sgu(proj, ln_g, ln_b, w, bt, layer, *, ts):
    n = proj.shape[0]
    width = SGU_GROUPS * LANES
    return pl.pallas_call(
        functools.partial(_sgu_kernel, ts=ts),
        grid=(_exact_div(n, ts),),
        in_specs=[pl.BlockSpec((ts, width), lambda i: (i, _exact_div(COL_UB, width))),
                  pl.BlockSpec((ts, width), lambda i: (i, _exact_div(COL_VB, width))),
                  pl.BlockSpec((None, 1, width), lambda i: (layer, 0, 0)),
                  pl.BlockSpec((None, 1, width), lambda i: (layer, 0, 0)),
                  pl.BlockSpec((None, SGU_GROUPS, SGU_CHUNK, SGU_CHUNK), lambda i: (layer, 0, 0, 0)),
                  pl.BlockSpec((None, SGU_CHUNK, SGU_GROUPS), lambda i: (layer, 0, 0))],
        out_specs=pl.BlockSpec((ts, width), lambda i: (i, 0)),
        out_shape=jax.ShapeDtypeStruct((n, width), BF16),
        compiler_params=_params(("parallel",)),
        name="sgu",
    )(proj, proj, ln_g, ln_b, w, bt)


def _hgrn_kernel(f_ref, i_ref, q_ref, g_ref, lb_ref, ng_ref, o_ref, state_ref, *, tt, hb):
    C, SUB = HGRN_CHUNK, HGRN_SUB

    @pl.when(pl.program_id(2) == 0)
    def _():
        state_ref[...] = jnp.zeros_like(state_ref)

    ng = ng_ref[...]
    r64 = lax.broadcasted_iota(jnp.int32, (C, C), 0)
    c64 = lax.broadcasted_iota(jnp.int32, (C, C), 1)
    tril = jnp.where(r64 >= c64, 1.0, 0.0).astype(BF16)
    t_i = lax.broadcasted_iota(jnp.int32, (C, LANES), 0)
    l_i = lax.broadcasted_iota(jnp.int32, (C, LANES), 1)
    own_half = ((l_i // C) == ((t_i // SUB) % 2)) & ((l_i % C) <= t_i)
    valid0 = own_half & (t_i < 2 * SUB)
    valid1 = own_half & (t_i >= 2 * SUB)

    def gates(r, h):
        cs = slice(h * LANES, (h + 1) * LANES)
        lb = lb_ref[:, cs]
        half_span = 0.5 * (1.0 - lb)
        f = (lb + half_span) + half_span * jnp.tanh(0.5 * f_ref[r, cs].astype(F32))
        lf = jnp.log(f)
        k = 1.0 - f
        x_q = q_ref[r, cs].astype(F32)
        qf = x_q * ((0.5 * HGRN_HEAD_DIM ** -0.5) + (0.5 * HGRN_HEAD_DIM ** -0.5) * jnp.tanh(0.5 * x_q))
        hi = lf.astype(BF16)
        lo = (lf - hi.astype(F32)).astype(BF16)
        b = (jnp.dot(tril, hi, preferred_element_type=F32)
             + jnp.dot(tril, lo, preferred_element_type=F32))
        return k, qf, b

    def scores(k, qf, b):
        refs = [jnp.zeros((1, LANES), F32)] + [b[SUB * i - 1:SUB * i, :] for i in range(1, _exact_div(C, SUB))]
        q_rel = jnp.concatenate(
            [(qf[SUB * i:SUB * (i + 1)] * jnp.exp(b[SUB * i:SUB * (i + 1)] - rf)).astype(BF16)
             for i, rf in enumerate(refs)], axis=0)
        blocks = []
        for i, rf in enumerate(refs):
            rows = SUB * (i + 1)
            blocks.append((k[:rows] * jnp.exp(rf - b[:rows])).astype(BF16))
            if rows < C:
                blocks.append(jnp.zeros((C - rows, LANES), BF16))
        kcat = jnp.concatenate(blocks, axis=0)
        sc = lax.dot_general(q_rel, kcat, (((1,), (1,)), ((), ())), preferred_element_type=F32)
        return jnp.where(valid0, sc[:, :LANES], jnp.where(valid1, sc[:, LANES:], 0.0))

    def outputs(r, h, k, qf, b, a):
        cs = slice(h * LANES, (h + 1) * LANES)
        v = i_ref[r, cs]
        b_last = b[C - 1:C, :]
        q_abs = (qf * jnp.exp(b)).astype(BF16)
        vv = jnp.concatenate([v, v], axis=0)
        st = state_ref[h]
        o = (lax.dot_general(q_abs, st.astype(BF16), (((1,), (1,)), ((), ())), preferred_element_type=F32)
             + jnp.dot(a.astype(BF16), vv, preferred_element_type=F32))
        k_end = (k * jnp.exp(b_last - b)).astype(BF16)
        state_ref[h] = st * jnp.exp(b_last) + lax.dot_general(
            v, k_end, (((0,), (0,)), ((), ())), preferred_element_type=F32)
        on = o * lax.rsqrt(jnp.mean(o * o, axis=-1, keepdims=True) + EPS) * ng
        x_g = g_ref[r, cs].astype(F32)
        o_ref[r, cs] = (on * (x_g * (0.5 + 0.5 * jnp.tanh(0.5 * x_g)))).astype(o_ref.dtype)

    def chunk(c, carry):
        r = pl.ds(pl.multiple_of(c * C, C), C)
        kqb = [gates(r, h) for h in range(hb)]
        a = [scores(*t) for t in kqb]
        for h in range(hb):
            outputs(r, h, *kqb[h], a[h])
        return carry

    lax.fori_loop(0, _exact_div(tt, C), chunk, 0)


def _hgrn(proj, lb_l, norm_g, layer, *, batch, seq, tt, hb):
    n = proj.shape[0]
    nt = _exact_div(seq, tt)
    width = HGRN_HEADS * HGRN_HEAD_DIM
    bw = hb * LANES

    def col(off):
        return lambda b, h, t: (b * nt + t, _exact_div(off, bw) + h)

    return pl.pallas_call(
        functools.partial(_hgrn_kernel, tt=tt, hb=hb),
        grid=(batch, _exact_div(HGRN_HEADS, hb), nt),
        in_specs=[pl.BlockSpec((tt, bw), col(COL_FC)),
                  pl.BlockSpec((tt, bw), col(COL_IC)),
                  pl.BlockSpec((tt, bw), col(COL_QC)),
                  pl.BlockSpec((tt, bw), col(COL_GC)),
                  pl.BlockSpec((1, bw), lambda b, h, t: (0, h)),
                  pl.BlockSpec((None, 1, LANES), lambda b, h, t: (layer, 0, 0))],
        out_specs=pl.BlockSpec((tt, bw), lambda b, h, t: (b * nt + t, h)),
        out_shape=jax.ShapeDtypeStruct((n, width), BF16),
        scratch_shapes=[pltpu.VMEM((hb, HGRN_HEAD_DIM, HGRN_HEAD_DIM), F32)],
        compiler_params=_params(("parallel", "parallel", "arbitrary")),
        name="hgrn2",
    )(proj, proj, proj, proj, lb_l, norm_g)


def _merge_kernel(x_ref, ya_ref, yb_ref, yc_ref, ga_ref, gb_ref, gc_ref,
                  wa_ref, wb_ref, wc_ref, wo_ref, o_ref):
    @pl.when(pl.program_id(1) == 0)
    def _():
        o_ref[...] = x_ref[...]

    m = (ga_ref[...].astype(F32) * jnp.dot(ya_ref[...], wa_ref[...], preferred_element_type=F32)
         + gb_ref[...].astype(F32) * jnp.dot(yb_ref[...], wb_ref[...], preferred_element_type=F32)
         + gc_ref[...].astype(F32) * jnp.dot(yc_ref[...], wc_ref[...], preferred_element_type=F32))
    o_ref[...] += jnp.dot(m.astype(BF16), wo_ref[...], preferred_element_type=F32)


def _merge(x, ya, yb, yc, proj, wa, wb, wc, wo, layer, *, tm, tk):
    n, d = x.shape
    wbr = ya.shape[1]
    g0 = _exact_div(COL_GATE, tk)
    gstep = _exact_div(d, tk)

    def gate(br):
        return pl.BlockSpec((tm, tk), lambda i, j: (i, g0 + br * gstep + j))

    y_spec = pl.BlockSpec((tm, wbr), lambda i, j: (i, 0))
    wbr_spec = pl.BlockSpec((None, wbr, tk), lambda i, j: (layer, 0, j))
    return pl.pallas_call(
        _merge_kernel,
        grid=(_exact_div(n, tm), gstep),
        in_specs=[pl.BlockSpec((tm, d), lambda i, j: (i, 0)),
                  y_spec, y_spec, y_spec, gate(0), gate(1), gate(2),
                  wbr_spec, wbr_spec, wbr_spec,
                  pl.BlockSpec((None, tk, d), lambda i, j: (layer, j, 0))],
        out_specs=pl.BlockSpec((tm, d), lambda i, j: (i, 0)),
        out_shape=jax.ShapeDtypeStruct((n, d), F32),
        compiler_params=_params(("parallel", "arbitrary")),
        name="merge",
    )(x, ya, yb, yc, proj, proj, proj, wa, wb, wc, wo)


def _xattn_kernel(x_ref, g_ref, wq_ref, kv_ref, wo_ref, o_ref, h_ref, *, tm):
    _rms_rows(x_ref, g_ref, h_ref, tm)
    q = jnp.dot(h_ref[...], wq_ref[...], preferred_element_type=F32).astype(BF16)
    xw = X_HEADS * X_HEAD_DIM
    scale = X_HEAD_DIM ** -0.5

    def score(hd):
        cs = slice(hd * X_HEAD_DIM, (hd + 1) * X_HEAD_DIM)
        return lax.dot_general(q[:, cs], kv_ref[:, cs], (((1,), (1,)), ((), ())),
                               preferred_element_type=F32) * scale

    def attend(hd, s):
        vh = kv_ref[:, xw + hd * X_HEAD_DIM:xw + (hd + 1) * X_HEAD_DIM]
        mx = jnp.max(s, axis=-1, keepdims=True)
        pe = jnp.exp(s - mx)
        inv = 1.0 / jnp.sum(pe, axis=-1, keepdims=True)
        return (jnp.dot(pe.astype(BF16), vh, preferred_element_type=F32) * inv).astype(BF16)

    outs = []
    s_next = score(0)
    for hd in range(X_HEADS):
        s_cur = s_next
        if hd + 1 < X_HEADS:
            s_next = score(hd + 1)
        outs.append(attend(hd, s_cur))
    oc = jnp.concatenate(outs, axis=1)
    o_ref[...] = x_ref[...] + jnp.dot(oc, wo_ref[...], preferred_element_type=F32)


def _xattn(x, g, wq, kv, wo, layer, *, seq, mem_len, tm):
    n, d = x.shape
    xw = X_HEADS * X_HEAD_DIM
    tiles_per_seq = _exact_div(seq, tm)
    return pl.pallas_call(
        functools.partial(_xattn_kernel, tm=tm),
        grid=(_exact_div(n, tm),),
        in_specs=[pl.BlockSpec((tm, d), lambda i: (i, 0)),
                  pl.BlockSpec((None, 1, d), lambda i: (layer, 0, 0)),
                  pl.BlockSpec((None, d, xw), lambda i: (layer, 0, 0)),
                  pl.BlockSpec((mem_len, 2 * xw), lambda i: (i // tiles_per_seq, layer)),
                  pl.BlockSpec((None, xw, d), lambda i: (layer, 0, 0))],
        out_specs=pl.BlockSpec((tm, d), lambda i: (i, 0)),
        out_shape=jax.ShapeDtypeStruct((n, d), F32),
        scratch_shapes=[pltpu.VMEM((tm, d), BF16)],
        compiler_params=_params(("parallel",)),
        name="xattn",
    )(x, g, wq, kv, wo)


def _ffn_kernel(x_ref, g_ref, fg_ref, wg_ref, wu_ref, wo_ref, o_ref, h_ref, *, tm, final_norm):
    @pl.when(pl.program_id(1) == 0)
    def _():
        _rms_rows(x_ref, g_ref, h_ref, tm)
        o_ref[...] = x_ref[...]

    h = h_ref[...]
    gt = jnp.dot(h, wg_ref[...], preferred_element_type=F32)
    up = jnp.dot(h, wu_ref[...], preferred_element_type=F32)
    a = (jax.nn.silu(gt) * up).astype(BF16)
    o_ref[...] += jnp.dot(a, wo_ref[...], preferred_element_type=F32)

    if final_norm:
        @pl.when(pl.program_id(1) == pl.num_programs(1) - 1)
        def _():
            _rms_rows(o_ref, fg_ref, o_ref, tm)


def _ffn_tile_w_in(w_in, tf):
    depth, d, two_dff = w_in.shape
    return jnp.transpose(w_in.reshape(depth, d, _exact_div(two_dff, tf), tf), (0, 2, 1, 3))


def _ffn(x, g, final_g, w_in_tiled, w_out, layer, *, tm, final_norm):
    n, d = x.shape
    tf = w_in_tiled.shape[-1]
    nf = _exact_div(w_out.shape[1], tf)
    assert w_in_tiled.shape[1] == 2 * nf
    return pl.pallas_call(
        functools.partial(_ffn_kernel, tm=tm, final_norm=final_norm),
        grid=(_exact_div(n, tm), nf),
        in_specs=[pl.BlockSpec((tm, d), lambda i, f: (i, 0)),
                  pl.BlockSpec((None, 1, d), lambda i, f: (layer, 0, 0)),
                  pl.BlockSpec((1, d), lambda i, f: (0, 0)),
                  pl.BlockSpec((None, None, d, tf), lambda i, f: (layer, f, 0, 0)),
                  pl.BlockSpec((None, None, d, tf), lambda i, f: (layer, nf + f, 0, 0)),
                  pl.BlockSpec((None, tf, d), lambda i, f: (layer, f, 0))],
        out_specs=pl.BlockSpec((tm, d), lambda i, f: (i, 0)),
        out_shape=jax.ShapeDtypeStruct((n, d), F32),
        scratch_shapes=[pltpu.VMEM((tm, d), BF16)],
        compiler_params=_params(("parallel", "arbitrary")),
        name="ffn",
    )(x, g, final_g, w_in_tiled, w_in_tiled, w_out)


def kernel(x, mem, norm_mix, w_in, w_gate, sinks, sgu_ln_g, sgu_ln_b, sgu_w, sgu_b, hgrn_lb, hgrn_norm,
           w_br_a, w_br_b, w_br_c, w_out, norm_x, mem_norm, w_xq, w_xkv, w_xo, norm_ffn, w_ffn_in,
           w_ffn_out, final_norm):
    batch, seq, d = x.shape
    depth = w_in.shape[0]
    mem_len = mem.shape[1]
    n = batch * seq

    (wqa, wka, wva, wub, wvb, wfc, wic, wqc, wgc) = jnp.split(
        w_in, [1024, 1280, 1536, 2560, 3584, 4608, 5632, 6656], axis=-1)
    w_all = jnp.concatenate([wqa, wub, wvb, wfc, wic, wqc, wgc, w_gate, wka, wva], axis=-1).astype(BF16)
    wa, wb, wc, wo = (w.astype(BF16) for w in (w_br_a, w_br_b, w_br_c, w_out))
    wxq, wxo = w_xq.astype(BF16), w_xo.astype(BF16)
    wxkv = jnp.concatenate([w_xkv[l] for l in range(depth)], axis=-1).astype(BF16)
    wfi, wfo = _ffn_tile_w_in(w_ffn_in.astype(BF16), FFN_TF), w_ffn_out.astype(BF16)
    sgu_bt = jnp.swapaxes(sgu_b, 1, 2)
    sm = jax.nn.softmax(hgrn_lb.astype(F32), axis=0)
    lb_all = jnp.cumsum(sm, axis=0) - sm[0:1]
    sgu_ln_g, sgu_ln_b, hgrn_norm, norm_x, norm_ffn = (
        p[:, None, :] for p in (sgu_ln_g, sgu_ln_b, hgrn_norm, norm_x, norm_ffn))

    xf = x.reshape(n, d)
    kv_all = _norm_matmul(mem.reshape(batch * mem_len, d), mem_norm.reshape(1, d), wxkv, 0,
                          tm=1024, tn=512)
    for l in range(depth):
        proj = _norm_matmul(xf, norm_mix[l:l + 1], w_all, l, tm=1024, tn=1536, activate=True)
        ya = _swa(proj, sinks[l], batch=batch, seq=seq, tq=512)
        yb = _sgu(proj, sgu_ln_g, sgu_ln_b, sgu_w, sgu_bt, l, ts=512)
        yc = _hgrn(proj, lb_all[l:l + 1], hgrn_norm, l, batch=batch, seq=seq, tt=512, hb=8)
        xf = _merge(xf, ya, yb, yc, proj, wa, wb, wc, wo, l, tm=512, tk=1024)
        xf = _xattn(xf, norm_x, wxq, kv_all, wxo, l, seq=seq, mem_len=mem_len, tm=512)
        xf = _ffn(xf, norm_ffn, final_norm.reshape(1, d), wfi, wfo, l, tm=1024,
                  final_norm=(l == depth - 1))
    return xf.reshape(batch, seq, d)
```

```python
import functools

import jax
import jax.numpy as jnp
from jax import lax
from jax.experimental import pallas as pl
from jax.experimental.pallas import tpu as pltpu

F32 = jnp.float32
BF16 = jnp.bfloat16
EPS = 1e-6

LANES = 128
V7X_VMEM_LIMIT = 56 * 1024 * 1024

ATTN_HEAD_DIM = 64
ATTN_Q_HEADS = 16
ATTN_KV_HEADS = 4
WINDOW = 128
SGU_CHUNK = 128
SGU_GROUPS = 8
HGRN_HEADS = 8
HGRN_HEAD_DIM = 128
HGRN_CHUNK = 64
HGRN_SUB = 16
X_HEADS = 4
X_HEAD_DIM = 128

COL_GATE, COL_QA, COL_UB, COL_VB, COL_FC, COL_IC, COL_QC, COL_GC, COL_KV, COL_END = (
    0, 6144, 7168, 8192, 9216, 10240, 11264, 12288, 13312, 13824)
ACT_SUB = 512


def _exact_div(a, b):
    assert a % b == 0, (a, b)
    return a // b


def _params(sem):
    return pltpu.CompilerParams(dimension_semantics=sem, vmem_limit_bytes=V7X_VMEM_LIMIT)


def _rms_rows(x_ref, g_ref, h_ref, rows):
    slab = 64
    g = g_ref[...]

    def body(i, c):
        r = pl.ds(pl.multiple_of(i * slab, slab), slab)
        x = x_ref[r, :]
        ms = jnp.mean(x * x, axis=-1, keepdims=True)
        h_ref[r, :] = (x * lax.rsqrt(ms + EPS) * g).astype(h_ref.dtype)
        return c

    lax.fori_loop(0, _exact_div(rows, slab), body, 0, unroll=4)


def _in_range(c0, lo_hi):
    _exact_div(lo_hi[0], ACT_SUB), _exact_div(lo_hi[1], ACT_SUB)
    return (c0 >= lo_hi[0]) & (c0 < lo_hi[1])


def _projection_activation(x, c0):
    is_sig = _in_range(c0, (COL_GATE, COL_QA))
    mul = jnp.where(_in_range(c0, (COL_QA, COL_UB)), ATTN_HEAD_DIM ** -0.5, 1.0)
    return jnp.where(is_sig, 0.5 + 0.5 * jnp.tanh(0.5 * x), x * mul)


def _norm_matmul_kernel(x_ref, g_ref, w_ref, o_ref, h_ref, *, activate, tm, tn):
    j = pl.program_id(1)

    @pl.when(j == 0)
    def _():
        _rms_rows(x_ref, g_ref, h_ref, tm)

    if not activate:
        o_ref[...] = jnp.dot(h_ref[...], w_ref[...], preferred_element_type=F32).astype(o_ref.dtype)
    else:
        h = h_ref[...]
        for s in range(_exact_div(tn, ACT_SUB)):
            cs = slice(s * ACT_SUB, (s + 1) * ACT_SUB)
            acc = jnp.dot(h, w_ref[:, cs], preferred_element_type=F32)
            o_ref[:, cs] = _projection_activation(acc, j * tn + s * ACT_SUB).astype(o_ref.dtype)


def _norm_matmul(x, g, w, layer, *, tm, tn, activate=False):
    n, d = x.shape
    nout = w.shape[-1]
    kern = functools.partial(_norm_matmul_kernel, activate=activate, tm=tm, tn=tn)
    if w.ndim == 3:
        w_spec = pl.BlockSpec((None, d, tn), lambda i, j: (layer, 0, j))
    else:
        w_spec = pl.BlockSpec((d, tn), lambda i, j: (0, j))
    return pl.pallas_call(
        kern,
        grid=(_exact_div(n, tm), _exact_div(nout, tn)),
        in_specs=[pl.BlockSpec((tm, d), lambda i, j: (i, 0)),
                  pl.BlockSpec((1, d), lambda i, j: (0, 0)),
                  w_spec],
        out_specs=pl.BlockSpec((tm, tn), lambda i, j: (i, j)),
        out_shape=jax.ShapeDtypeStruct((n, nout), BF16),
        scratch_shapes=[pltpu.VMEM((tm, d), BF16)],
        compiler_params=_params(("parallel", "arbitrary")),
        name="norm_matmul",
    )(x, g, w)


def _swap_halves(t):
    return jnp.concatenate([t[:, 64:], t[:, :64]], axis=1)


def _swa_kernel(sink_ref, q_ref, kvc_ref, kvp_ref, o_ref, kv_buf, *, tq):
    t_idx = pl.program_id(1)
    kv_buf[0:WINDOW, :] = kvp_ref[...]
    kv_buf[WINDOW:, :] = kvc_ref[...]

    row = lax.broadcasted_iota(jnp.int32, (WINDOW, 2 * WINDOW), 0)
    col = lax.broadcasted_iota(jnp.int32, (WINDOW, 2 * WINDOW), 1)
    band = (col > row) & (col <= row + WINDOW)
    lane = lax.broadcasted_iota(jnp.int32, (2 * WINDOW, LANES), 1)
    lo = lane < 64
    out_lane = lax.broadcasted_iota(jnp.int32, (WINDOW, LANES), 1)

    def blk(i, c):
        r0 = pl.multiple_of(i * WINDOW, WINDOW)
        cmin = jnp.where((t_idx == 0) & (i == 0), WINDOW, 0)
        valid = band & (col >= cmin)
        kcats, vcats = [], []
        for m in range(2):
            kt = kv_buf[pl.ds(r0, 2 * WINDOW), m * LANES:(m + 1) * LANES]
            vt = kv_buf[pl.ds(r0, 2 * WINDOW), 256 + m * LANES:256 + (m + 1) * LANES]
            ks, vs = _swap_halves(kt), _swap_halves(vt)
            zero = jnp.zeros_like(kt)
            for e in range(2):
                k_lo = jnp.where(lo, kt if e == 0 else ks, zero)
                k_hi = jnp.where(lo, zero, ks if e == 0 else kt)
                v_lo = jnp.where(lo, vt if e == 0 else vs, zero)
                v_hi = jnp.where(lo, zero, vs if e == 0 else vt)
                kcats.append(jnp.concatenate([k_lo, k_hi], axis=0))
                vcats.append(jnp.concatenate([v_lo, v_hi], axis=0))

        def score(p):
            qt = q_ref[pl.ds(r0, WINDOW), p * LANES:(p + 1) * LANES]
            return lax.dot_general(qt, kcats[p // 2], (((1,), (1,)), ((), ())),
                                   preferred_element_type=F32)

        def attend(p, s):
            probs, invs = [], []
            for hh in range(2):
                sink = sink_ref[2 * p + hh]
                sh = jnp.where(valid, s[:, hh * 256:(hh + 1) * 256], -jnp.inf)
                mx = jnp.maximum(jnp.max(sh, axis=-1, keepdims=True), sink)
                pe = jnp.exp(sh - mx)
                den = jnp.sum(pe, axis=-1, keepdims=True) + jnp.exp(sink - mx)
                probs.append(pe.astype(BF16))
                invs.append(1.0 / den)
            pcat = jnp.concatenate(probs, axis=1)
            o = jnp.dot(pcat, vcats[p // 2], preferred_element_type=F32)
            o = o * jnp.where(out_lane < 64, invs[0], invs[1])
            o_ref[pl.ds(r0, WINDOW), p * LANES:(p + 1) * LANES] = o.astype(o_ref.dtype)

        n_tiles = ATTN_Q_HEADS // 2
        s_next = score(0)
        for p in range(n_tiles):
            s_cur = s_next
            if p + 1 < n_tiles:
                s_next = score(p + 1)
            attend(p, s_cur)
        return c

    lax.fori_loop(0, _exact_div(tq, WINDOW), blk, 0)


def _swa(proj, sinks_l, *, batch, seq, tq):
    n = proj.shape[0]
    nt = _exact_div(seq, tq)
    qw = ATTN_Q_HEADS * ATTN_HEAD_DIM
    kvw = 2 * ATTN_KV_HEADS * ATTN_HEAD_DIM
    kv_blk = _exact_div(COL_KV, kvw)
    bpt = _exact_div(tq, WINDOW)

    def prev_map(b, t):
        return (jnp.maximum(b * _exact_div(seq, WINDOW) + t * bpt - 1, 0), kv_blk)

    return pl.pallas_call(
        functools.partial(_swa_kernel, tq=tq),
        grid=(batch, nt),
        in_specs=[pl.BlockSpec(memory_space=pltpu.SMEM),
                  pl.BlockSpec((tq, qw), lambda b, t: (b * nt + t, _exact_div(COL_QA, qw))),
                  pl.BlockSpec((tq, kvw), lambda b, t: (b * nt + t, kv_blk)),
                  pl.BlockSpec((WINDOW, kvw), prev_map)],
        out_specs=pl.BlockSpec((tq, qw), lambda b, t: (b * nt + t, 0)),
        out_shape=jax.ShapeDtypeStruct((n, qw), BF16),
        scratch_shapes=[pltpu.VMEM((tq + WINDOW, kvw), BF16)],
        compiler_params=_params(("parallel", "parallel")),
        name="swa",
    )(sinks_l, proj, proj, proj)


def _sgu_kernel(u_ref, v_ref, lng_ref, lnb_ref, w_ref, bt_ref, o_ref, *, ts):
    row = lax.broadcasted_iota(jnp.int32, (SGU_CHUNK, SGU_CHUNK), 0)
    col = lax.broadcasted_iota(jnp.int32, (SGU_CHUNK, SGU_CHUNK), 1)
    tril = row >= col
    lng = lng_ref[...]
    lnb = lnb_ref[...]

    def chunk(c, carry):
        r = pl.ds(pl.multiple_of(c * SGU_CHUNK, SGU_CHUNK), SGU_CHUNK)
        v = jax.nn.gelu(v_ref[r, :].astype(F32))
        mu = jnp.mean(v, axis=-1, keepdims=True)
        vc = v - mu
        var = jnp.mean(vc * vc, axis=-1, keepdims=True)
        vn = (vc * lax.rsqrt(var + EPS) * lng + lnb).astype(BF16)
        for g in range(SGU_GROUPS):
            cs = slice(g * LANES, (g + 1) * LANES)
            w = jnp.where(tril, w_ref[g], 0.0).astype(BF16)
            mixed = jnp.dot(w, vn[:, cs], preferred_element_type=F32) + bt_ref[:, g:g + 1]
            u = jax.nn.gelu(u_ref[r, cs].astype(F32))
            o_ref[r, cs] = (u * mixed).astype(o_ref.dtype)
        return carry

    lax.fori_loop(0, _exact_div(ts, SGU_CHUNK), chunk, 0)


def _sgu(proj, ln_g, ln_b, w, bt, layer, *, ts):
    n = proj.shape[0]
    width = SGU_GROUPS * LANES
    return pl.pallas_call(
        functools.partial(_sgu_kernel, ts=ts),
        grid=(_exact_div(n, ts),),
        in_specs=[pl.BlockSpec((ts, width), lambda i: (i, _exact_div(COL_UB, width))),
                  pl.BlockSpec((ts, width), lambda i: (i, _exact_div(COL_VB, width))),
                  pl.BlockSpec((None, 1, width), lambda i: (layer, 0, 0)),
                  pl.BlockSpec((None, 1, width), lambda i: (layer, 0, 0)),
                  pl.BlockSpec((None, SGU_GROUPS, SGU_CHUNK, SGU_CHUNK), lambda i: (layer, 0, 0, 0)),
                  pl.BlockSpec((None, SGU_CHUNK, SGU_GROUPS), lambda i: (layer, 0, 0))],
        out_specs=pl.BlockSpec((ts, width), lambda i: (i, 0)),
        out_shape=jax.ShapeDtypeStruct((n, width), BF16),
        compiler_params=_params(("parallel",)),
        name="sgu",
    )(proj, proj, ln_g, ln_b, w, bt)


def _hgrn_kernel(f_ref, i_ref, q_ref, g_ref, lb_ref, ng_ref, o_ref, state_ref, kqb_ref, a_ref, *, tt, hb):
    C, SUB = HGRN_CHUNK, HGRN_SUB

    @pl.when(pl.program_id(2) == 0)
    def _():
        state_ref[...] = jnp.zeros_like(state_ref)

    ng = ng_ref[...]
    r64 = lax.broadcasted_iota(jnp.int32, (C, C), 0)
    c64 = lax.broadcasted_iota(jnp.int32, (C, C), 1)
    tril = jnp.where(r64 >= c64, 1.0, 0.0).astype(BF16)
    t_i = lax.broadcasted_iota(jnp.int32, (C, LANES), 0)
    l_i = lax.broadcasted_iota(jnp.int32, (C, LANES), 1)
    own_half = ((l_i // C) == ((t_i // SUB) % 2)) & ((l_i % C) <= t_i)
    valid0 = own_half & (t_i < 2 * SUB)
    valid1 = own_half & (t_i >= 2 * SUB)

    def gates(r, h):
        cs = slice(h * LANES, (h + 1) * LANES)
        lb = lb_ref[:, cs]
        half_span = 0.5 * (1.0 - lb)
        f = (lb + half_span) + half_span * jnp.tanh(0.5 * f_ref[r, cs].astype(F32))
        lf = jnp.log2(f)
        k = 1.0 - f
        x_q = q_ref[r, cs].astype(F32)
        qf = x_q * ((0.5 * HGRN_HEAD_DIM ** -0.5) + (0.5 * HGRN_HEAD_DIM ** -0.5) * jnp.tanh(0.5 * x_q))
        hi = lf.astype(BF16)
        lo = (lf - hi.astype(F32)).astype(BF16)
        b = (jnp.dot(tril, hi, preferred_element_type=F32)
             + jnp.dot(tril, lo, preferred_element_type=F32))
        return k, qf, b

    def scores(k, qf, b):
        refs = [jnp.zeros((1, LANES), F32)] + [b[SUB * i - 1:SUB * i, :] for i in range(1, _exact_div(C, SUB))]
        q_rel = jnp.concatenate(
            [(qf[SUB * i:SUB * (i + 1)] * jnp.exp2(b[SUB * i:SUB * (i + 1)] - rf)).astype(BF16)
             for i, rf in enumerate(refs)], axis=0)
        blocks = []
        for i, rf in enumerate(refs):
            rows = SUB * (i + 1)
            blocks.append((k[:rows] * jnp.exp2(rf - b[:rows])).astype(BF16))
            if rows < C:
                blocks.append(jnp.zeros((C - rows, LANES), BF16))
        kcat = jnp.concatenate(blocks, axis=0)
        sc = lax.dot_general(q_rel, kcat, (((1,), (1,)), ((), ())), preferred_element_type=F32)
        return jnp.where(valid0, sc[:, :LANES], jnp.where(valid1, sc[:, LANES:], 0.0))

    def outputs(r, h, k, qf, b, a):
        cs = slice(h * LANES, (h + 1) * LANES)
        v = i_ref[r, cs]
        b_last = b[C - 1:C, :]
        q_abs = (qf * jnp.exp2(b)).astype(BF16)
        vv = jnp.concatenate([v, v], axis=0)
        st = state_ref[h]
        o = (lax.dot_general(q_abs, st.astype(BF16), (((1,), (1,)), ((), ())), preferred_element_type=F32)
             + jnp.dot(a.astype(BF16), vv, preferred_element_type=F32))
        k_end = (k * jnp.exp2(b_last - b)).astype(BF16)
        state_ref[h] = st * jnp.exp2(b_last) + lax.dot_general(
            v, k_end, (((0,), (0,)), ((), ())), preferred_element_type=F32)
        on = o * lax.rsqrt(jnp.mean(o * o, axis=-1, keepdims=True) + EPS) * ng
        x_g = g_ref[r, cs].astype(F32)
        o_ref[r, cs] = (on * (x_g * (0.5 + 0.5 * jnp.tanh(0.5 * x_g)))).astype(o_ref.dtype)

    def rows(c):
        return pl.ds(pl.multiple_of(c * C, C), C)

    def stage_scores(r):
        kqb = [gates(r, h) for h in range(hb)]
        a = [scores(*t) for t in kqb]
        for h in range(hb):
            for j, val in enumerate(kqb[h]):
                kqb_ref[j, h] = val
            a_ref[h] = a[h].astype(BF16)

    n_chunks = _exact_div(tt, C)
    stage_scores(rows(0))

    def chunk(c, carry):
        staged = [(kqb_ref[0, h], kqb_ref[1, h], kqb_ref[2, h], a_ref[h]) for h in range(hb)]
        for h in range(hb):
            outputs(rows(c), h, *staged[h])
        stage_scores(rows(jnp.minimum(c + 1, n_chunks - 1)))
        return carry

    lax.fori_loop(0, n_chunks, chunk, 0)


def _hgrn(proj, lb_l, norm_g, layer, *, batch, seq, tt, hb):
    n = proj.shape[0]
    nt = _exact_div(seq, tt)
    width = HGRN_HEADS * HGRN_HEAD_DIM
    bw = hb * LANES

    def col(off):
        return lambda b, h, t: (b * nt + t, _exact_div(off, bw) + h)

    return pl.pallas_call(
        functools.partial(_hgrn_kernel, tt=tt, hb=hb),
        grid=(batch, _exact_div(HGRN_HEADS, hb), nt),
        in_specs=[pl.BlockSpec((tt, bw), col(COL_FC)),
                  pl.BlockSpec((tt, bw), col(COL_IC)),
                  pl.BlockSpec((tt, bw), col(COL_QC)),
                  pl.BlockSpec((tt, bw), col(COL_GC)),
                  pl.BlockSpec((1, bw), lambda b, h, t: (0, h)),
                  pl.BlockSpec((None, 1, LANES), lambda b, h, t: (layer, 0, 0))],
        out_specs=pl.BlockSpec((tt, bw), lambda b, h, t: (b * nt + t, h)),
        out_shape=jax.ShapeDtypeStruct((n, width), BF16),
        scratch_shapes=[pltpu.VMEM((hb, HGRN_HEAD_DIM, HGRN_HEAD_DIM), F32),
                        pltpu.VMEM((3, hb, HGRN_CHUNK, LANES), F32),
                        pltpu.VMEM((hb, HGRN_CHUNK, LANES), BF16)],
        compiler_params=_params(("parallel", "parallel", "arbitrary")),
        name="hgrn2",
    )(proj, proj, proj, proj, lb_l, norm_g)


def _merge_kernel(x_ref, ya_ref, yb_ref, yc_ref, ga_ref, gb_ref, gc_ref,
                  wa_ref, wb_ref, wc_ref, wo_ref, o_ref, *, tk):
    d = o_ref.shape[1]
    ya, yb, yc = ya_ref[...], yb_ref[...], yc_ref[...]

    def merged(c):
        cs = slice(c * tk, (c + 1) * tk)
        m = (ga_ref[:, cs].astype(F32) * jnp.dot(ya, wa_ref[:, cs], preferred_element_type=F32)
             + gb_ref[:, cs].astype(F32) * jnp.dot(yb, wb_ref[:, cs], preferred_element_type=F32)
             + gc_ref[:, cs].astype(F32) * jnp.dot(yc, wc_ref[:, cs], preferred_element_type=F32))
        return m.astype(BF16)

    n_groups = _exact_div(d, tk)
    acc = x_ref[...]
    m_next = merged(0)
    for c in range(n_groups):
        m_cur = m_next
        if c + 1 < n_groups:
            m_next = merged(c + 1)
        acc = acc + jnp.dot(m_cur, wo_ref[c * tk:(c + 1) * tk, :], preferred_element_type=F32)
    o_ref[...] = acc


def _merge(x, ya, yb, yc, proj, wa, wb, wc, wo, layer, *, tm, tk):
    n, d = x.shape
    wbr = ya.shape[1]
    g0 = _exact_div(COL_GATE, d)

    def gate(br):
        return pl.BlockSpec((tm, d), lambda i: (i, g0 + br))

    def resident(rows):
        return pl.BlockSpec((None, rows, d), lambda i: (layer, 0, 0), pipeline_mode=pl.Buffered(1))

    y_spec = pl.BlockSpec((tm, wbr), lambda i: (i, 0))
    return pl.pallas_call(
        functools.partial(_merge_kernel, tk=tk),
        grid=(_exact_div(n, tm),),
        in_specs=[pl.BlockSpec((tm, d), lambda i: (i, 0)),
                  y_spec, y_spec, y_spec, gate(0), gate(1), gate(2),
                  resident(wbr), resident(wbr), resident(wbr), resident(d)],
        out_specs=pl.BlockSpec((tm, d), lambda i: (i, 0)),
        out_shape=jax.ShapeDtypeStruct((n, d), F32),
        compiler_params=_params(("parallel",)),
        name="merge",
    )(x, ya, yb, yc, proj, proj, proj, wa, wb, wc, wo)


def _xattn_kernel(x_ref, g_ref, wq_ref, kv_ref, wo_ref, o_ref, h_ref, *, tm):
    _rms_rows(x_ref, g_ref, h_ref, tm)
    q = jnp.dot(h_ref[...], wq_ref[...], preferred_element_type=F32).astype(BF16)
    xw = X_HEADS * X_HEAD_DIM
    scale = X_HEAD_DIM ** -0.5

    def score(hd):
        cs = slice(hd * X_HEAD_DIM, (hd + 1) * X_HEAD_DIM)
        return lax.dot_general(q[:, cs], kv_ref[:, cs], (((1,), (1,)), ((), ())),
                               preferred_element_type=F32) * scale

    def attend(hd, s):
        vh = kv_ref[:, xw + hd * X_HEAD_DIM:xw + (hd + 1) * X_HEAD_DIM]
        mx = jnp.max(s, axis=-1, keepdims=True)
        pe = jnp.exp(s - mx)
        inv = 1.0 / jnp.sum(pe, axis=-1, keepdims=True)
        return (jnp.dot(pe.astype(BF16), vh, preferred_element_type=F32) * inv).astype(BF16)

    outs = []
    s_next = score(0)
    for hd in range(X_HEADS):
        s_cur = s_next
        if hd + 1 < X_HEADS:
            s_next = score(hd + 1)
        outs.append(attend(hd, s_cur))
    oc = jnp.concatenate(outs, axis=1)
    o_ref[...] = x_ref[...] + jnp.dot(oc, wo_ref[...], preferred_element_type=F32)


def _xattn(x, g, wq, kv, wo, layer, *, seq, mem_len, tm):
    n, d = x.shape
    xw = X_HEADS * X_HEAD_DIM
    tiles_per_seq = _exact_div(seq, tm)
    return pl.pallas_call(
        functools.partial(_xattn_kernel, tm=tm),
        grid=(_exact_div(n, tm),),
        in_specs=[pl.BlockSpec((tm, d), lambda i: (i, 0)),
                  pl.BlockSpec((None, 1, d), lambda i: (layer, 0, 0)),
                  pl.BlockSpec((None, d, xw), lambda i: (layer, 0, 0)),
                  pl.BlockSpec((mem_len, 2 * xw), lambda i: (i // tiles_per_seq, layer)),
                  pl.BlockSpec((None, xw, d), lambda i: (layer, 0, 0))],
        out_specs=pl.BlockSpec((tm, d), lambda i: (i, 0)),
        out_shape=jax.ShapeDtypeStruct((n, d), F32),
        scratch_shapes=[pltpu.VMEM((tm, d), BF16)],
        compiler_params=_params(("parallel",)),
        name="xattn",
    )(x, g, wq, kv, wo)


def _ffn_kernel(x_ref, g_ref, fg_ref, wg_ref, wu_ref, wo_ref, o_ref, h_ref, *, tm, final_norm):
    @pl.when(pl.program_id(1) == 0)
    def _():
        _rms_rows(x_ref, g_ref, h_ref, tm)
        o_ref[...] = x_ref[...]

    h = h_ref[...]
    gt = jnp.dot(h, wg_ref[...], preferred_element_type=F32)
    up = jnp.dot(h, wu_ref[...], preferred_element_type=F32)
    a = (jax.nn.silu(gt) * up).astype(BF16)
    o_ref[...] += jnp.dot(a, wo_ref[...], preferred_element_type=F32)

    if final_norm:
        @pl.when(pl.program_id(1) == pl.num_programs(1) - 1)
        def _():
            _rms_rows(o_ref, fg_ref, o_ref, tm)


def _ffn(x, g, final_g, w_in, w_out, layer, *, tm, tf, final_norm):
    n, d = x.shape
    nf = _exact_div(w_out.shape[1], tf)
    return pl.pallas_call(
        functools.partial(_ffn_kernel, tm=tm, final_norm=final_norm),
        grid=(_exact_div(n, tm), nf),
        in_specs=[pl.BlockSpec((tm, d), lambda i, f: (i, 0)),
                  pl.BlockSpec((None, 1, d), lambda i, f: (layer, 0, 0)),
                  pl.BlockSpec((1, d), lambda i, f: (0, 0)),
                  pl.BlockSpec((None, d, tf), lambda i, f: (layer, 0, f)),
                  pl.BlockSpec((None, d, tf), lambda i, f: (layer, 0, nf + f)),
                  pl.BlockSpec((None, tf, d), lambda i, f: (layer, f, 0))],
        out_specs=pl.BlockSpec((tm, d), lambda i, f: (i, 0)),
        out_shape=jax.ShapeDtypeStruct((n, d), F32),
        scratch_shapes=[pltpu.VMEM((tm, d), BF16)],
        compiler_params=_params(("parallel", "arbitrary")),
        name="ffn",
    )(x, g, final_g, w_in, w_in, w_out)


def kernel(x, mem, norm_mix, w_in, w_gate, sinks, sgu_ln_g, sgu_ln_b, sgu_w, sgu_b, hgrn_lb, hgrn_norm,
           w_br_a, w_br_b, w_br_c, w_out, norm_x, mem_norm, w_xq, w_xkv, w_xo, norm_ffn, w_ffn_in,
           w_ffn_out, final_norm):
    batch, seq, d = x.shape
    depth = w_in.shape[0]
    mem_len = mem.shape[1]
    n = batch * seq

    (wqa, wka, wva, wub, wvb, wfc, wic, wqc, wgc) = jnp.split(
        w_in, [1024, 1280, 1536, 2560, 3584, 4608, 5632, 6656], axis=-1)
    w_all = jnp.concatenate([w_gate, wqa, wub, wvb, wfc, wic, wqc, wgc, wka, wva], axis=-1).astype(BF16)
    wa, wb, wc, wo = (w.astype(BF16) for w in (w_br_a, w_br_b, w_br_c, w_out))
    wxq, wxo = w_xq.astype(BF16), w_xo.astype(BF16)
    wxkv = jnp.concatenate([w_xkv[l] for l in range(depth)], axis=-1).astype(BF16)
    wfi, wfo = w_ffn_in.astype(BF16), w_ffn_out.astype(BF16)
    sgu_bt = jnp.swapaxes(sgu_b, 1, 2)
    sm = jax.nn.softmax(hgrn_lb.astype(F32), axis=0)
    lb_all = jnp.cumsum(sm, axis=0) - sm[0:1]
    sgu_ln_g, sgu_ln_b, hgrn_norm, norm_x, norm_ffn = (
        p[:, None, :] for p in (sgu_ln_g, sgu_ln_b, hgrn_norm, norm_x, norm_ffn))

    xf = x.reshape(n, d)
    kv_all = _norm_matmul(mem.reshape(batch * mem_len, d), mem_norm.reshape(1, d), wxkv, 0,
                          tm=1024, tn=512)
    for l in range(depth):
        proj = _norm_matmul(xf, norm_mix[l:l + 1], w_all, l, tm=1024, tn=1536, activate=True)
        ya = _swa(proj, sinks[l], batch=batch, seq=seq, tq=512)
        yb = _sgu(proj, sgu_ln_g, sgu_ln_b, sgu_w, sgu_bt, l, ts=512)
        yc = _hgrn(proj, lb_all[l:l + 1], hgrn_norm, l, batch=batch, seq=seq, tt=seq, hb=8)
        xf = _merge(xf, ya, yb, yc, proj, wa, wb, wc, wo, l, tm=256, tk=512)
        xf = _xattn(xf, norm_x, wxq, kv_all, wxo, l, seq=seq, mem_len=mem_len, tm=512)
        xf = _ffn(xf, norm_ffn, final_norm.reshape(1, d), wfi, wfo, l, tm=1024, tf=512,
                  final_norm=(l == depth - 1))
    return xf.reshape(batch, seq, d)
```

```python
import functools

import jax
import jax.numpy as jnp
from jax import lax
from jax.experimental import pallas as pl
from jax.experimental.pallas import tpu as pltpu

F32 = jnp.float32
BF16 = jnp.bfloat16
EPS = 1e-6

LANES = 128
V7X_VMEM_LIMIT = 56 * 1024 * 1024

ATTN_HEAD_DIM = 64
ATTN_Q_HEADS = 16
ATTN_KV_HEADS = 4
WINDOW = 128
SGU_CHUNK = 128
SGU_GROUPS = 8
HGRN_HEADS = 8
HGRN_HEAD_DIM = 128
HGRN_CHUNK = 64
HGRN_SUB = 16
X_HEADS = 4
X_HEAD_DIM = 128

COL_GATE, COL_QA, COL_UB, COL_VB, COL_FC, COL_IC, COL_QC, COL_GC, COL_KV, COL_END = (
    0, 6144, 7168, 8192, 9216, 10240, 11264, 12288, 13312, 13824)
ACT_SUB = 512


def _exact_div(a, b):
    assert a % b == 0, (a, b)
    return a // b


def _params(sem):
    return pltpu.CompilerParams(dimension_semantics=sem, vmem_limit_bytes=V7X_VMEM_LIMIT)


def _rms_rows(x_ref, g_ref, h_ref, rows, copy_ref=None):
    slab = 64
    g = g_ref[...]

    def body(i, c):
        r = pl.ds(pl.multiple_of(i * slab, slab), slab)
        x = x_ref[r, :]
        if copy_ref is not None:
            copy_ref[r, :] = x
        ms = jnp.mean(x * x, axis=-1, keepdims=True)
        h_ref[r, :] = (x * lax.rsqrt(ms + EPS) * g).astype(h_ref.dtype)
        return c

    lax.fori_loop(0, _exact_div(rows, slab), body, 0, unroll=4)


def _in_range(c0, lo_hi):
    _exact_div(lo_hi[0], ACT_SUB), _exact_div(lo_hi[1], ACT_SUB)
    return (c0 >= lo_hi[0]) & (c0 < lo_hi[1])


def _projection_activation(x, c0):
    is_sig = _in_range(c0, (COL_GATE, COL_QA))
    mul = jnp.where(_in_range(c0, (COL_QA, COL_UB)), ATTN_HEAD_DIM ** -0.5, 1.0)
    return jnp.where(is_sig, 0.5 + 0.5 * jnp.tanh(0.5 * x), x * mul)


def _norm_matmul_kernel(x_ref, g_ref, w_ref, o_ref, h_ref, *, activate, tm, tn):
    j = pl.program_id(1)

    @pl.when(j == 0)
    def _():
        _rms_rows(x_ref, g_ref, h_ref, tm)

    if not activate:
        o_ref[...] = jnp.dot(h_ref[...], w_ref[...], preferred_element_type=F32).astype(o_ref.dtype)
    else:
        h = h_ref[...]
        for s in range(_exact_div(tn, ACT_SUB)):
            cs = slice(s * ACT_SUB, (s + 1) * ACT_SUB)
            acc = jnp.dot(h, w_ref[:, cs], preferred_element_type=F32)
            o_ref[:, cs] = _projection_activation(acc, j * tn + s * ACT_SUB).astype(o_ref.dtype)


def _norm_matmul(x, g, w, layer, *, tm, tn, activate=False):
    n, d = x.shape
    nout = w.shape[-1]
    kern = functools.partial(_norm_matmul_kernel, activate=activate, tm=tm, tn=tn)
    if w.ndim == 3:
        w_spec = pl.BlockSpec((None, d, tn), lambda i, j: (layer, 0, j))
    else:
        w_spec = pl.BlockSpec((d, tn), lambda i, j: (0, j))
    return pl.pallas_call(
        kern,
        grid=(_exact_div(n, tm), _exact_div(nout, tn)),
        in_specs=[pl.BlockSpec((tm, d), lambda i, j: (i, 0)),
                  pl.BlockSpec((1, d), lambda i, j: (0, 0)),
                  w_spec],
        out_specs=pl.BlockSpec((tm, tn), lambda i, j: (i, j)),
        out_shape=jax.ShapeDtypeStruct((n, nout), BF16),
        scratch_shapes=[pltpu.VMEM((tm, d), BF16)],
        compiler_params=_params(("parallel", "arbitrary")),
        name="norm_matmul",
    )(x, g, w)


def _swap_halves(t):
    return jnp.concatenate([t[:, 64:], t[:, :64]], axis=1)


def _swa_kernel(sink_ref, q_ref, kvc_ref, kvp_ref, o_ref, kv_buf, *, tq):
    t_idx = pl.program_id(1)
    kv_buf[0:WINDOW, :] = kvp_ref[...]
    kv_buf[WINDOW:, :] = kvc_ref[...]

    row = lax.broadcasted_iota(jnp.int32, (WINDOW, 2 * WINDOW), 0)
    col = lax.broadcasted_iota(jnp.int32, (WINDOW, 2 * WINDOW), 1)
    band = (col > row) & (col <= row + WINDOW)
    lane = lax.broadcasted_iota(jnp.int32, (2 * WINDOW, LANES), 1)
    lo = lane < 64
    out_lane = lax.broadcasted_iota(jnp.int32, (WINDOW, LANES), 1)

    def blk(i, c):
        r0 = pl.multiple_of(i * WINDOW, WINDOW)
        cmin = jnp.where((t_idx == 0) & (i == 0), WINDOW, 0)
        valid = band & (col >= cmin)
        kcats, vcats = [], []
        for m in range(2):
            kt = kv_buf[pl.ds(r0, 2 * WINDOW), m * LANES:(m + 1) * LANES]
            vt = kv_buf[pl.ds(r0, 2 * WINDOW), 256 + m * LANES:256 + (m + 1) * LANES]
            ks, vs = _swap_halves(kt), _swap_halves(vt)
            zero = jnp.zeros_like(kt)
            for e in range(2):
                k_lo = jnp.where(lo, kt if e == 0 else ks, zero)
                k_hi = jnp.where(lo, zero, ks if e == 0 else kt)
                v_lo = jnp.where(lo, vt if e == 0 else vs, zero)
                v_hi = jnp.where(lo, zero, vs if e == 0 else vt)
                kcats.append(jnp.concatenate([k_lo, k_hi], axis=0))
                vcats.append(jnp.concatenate([v_lo, v_hi], axis=0))

        def score(p):
            qt = q_ref[pl.ds(r0, WINDOW), p * LANES:(p + 1) * LANES]
            return lax.dot_general(qt, kcats[p // 2], (((1,), (1,)), ((), ())),
                                   preferred_element_type=F32)

        def attend(p, s):
            probs, invs = [], []
            for hh in range(2):
                sink = sink_ref[2 * p + hh]
                sh = jnp.where(valid, s[:, hh * 256:(hh + 1) * 256], -jnp.inf)
                mx = jnp.maximum(jnp.max(sh, axis=-1, keepdims=True), sink)
                pe = jnp.exp(sh - mx)
                den = jnp.sum(pe, axis=-1, keepdims=True) + jnp.exp(sink - mx)
                probs.append(pe.astype(BF16))
                invs.append(1.0 / den)
            pcat = jnp.concatenate(probs, axis=1)
            o = jnp.dot(pcat, vcats[p // 2], preferred_element_type=F32)
            o = o * jnp.where(out_lane < 64, invs[0], invs[1])
            o_ref[pl.ds(r0, WINDOW), p * LANES:(p + 1) * LANES] = o.astype(o_ref.dtype)

        n_tiles = ATTN_Q_HEADS // 2
        ahead = 2
        pending = [score(p) for p in range(ahead)]
        for p in range(n_tiles):
            if p + ahead < n_tiles:
                pending.append(score(p + ahead))
            attend(p, pending.pop(0))
        return c

    lax.fori_loop(0, _exact_div(tq, WINDOW), blk, 0)


def _swa(proj, sinks_l, *, batch, seq, tq):
    n = proj.shape[0]
    nt = _exact_div(seq, tq)
    qw = ATTN_Q_HEADS * ATTN_HEAD_DIM
    kvw = 2 * ATTN_KV_HEADS * ATTN_HEAD_DIM
    kv_blk = _exact_div(COL_KV, kvw)
    bpt = _exact_div(tq, WINDOW)

    def prev_map(b, t):
        return (jnp.maximum(b * _exact_div(seq, WINDOW) + t * bpt - 1, 0), kv_blk)

    return pl.pallas_call(
        functools.partial(_swa_kernel, tq=tq),
        grid=(batch, nt),
        in_specs=[pl.BlockSpec(memory_space=pltpu.SMEM),
                  pl.BlockSpec((tq, qw), lambda b, t: (b * nt + t, _exact_div(COL_QA, qw))),
                  pl.BlockSpec((tq, kvw), lambda b, t: (b * nt + t, kv_blk)),
                  pl.BlockSpec((WINDOW, kvw), prev_map)],
        out_specs=pl.BlockSpec((tq, qw), lambda b, t: (b * nt + t, 0)),
        out_shape=jax.ShapeDtypeStruct((n, qw), BF16),
        scratch_shapes=[pltpu.VMEM((tq + WINDOW, kvw), BF16)],
        compiler_params=_params(("parallel", "parallel")),
        name="swa",
    )(sinks_l, proj, proj, proj)


def _sgu_kernel(u_ref, v_ref, lng_ref, lnb_ref, w_ref, bt_ref, o_ref, *, ts):
    row = lax.broadcasted_iota(jnp.int32, (SGU_CHUNK, SGU_CHUNK), 0)
    col = lax.broadcasted_iota(jnp.int32, (SGU_CHUNK, SGU_CHUNK), 1)
    tril = row >= col
    lng = lng_ref[...]
    lnb = lnb_ref[...]

    def chunk(c, carry):
        r = pl.ds(pl.multiple_of(c * SGU_CHUNK, SGU_CHUNK), SGU_CHUNK)
        v = jax.nn.gelu(v_ref[r, :].astype(F32))
        mu = jnp.mean(v, axis=-1, keepdims=True)
        vc = v - mu
        var = jnp.mean(vc * vc, axis=-1, keepdims=True)
        vn = (vc * lax.rsqrt(var + EPS) * lng + lnb).astype(BF16)
        for g in range(SGU_GROUPS):
            cs = slice(g * LANES, (g + 1) * LANES)
            w = jnp.where(tril, w_ref[g], 0.0).astype(BF16)
            mixed = jnp.dot(w, vn[:, cs], preferred_element_type=F32) + bt_ref[:, g:g + 1]
            u = jax.nn.gelu(u_ref[r, cs].astype(F32))
            o_ref[r, cs] = (u * mixed).astype(o_ref.dtype)
        return carry

    lax.fori_loop(0, _exact_div(ts, SGU_CHUNK), chunk, 0)


def _sgu(proj, ln_g, ln_b, w, bt, layer, *, ts):
    n = proj.shape[0]
    width = SGU_GROUPS * LANES
    return pl.pallas_call(
        functools.partial(_sgu_kernel, ts=ts),
        grid=(_exact_div(n, ts),),
        in_specs=[pl.BlockSpec((ts, width), lambda i: (i, _exact_div(COL_UB, width))),
                  pl.BlockSpec((ts, width), lambda i: (i, _exact_div(COL_VB, width))),
                  pl.BlockSpec((None, 1, width), lambda i: (layer, 0, 0)),
                  pl.BlockSpec((None, 1, width), lambda i: (layer, 0, 0)),
                  pl.BlockSpec((None, SGU_GROUPS, SGU_CHUNK, SGU_CHUNK), lambda i: (layer, 0, 0, 0)),
                  pl.BlockSpec((None, SGU_CHUNK, SGU_GROUPS), lambda i: (layer, 0, 0))],
        out_specs=pl.BlockSpec((ts, width), lambda i: (i, 0)),
        out_shape=jax.ShapeDtypeStruct((n, width), BF16),
        compiler_params=_params(("parallel",)),
        name="sgu",
    )(proj, proj, ln_g, ln_b, w, bt)


def _hgrn_kernel(f_ref, i_ref, q_ref, g_ref, lb_ref, ng_ref, o_ref, state_ref, kqb_ref, a_ref, *, tt, hb):
    C, SUB = HGRN_CHUNK, HGRN_SUB

    @pl.when(pl.program_id(2) == 0)
    def _():
        state_ref[...] = jnp.zeros_like(state_ref)

    ng = ng_ref[...]
    r64 = lax.broadcasted_iota(jnp.int32, (C, C), 0)
    c64 = lax.broadcasted_iota(jnp.int32, (C, C), 1)
    tril = jnp.where(r64 >= c64, 1.0, 0.0).astype(BF16)
    t_i = lax.broadcasted_iota(jnp.int32, (C, LANES), 0)
    l_i = lax.broadcasted_iota(jnp.int32, (C, LANES), 1)
    own_half = ((l_i // C) == ((t_i // SUB) % 2)) & ((l_i % C) <= t_i)
    valid0 = own_half & (t_i < 2 * SUB)
    valid1 = own_half & (t_i >= 2 * SUB)

    def gates(r, h):
        cs = slice(h * LANES, (h + 1) * LANES)
        lb = lb_ref[:, cs]
        half_span = 0.5 * (1.0 - lb)
        f = (lb + half_span) + half_span * jnp.tanh(0.5 * f_ref[r, cs].astype(F32))
        lf = jnp.log2(f)
        k = 1.0 - f
        x_q = q_ref[r, cs].astype(F32)
        qf = x_q * ((0.5 * HGRN_HEAD_DIM ** -0.5) + (0.5 * HGRN_HEAD_DIM ** -0.5) * jnp.tanh(0.5 * x_q))
        hi = lf.astype(BF16)
        lo = (lf - hi.astype(F32)).astype(BF16)
        b = (jnp.dot(tril, hi, preferred_element_type=F32)
             + jnp.dot(tril, lo, preferred_element_type=F32))
        return k, qf, b

    def scores(k, qf, b):
        refs = [jnp.zeros((1, LANES), F32)] + [b[SUB * i - 1:SUB * i, :] for i in range(1, _exact_div(C, SUB))]
        q_rel = jnp.concatenate(
            [(qf[SUB * i:SUB * (i + 1)] * jnp.exp2(b[SUB * i:SUB * (i + 1)] - rf)).astype(BF16)
             for i, rf in enumerate(refs)], axis=0)
        blocks = []
        for i, rf in enumerate(refs):
            rows = SUB * (i + 1)
            blocks.append((k[:rows] * jnp.exp2(rf - b[:rows])).astype(BF16))
            if rows < C:
                blocks.append(jnp.zeros((C - rows, LANES), BF16))
        kcat = jnp.concatenate(blocks, axis=0)
        sc = lax.dot_general(q_rel, kcat, (((1,), (1,)), ((), ())), preferred_element_type=F32)
        return jnp.where(valid0, sc[:, :LANES], jnp.where(valid1, sc[:, LANES:], 0.0))

    def outputs(r, h, k, qf, b, a):
        cs = slice(h * LANES, (h + 1) * LANES)
        v = i_ref[r, cs]
        b_last = b[C - 1:C, :]
        q_abs = (qf * jnp.exp2(b)).astype(BF16)
        vv = jnp.concatenate([v, v], axis=0)
        st = state_ref[h]
        o = (lax.dot_general(q_abs, st.astype(BF16), (((1,), (1,)), ((), ())), preferred_element_type=F32)
             + jnp.dot(a.astype(BF16), vv, preferred_element_type=F32))
        k_end = (k * jnp.exp2(b_last - b)).astype(BF16)
        state_ref[h] = st * jnp.exp2(b_last) + lax.dot_general(
            v, k_end, (((0,), (0,)), ((), ())), preferred_element_type=F32)
        on = o * lax.rsqrt(jnp.mean(o * o, axis=-1, keepdims=True) + EPS) * ng
        x_g = g_ref[r, cs].astype(F32)
        o_ref[r, cs] = (on * (x_g * (0.5 + 0.5 * jnp.tanh(0.5 * x_g)))).astype(o_ref.dtype)

    def rows(c):
        return pl.ds(pl.multiple_of(c * C, C), C)

    def stage_scores(r):
        kqb = [gates(r, h) for h in range(hb)]
        a = [scores(*t) for t in kqb]
        for h in range(hb):
            for j, val in enumerate(kqb[h]):
                kqb_ref[j, h] = val
            a_ref[h] = a[h].astype(BF16)

    n_chunks = _exact_div(tt, C)
    stage_scores(rows(0))

    def chunk(c, carry):
        staged = [(kqb_ref[0, h], kqb_ref[1, h], kqb_ref[2, h], a_ref[h]) for h in range(hb)]
        for h in range(hb):
            outputs(rows(c), h, *staged[h])
        stage_scores(rows(jnp.minimum(c + 1, n_chunks - 1)))
        return carry

    lax.fori_loop(0, n_chunks, chunk, 0)


def _hgrn(proj, lb_l, norm_g, layer, *, batch, seq, tt, hb):
    n = proj.shape[0]
    nt = _exact_div(seq, tt)
    width = HGRN_HEADS * HGRN_HEAD_DIM
    bw = hb * LANES

    def col(off):
        return lambda b, h, t: (b * nt + t, _exact_div(off, bw) + h)

    return pl.pallas_call(
        functools.partial(_hgrn_kernel, tt=tt, hb=hb),
        grid=(batch, _exact_div(HGRN_HEADS, hb), nt),
        in_specs=[pl.BlockSpec((tt, bw), col(COL_FC)),
                  pl.BlockSpec((tt, bw), col(COL_IC)),
                  pl.BlockSpec((tt, bw), col(COL_QC)),
                  pl.BlockSpec((tt, bw), col(COL_GC)),
                  pl.BlockSpec((1, bw), lambda b, h, t: (0, h)),
                  pl.BlockSpec((None, 1, LANES), lambda b, h, t: (layer, 0, 0))],
        out_specs=pl.BlockSpec((tt, bw), lambda b, h, t: (b * nt + t, h)),
        out_shape=jax.ShapeDtypeStruct((n, width), BF16),
        scratch_shapes=[pltpu.VMEM((hb, HGRN_HEAD_DIM, HGRN_HEAD_DIM), F32),
                        pltpu.VMEM((3, hb, HGRN_CHUNK, LANES), F32),
                        pltpu.VMEM((hb, HGRN_CHUNK, LANES), BF16)],
        compiler_params=_params(("parallel", "parallel", "arbitrary")),
        name="hgrn2",
    )(proj, proj, proj, proj, lb_l, norm_g)


def _merge_kernel(x_ref, ya_ref, yb_ref, yc_ref, ga_ref, gb_ref, gc_ref,
                  wa_ref, wb_ref, wc_ref, wo_ref, o_ref, *, tk):
    d = o_ref.shape[1]
    ya, yb, yc = ya_ref[...], yb_ref[...], yc_ref[...]

    def merged(c):
        cs = slice(c * tk, (c + 1) * tk)
        m = (ga_ref[:, cs].astype(F32) * jnp.dot(ya, wa_ref[:, cs], preferred_element_type=F32)
             + gb_ref[:, cs].astype(F32) * jnp.dot(yb, wb_ref[:, cs], preferred_element_type=F32)
             + gc_ref[:, cs].astype(F32) * jnp.dot(yc, wc_ref[:, cs], preferred_element_type=F32))
        return m.astype(BF16)

    n_groups = _exact_div(d, tk)
    acc = x_ref[...]
    m_next = merged(0)
    for c in range(n_groups):
        m_cur = m_next
        if c + 1 < n_groups:
            m_next = merged(c + 1)
        acc = acc + jnp.dot(m_cur, wo_ref[c * tk:(c + 1) * tk, :], preferred_element_type=F32)
    o_ref[...] = acc


def _merge(x, ya, yb, yc, proj, wa, wb, wc, wo, layer, *, tm, tk):
    n, d = x.shape
    wbr = ya.shape[1]
    g0 = _exact_div(COL_GATE, d)

    def gate(br):
        return pl.BlockSpec((tm, d), lambda i: (i, g0 + br))

    def resident(rows):
        return pl.BlockSpec((None, rows, d), lambda i: (layer, 0, 0), pipeline_mode=pl.Buffered(1))

    y_spec = pl.BlockSpec((tm, wbr), lambda i: (i, 0))
    return pl.pallas_call(
        functools.partial(_merge_kernel, tk=tk),
        grid=(_exact_div(n, tm),),
        in_specs=[pl.BlockSpec((tm, d), lambda i: (i, 0)),
                  y_spec, y_spec, y_spec, gate(0), gate(1), gate(2),
                  resident(wbr), resident(wbr), resident(wbr), resident(d)],
        out_specs=pl.BlockSpec((tm, d), lambda i: (i, 0)),
        out_shape=jax.ShapeDtypeStruct((n, d), F32),
        compiler_params=_params(("parallel",)),
        name="merge",
    )(x, ya, yb, yc, proj, proj, proj, wa, wb, wc, wo)


def _xattn_kernel(x_ref, g_ref, wq_ref, kv_ref, wo_ref, o_ref, h_ref, *, tm):
    _rms_rows(x_ref, g_ref, h_ref, tm)
    q = jnp.dot(h_ref[...], wq_ref[...], preferred_element_type=F32).astype(BF16)
    xw = X_HEADS * X_HEAD_DIM
    scale = X_HEAD_DIM ** -0.5

    def score(hd):
        cs = slice(hd * X_HEAD_DIM, (hd + 1) * X_HEAD_DIM)
        return lax.dot_general(q[:, cs], kv_ref[:, cs], (((1,), (1,)), ((), ())),
                               preferred_element_type=F32) * scale

    def attend(hd, s):
        vh = kv_ref[:, xw + hd * X_HEAD_DIM:xw + (hd + 1) * X_HEAD_DIM]
        mx = jnp.max(s, axis=-1, keepdims=True)
        pe = jnp.exp(s - mx)
        inv = 1.0 / jnp.sum(pe, axis=-1, keepdims=True)
        return (jnp.dot(pe.astype(BF16), vh, preferred_element_type=F32) * inv).astype(BF16)

    outs = []
    s_next = score(0)
    for hd in range(X_HEADS):
        s_cur = s_next
        if hd + 1 < X_HEADS:
            s_next = score(hd + 1)
        outs.append(attend(hd, s_cur))
    oc = jnp.concatenate(outs, axis=1)
    o_ref[...] = x_ref[...] + jnp.dot(oc, wo_ref[...], preferred_element_type=F32)


def _xattn(x, g, wq, kv, wo, layer, *, seq, mem_len, tm):
    n, d = x.shape
    xw = X_HEADS * X_HEAD_DIM
    tiles_per_seq = _exact_div(seq, tm)
    return pl.pallas_call(
        functools.partial(_xattn_kernel, tm=tm),
        grid=(_exact_div(n, tm),),
        in_specs=[pl.BlockSpec((tm, d), lambda i: (i, 0)),
                  pl.BlockSpec((None, 1, d), lambda i: (layer, 0, 0)),
                  pl.BlockSpec((None, d, xw), lambda i: (layer, 0, 0)),
                  pl.BlockSpec((mem_len, 2 * xw), lambda i: (i // tiles_per_seq, layer)),
                  pl.BlockSpec((None, xw, d), lambda i: (layer, 0, 0))],
        out_specs=pl.BlockSpec((tm, d), lambda i: (i, 0)),
        out_shape=jax.ShapeDtypeStruct((n, d), F32),
        scratch_shapes=[pltpu.VMEM((tm, d), BF16)],
        compiler_params=_params(("parallel",)),
        name="xattn",
    )(x, g, wq, kv, wo)


def _ffn_kernel(x_ref, g_ref, fg_ref, wg_ref, wu_ref, wo_ref, o_ref, h_ref, *, tm, final_norm):
    @pl.when(pl.program_id(1) == 0)
    def _():
        _rms_rows(x_ref, g_ref, h_ref, tm, copy_ref=o_ref)

    h = h_ref[...]
    gt = jnp.dot(h, wg_ref[...], preferred_element_type=F32)
    up = jnp.dot(h, wu_ref[...], preferred_element_type=F32)
    a = (jax.nn.silu(gt) * up).astype(BF16)
    o_ref[...] += jnp.dot(a, wo_ref[...], preferred_element_type=F32)

    if final_norm:
        @pl.when(pl.program_id(1) == pl.num_programs(1) - 1)
        def _():
            _rms_rows(o_ref, fg_ref, o_ref, tm)


def _ffn(x, g, final_g, w_in, w_out, layer, *, tm, tf, final_norm):
    n, d = x.shape
    nf = _exact_div(w_out.shape[1], tf)
    return pl.pallas_call(
        functools.partial(_ffn_kernel, tm=tm, final_norm=final_norm),
        grid=(_exact_div(n, tm), nf),
        in_specs=[pl.BlockSpec((tm, d), lambda i, f: (i, 0)),
                  pl.BlockSpec((None, 1, d), lambda i, f: (layer, 0, 0)),
                  pl.BlockSpec((1, d), lambda i, f: (0, 0)),
                  pl.BlockSpec((None, d, tf), lambda i, f: (layer, 0, f)),
                  pl.BlockSpec((None, d, tf), lambda i, f: (layer, 0, nf + f)),
                  pl.BlockSpec((None, tf, d), lambda i, f: (layer, f, 0))],
        out_specs=pl.BlockSpec((tm, d), lambda i, f: (i, 0)),
        out_shape=jax.ShapeDtypeStruct((n, d), F32),
        scratch_shapes=[pltpu.VMEM((tm, d), BF16)],
        compiler_params=_params(("parallel", "arbitrary")),
        name="ffn",
    )(x, g, final_g, w_in, w_in, w_out)


def kernel(x, mem, norm_mix, w_in, w_gate, sinks, sgu_ln_g, sgu_ln_b, sgu_w, sgu_b, hgrn_lb, hgrn_norm,
           w_br_a, w_br_b, w_br_c, w_out, norm_x, mem_norm, w_xq, w_xkv, w_xo, norm_ffn, w_ffn_in,
           w_ffn_out, final_norm):
    batch, seq, d = x.shape
    depth = w_in.shape[0]
    mem_len = mem.shape[1]
    n = batch * seq

    (wqa, wka, wva, wub, wvb, wfc, wic, wqc, wgc) = jnp.split(
        w_in, [1024, 1280, 1536, 2560, 3584, 4608, 5632, 6656], axis=-1)
    w_all = jnp.concatenate([w_gate, wqa, wub, wvb, wfc, wic, wqc, wgc, wka, wva], axis=-1).astype(BF16)
    wa, wb, wc, wo = (w.astype(BF16) for w in (w_br_a, w_br_b, w_br_c, w_out))
    wxq, wxo = w_xq.astype(BF16), w_xo.astype(BF16)
    wxkv = jnp.concatenate([w_xkv[l] for l in range(depth)], axis=-1).astype(BF16)
    wfi, wfo = w_ffn_in.astype(BF16), w_ffn_out.astype(BF16)
    sgu_bt = jnp.swapaxes(sgu_b, 1, 2)
    sm = jax.nn.softmax(hgrn_lb.astype(F32), axis=0)
    lb_all = jnp.cumsum(sm, axis=0) - sm[0:1]
    sgu_ln_g, sgu_ln_b, hgrn_norm, norm_x, norm_ffn = (
        p[:, None, :] for p in (sgu_ln_g, sgu_ln_b, hgrn_norm, norm_x, norm_ffn))

    xf = x.reshape(n, d)
    kv_all = _norm_matmul(mem.reshape(batch * mem_len, d), mem_norm.reshape(1, d), wxkv, 0,
                          tm=1024, tn=512)
    for l in range(depth):
        proj = _norm_matmul(xf, norm_mix[l:l + 1], w_all, l, tm=1024, tn=1536, activate=True)
        ya = _swa(proj, sinks[l], batch=batch, seq=seq, tq=512)
        yb = _sgu(proj, sgu_ln_g, sgu_ln_b, sgu_w, sgu_bt, l, ts=512)
        yc = _hgrn(proj, lb_all[l:l + 1], hgrn_norm, l, batch=batch, seq=seq, tt=1024, hb=8)
        xf = _merge(xf, ya, yb, yc, proj, wa, wb, wc, wo, l, tm=256, tk=512)
        xf = _xattn(xf, norm_x, wxq, kv_all, wxo, l, seq=seq, mem_len=mem_len, tm=512)
        xf = _ffn(xf, norm_ffn, final_norm.reshape(1, d), wfi, wfo, l, tm=1024, tf=512,
                  final_norm=(l == depth - 1))
    return xf.reshape(batch, seq, d)
```

```python
import functools

import jax
import jax.numpy as jnp
from jax import lax
from jax.experimental import pallas as pl
from jax.experimental.pallas import tpu as pltpu

F32 = jnp.float32
BF16 = jnp.bfloat16
EPS = 1e-6

LANES = 128
V7X_VMEM_LIMIT = 56 * 1024 * 1024

ATTN_HEAD_DIM = 64
ATTN_Q_HEADS = 16
ATTN_KV_HEADS = 4
WINDOW = 128
SGU_CHUNK = 128
SGU_GROUPS = 8
HGRN_HEADS = 8
HGRN_HEAD_DIM = 128
HGRN_CHUNK = 64
HGRN_SUB = 16
X_HEADS = 4
X_HEAD_DIM = 128

COL_GATE, COL_QA, COL_UB, COL_VB, COL_FC, COL_IC, COL_QC, COL_GC, COL_KV, COL_END = (
    0, 6144, 7168, 8192, 9216, 10240, 11264, 12288, 13312, 13824)
ACT_SUB = 512


def _exact_div(a, b):
    assert a % b == 0, (a, b)
    return a // b


def _params(sem):
    return pltpu.CompilerParams(dimension_semantics=sem, vmem_limit_bytes=V7X_VMEM_LIMIT)


def _rms_rows(x_ref, g_ref, h_ref, rows, copy_ref=None):
    slab = 64
    g = g_ref[...]

    def body(i, c):
        r = pl.ds(pl.multiple_of(i * slab, slab), slab)
        x = x_ref[r, :]
        if copy_ref is not None:
            copy_ref[r, :] = x
        ms = jnp.mean(x * x, axis=-1, keepdims=True)
        h_ref[r, :] = (x * lax.rsqrt(ms + EPS) * g).astype(h_ref.dtype)
        return c

    lax.fori_loop(0, _exact_div(rows, slab), body, 0, unroll=4)


def _in_range(c0, lo_hi):
    _exact_div(lo_hi[0], ACT_SUB), _exact_div(lo_hi[1], ACT_SUB)
    return (c0 >= lo_hi[0]) & (c0 < lo_hi[1])


def _projection_activation(x, c0):
    is_sig = _in_range(c0, (COL_GATE, COL_QA))
    mul = jnp.where(_in_range(c0, (COL_QA, COL_UB)), ATTN_HEAD_DIM ** -0.5, 1.0)
    return jnp.where(is_sig, 0.5 + 0.5 * jnp.tanh(0.5 * x), x * mul)


def _norm_matmul_kernel(x_ref, g_ref, w_ref, o_ref, h_ref, *, activate, tm, tn):
    j = pl.program_id(1)

    @pl.when(j == 0)
    def _():
        _rms_rows(x_ref, g_ref, h_ref, tm)

    if not activate:
        o_ref[...] = jnp.dot(h_ref[...], w_ref[...], preferred_element_type=F32).astype(o_ref.dtype)
    else:
        h = h_ref[...]
        for s in range(_exact_div(tn, ACT_SUB)):
            cs = slice(s * ACT_SUB, (s + 1) * ACT_SUB)
            acc = jnp.dot(h, w_ref[:, cs], preferred_element_type=F32)
            o_ref[:, cs] = _projection_activation(acc, j * tn + s * ACT_SUB).astype(o_ref.dtype)


def _norm_matmul(x, g, w, layer, *, tm, tn, activate=False):
    n, d = x.shape
    nout = w.shape[-1]
    kern = functools.partial(_norm_matmul_kernel, activate=activate, tm=tm, tn=tn)
    if w.ndim == 3:
        w_spec = pl.BlockSpec((None, d, tn), lambda i, j: (layer, 0, j))
    else:
        w_spec = pl.BlockSpec((d, tn), lambda i, j: (0, j))
    return pl.pallas_call(
        kern,
        grid=(_exact_div(n, tm), _exact_div(nout, tn)),
        in_specs=[pl.BlockSpec((tm, d), lambda i, j: (i, 0)),
                  pl.BlockSpec((1, d), lambda i, j: (0, 0)),
                  w_spec],
        out_specs=pl.BlockSpec((tm, tn), lambda i, j: (i, j)),
        out_shape=jax.ShapeDtypeStruct((n, nout), BF16),
        scratch_shapes=[pltpu.VMEM((tm, d), BF16)],
        compiler_params=_params(("parallel", "arbitrary")),
        name="norm_matmul",
    )(x, g, w)


def _swap_halves(t):
    return jnp.concatenate([t[:, 64:], t[:, :64]], axis=1)


def _swa_kernel(sink_ref, q_ref, kvc_ref, kvp_ref, o_ref, kv_buf, *, tq):
    t_idx = pl.program_id(1)
    kv_buf[0:WINDOW, :] = kvp_ref[...]
    kv_buf[WINDOW:, :] = kvc_ref[...]

    row = lax.broadcasted_iota(jnp.int32, (WINDOW, 2 * WINDOW), 0)
    col = lax.broadcasted_iota(jnp.int32, (WINDOW, 2 * WINDOW), 1)
    band = (col > row) & (col <= row + WINDOW)
    lane = lax.broadcasted_iota(jnp.int32, (2 * WINDOW, LANES), 1)
    lo = lane < 64
    out_lane = lax.broadcasted_iota(jnp.int32, (WINDOW, LANES), 1)

    def blk(i, c):
        r0 = pl.multiple_of(i * WINDOW, WINDOW)
        cmin = jnp.where((t_idx == 0) & (i == 0), WINDOW, 0)
        valid = band & (col >= cmin)
        kcats, vcats = [], []
        for m in range(2):
            kt = kv_buf[pl.ds(r0, 2 * WINDOW), m * LANES:(m + 1) * LANES]
            vt = kv_buf[pl.ds(r0, 2 * WINDOW), 256 + m * LANES:256 + (m + 1) * LANES]
            ks, vs = _swap_halves(kt), _swap_halves(vt)
            zero = jnp.zeros_like(kt)
            for e in range(2):
                k_lo = jnp.where(lo, kt if e == 0 else ks, zero)
                k_hi = jnp.where(lo, zero, ks if e == 0 else kt)
                v_lo = jnp.where(lo, vt if e == 0 else vs, zero)
                v_hi = jnp.where(lo, zero, vs if e == 0 else vt)
                kcats.append(jnp.concatenate([k_lo, k_hi], axis=0))
                vcats.append(jnp.concatenate([v_lo, v_hi], axis=0))

        def score(j):
            qt = jnp.concatenate([q_ref[pl.ds(r0, WINDOW), p * LANES:(p + 1) * LANES]
                                  for p in (2 * j, 2 * j + 1)], axis=0)
            return lax.dot_general(qt, kcats[j], (((1,), (1,)), ((), ())),
                                   preferred_element_type=F32)

        def attend(j, s):
            pcats, scales = [], []
            for t in range(2):
                p = 2 * j + t
                probs, invs = [], []
                for hh in range(2):
                    sink = sink_ref[2 * p + hh]
                    sh = jnp.where(valid, s[t * WINDOW:(t + 1) * WINDOW, hh * 256:(hh + 1) * 256], -jnp.inf)
                    mx = jnp.maximum(jnp.max(sh, axis=-1, keepdims=True), sink)
                    pe = jnp.exp(sh - mx)
                    den = jnp.sum(pe, axis=-1, keepdims=True) + jnp.exp(sink - mx)
                    probs.append(pe.astype(BF16))
                    invs.append(1.0 / den)
                pcats.append(jnp.concatenate(probs, axis=1))
                scales.append(jnp.where(out_lane < 64, invs[0], invs[1]))
            o = jnp.dot(jnp.concatenate(pcats, axis=0), vcats[j], preferred_element_type=F32)
            for t in range(2):
                p = 2 * j + t
                o_ref[pl.ds(r0, WINDOW), p * LANES:(p + 1) * LANES] = (
                    o[t * WINDOW:(t + 1) * WINDOW] * scales[t]).astype(o_ref.dtype)

        s_next = score(0)
        for j in range(ATTN_KV_HEADS):
            s_cur = s_next
            if j + 1 < ATTN_KV_HEADS:
                s_next = score(j + 1)
            attend(j, s_cur)
        return c

    lax.fori_loop(0, _exact_div(tq, WINDOW), blk, 0)


def _swa(proj, sinks_l, *, batch, seq, tq):
    n = proj.shape[0]
    nt = _exact_div(seq, tq)
    qw = ATTN_Q_HEADS * ATTN_HEAD_DIM
    kvw = 2 * ATTN_KV_HEADS * ATTN_HEAD_DIM
    kv_blk = _exact_div(COL_KV, kvw)
    bpt = _exact_div(tq, WINDOW)

    def prev_map(b, t):
        return (jnp.maximum(b * _exact_div(seq, WINDOW) + t * bpt - 1, 0), kv_blk)

    return pl.pallas_call(
        functools.partial(_swa_kernel, tq=tq),
        grid=(batch, nt),
        in_specs=[pl.BlockSpec(memory_space=pltpu.SMEM),
                  pl.BlockSpec((tq, qw), lambda b, t: (b * nt + t, _exact_div(COL_QA, qw))),
                  pl.BlockSpec((tq, kvw), lambda b, t: (b * nt + t, kv_blk)),
                  pl.BlockSpec((WINDOW, kvw), prev_map)],
        out_specs=pl.BlockSpec((tq, qw), lambda b, t: (b * nt + t, 0)),
        out_shape=jax.ShapeDtypeStruct((n, qw), BF16),
        scratch_shapes=[pltpu.VMEM((tq + WINDOW, kvw), BF16)],
        compiler_params=_params(("parallel", "parallel")),
        name="swa",
    )(sinks_l, proj, proj, proj)


def _sgu_kernel(u_ref, v_ref, lng_ref, lnb_ref, w_ref, bt_ref, o_ref, *, ts):
    row = lax.broadcasted_iota(jnp.int32, (SGU_CHUNK, SGU_CHUNK), 0)
    col = lax.broadcasted_iota(jnp.int32, (SGU_CHUNK, SGU_CHUNK), 1)
    tril = row >= col
    lng = lng_ref[...]
    lnb = lnb_ref[...]

    def chunk(c, carry):
        r = pl.ds(pl.multiple_of(c * SGU_CHUNK, SGU_CHUNK), SGU_CHUNK)
        v = jax.nn.gelu(v_ref[r, :].astype(F32))
        mu = jnp.mean(v, axis=-1, keepdims=True)
        vc = v - mu
        var = jnp.mean(vc * vc, axis=-1, keepdims=True)
        vn = (vc * lax.rsqrt(var + EPS) * lng + lnb).astype(BF16)
        for g in range(SGU_GROUPS):
            cs = slice(g * LANES, (g + 1) * LANES)
            w = jnp.where(tril, w_ref[g], 0.0).astype(BF16)
            mixed = jnp.dot(w, vn[:, cs], preferred_element_type=F32) + bt_ref[:, g:g + 1]
            u = jax.nn.gelu(u_ref[r, cs].astype(F32))
            o_ref[r, cs] = (u * mixed).astype(o_ref.dtype)
        return carry

    lax.fori_loop(0, _exact_div(ts, SGU_CHUNK), chunk, 0)


def _sgu(proj, ln_g, ln_b, w, bt, layer, *, ts):
    n = proj.shape[0]
    width = SGU_GROUPS * LANES
    return pl.pallas_call(
        functools.partial(_sgu_kernel, ts=ts),
        grid=(_exact_div(n, ts),),
        in_specs=[pl.BlockSpec((ts, width), lambda i: (i, _exact_div(COL_UB, width))),
                  pl.BlockSpec((ts, width), lambda i: (i, _exact_div(COL_VB, width))),
                  pl.BlockSpec((None, 1, width), lambda i: (layer, 0, 0)),
                  pl.BlockSpec((None, 1, width), lambda i: (layer, 0, 0)),
                  pl.BlockSpec((None, SGU_GROUPS, SGU_CHUNK, SGU_CHUNK), lambda i: (layer, 0, 0, 0)),
                  pl.BlockSpec((None, SGU_CHUNK, SGU_GROUPS), lambda i: (layer, 0, 0))],
        out_specs=pl.BlockSpec((ts, width), lambda i: (i, 0)),
        out_shape=jax.ShapeDtypeStruct((n, width), BF16),
        compiler_params=_params(("parallel",)),
        name="sgu",
    )(proj, proj, ln_g, ln_b, w, bt)


def _hgrn_kernel(f_ref, i_ref, q_ref, g_ref, lb_ref, ng_ref, o_ref, state_ref, kqb_ref, a_ref, *, tt, hb):
    C, SUB = HGRN_CHUNK, HGRN_SUB

    @pl.when(pl.program_id(2) == 0)
    def _():
        state_ref[...] = jnp.zeros_like(state_ref)

    ng = ng_ref[...]
    r64 = lax.broadcasted_iota(jnp.int32, (C, C), 0)
    c64 = lax.broadcasted_iota(jnp.int32, (C, C), 1)
    tril = jnp.where(r64 >= c64, 1.0, 0.0).astype(BF16)
    t_i = lax.broadcasted_iota(jnp.int32, (C, LANES), 0)
    l_i = lax.broadcasted_iota(jnp.int32, (C, LANES), 1)
    own_half = ((l_i // C) == ((t_i // SUB) % 2)) & ((l_i % C) <= t_i)
    valid0 = own_half & (t_i < 2 * SUB)
    valid1 = own_half & (t_i >= 2 * SUB)

    def gates(r, h):
        cs = slice(h * LANES, (h + 1) * LANES)
        lb = lb_ref[:, cs]
        half_span = 0.5 * (1.0 - lb)
        f = (lb + half_span) + half_span * jnp.tanh(0.5 * f_ref[r, cs].astype(F32))
        lf = jnp.log2(f)
        k = 1.0 - f
        x_q = q_ref[r, cs].astype(F32)
        qf = x_q * ((0.5 * HGRN_HEAD_DIM ** -0.5) + (0.5 * HGRN_HEAD_DIM ** -0.5) * jnp.tanh(0.5 * x_q))
        hi = lf.astype(BF16)
        lo = (lf - hi.astype(F32)).astype(BF16)
        b = (jnp.dot(tril, hi, preferred_element_type=F32)
             + jnp.dot(tril, lo, preferred_element_type=F32))
        return k, qf, b

    def scores(k, qf, b):
        refs = [jnp.zeros((1, LANES), F32)] + [b[SUB * i - 1:SUB * i, :] for i in range(1, _exact_div(C, SUB))]
        q_rel = jnp.concatenate(
            [(qf[SUB * i:SUB * (i + 1)] * jnp.exp2(b[SUB * i:SUB * (i + 1)] - rf)).astype(BF16)
             for i, rf in enumerate(refs)], axis=0)
        blocks = []
        for i, rf in enumerate(refs):
            rows = SUB * (i + 1)
            blocks.append((k[:rows] * jnp.exp2(rf - b[:rows])).astype(BF16))
            if rows < C:
                blocks.append(jnp.zeros((C - rows, LANES), BF16))
        kcat = jnp.concatenate(blocks, axis=0)
        sc = lax.dot_general(q_rel, kcat, (((1,), (1,)), ((), ())), preferred_element_type=F32)
        return jnp.where(valid0, sc[:, :LANES], jnp.where(valid1, sc[:, LANES:], 0.0))

    def outputs(r, h, k, qf, b, a):
        cs = slice(h * LANES, (h + 1) * LANES)
        v = i_ref[r, cs]
        b_last = b[C - 1:C, :]
        q_abs = (qf * jnp.exp2(b)).astype(BF16)
        vv = jnp.concatenate([v, v], axis=0)
        st = state_ref[h]
        o = (lax.dot_general(q_abs, st.astype(BF16), (((1,), (1,)), ((), ())), preferred_element_type=F32)
             + jnp.dot(a.astype(BF16), vv, preferred_element_type=F32))
        k_end = (k * jnp.exp2(b_last - b)).astype(BF16)
        state_ref[h] = st * jnp.exp2(b_last) + lax.dot_general(
            v, k_end, (((0,), (0,)), ((), ())), preferred_element_type=F32)
        on = o * lax.rsqrt(jnp.mean(o * o, axis=-1, keepdims=True) + EPS) * ng
        x_g = g_ref[r, cs].astype(F32)
        o_ref[r, cs] = (on * (x_g * (0.5 + 0.5 * jnp.tanh(0.5 * x_g)))).astype(o_ref.dtype)

    def rows(c):
        return pl.ds(pl.multiple_of(c * C, C), C)

    def stage_scores(r):
        kqb = [gates(r, h) for h in range(hb)]
        a = [scores(*t) for t in kqb]
        for h in range(hb):
            for j, val in enumerate(kqb[h]):
                kqb_ref[j, h] = val
            a_ref[h] = a[h].astype(BF16)

    n_chunks = _exact_div(tt, C)
    stage_scores(rows(0))

    def chunk(c, carry):
        staged = [(kqb_ref[0, h], kqb_ref[1, h], kqb_ref[2, h], a_ref[h]) for h in range(hb)]
        for h in range(hb):
            outputs(rows(c), h, *staged[h])
        stage_scores(rows(jnp.minimum(c + 1, n_chunks - 1)))
        return carry

    lax.fori_loop(0, n_chunks, chunk, 0)


def _hgrn(proj, lb_l, norm_g, layer, *, batch, seq, tt, hb):
    n = proj.shape[0]
    nt = _exact_div(seq, tt)
    width = HGRN_HEADS * HGRN_HEAD_DIM
    bw = hb * LANES

    def col(off):
        return lambda b, h, t: (b * nt + t, _exact_div(off, bw) + h)

    return pl.pallas_call(
        functools.partial(_hgrn_kernel, tt=tt, hb=hb),
        grid=(batch, _exact_div(HGRN_HEADS, hb), nt),
        in_specs=[pl.BlockSpec((tt, bw), col(COL_FC)),
                  pl.BlockSpec((tt, bw), col(COL_IC)),
                  pl.BlockSpec((tt, bw), col(COL_QC)),
                  pl.BlockSpec((tt, bw), col(COL_GC)),
                  pl.BlockSpec((1, bw), lambda b, h, t: (0, h)),
                  pl.BlockSpec((None, 1, LANES), lambda b, h, t: (layer, 0, 0))],
        out_specs=pl.BlockSpec((tt, bw), lambda b, h, t: (b * nt + t, h)),
        out_shape=jax.ShapeDtypeStruct((n, width), BF16),
        scratch_shapes=[pltpu.VMEM((hb, HGRN_HEAD_DIM, HGRN_HEAD_DIM), F32),
                        pltpu.VMEM((3, hb, HGRN_CHUNK, LANES), F32),
                        pltpu.VMEM((hb, HGRN_CHUNK, LANES), BF16)],
        compiler_params=_params(("parallel", "parallel", "arbitrary")),
        name="hgrn2",
    )(proj, proj, proj, proj, lb_l, norm_g)


def _merge_kernel(x_ref, ya_ref, yb_ref, yc_ref, ga_ref, gb_ref, gc_ref,
                  wa_ref, wb_ref, wc_ref, wo_ref, o_ref, *, tk):
    d = o_ref.shape[1]
    ya, yb, yc = ya_ref[...], yb_ref[...], yc_ref[...]

    def merged(c):
        cs = slice(c * tk, (c + 1) * tk)
        m = (ga_ref[:, cs].astype(F32) * jnp.dot(ya, wa_ref[:, cs], preferred_element_type=F32)
             + gb_ref[:, cs].astype(F32) * jnp.dot(yb, wb_ref[:, cs], preferred_element_type=F32)
             + gc_ref[:, cs].astype(F32) * jnp.dot(yc, wc_ref[:, cs], preferred_element_type=F32))
        return m.astype(BF16)

    n_groups = _exact_div(d, tk)
    acc = x_ref[...]
    m_next = merged(0)
    for c in range(n_groups):
        m_cur = m_next
        if c + 1 < n_groups:
            m_next = merged(c + 1)
        acc = acc + jnp.dot(m_cur, wo_ref[c * tk:(c + 1) * tk, :], preferred_element_type=F32)
    o_ref[...] = acc


def _merge(x, ya, yb, yc, proj, wa, wb, wc, wo, layer, *, tm, tk):
    n, d = x.shape
    wbr = ya.shape[1]
    g0 = _exact_div(COL_GATE, d)

    def gate(br):
        return pl.BlockSpec((tm, d), lambda i: (i, g0 + br))

    def resident(rows):
        return pl.BlockSpec((None, rows, d), lambda i: (layer, 0, 0), pipeline_mode=pl.Buffered(1))

    y_spec = pl.BlockSpec((tm, wbr), lambda i: (i, 0))
    return pl.pallas_call(
        functools.partial(_merge_kernel, tk=tk),
        grid=(_exact_div(n, tm),),
        in_specs=[pl.BlockSpec((tm, d), lambda i: (i, 0)),
                  y_spec, y_spec, y_spec, gate(0), gate(1), gate(2),
                  resident(wbr), resident(wbr), resident(wbr), resident(d)],
        out_specs=pl.BlockSpec((tm, d), lambda i: (i, 0)),
        out_shape=jax.ShapeDtypeStruct((n, d), F32),
        compiler_params=_params(("parallel",)),
        name="merge",
    )(x, ya, yb, yc, proj, proj, proj, wa, wb, wc, wo)


def _xattn_kernel(x_ref, g_ref, wq_ref, kv_ref, wo_ref, o_ref, h_ref, *, tm):
    _rms_rows(x_ref, g_ref, h_ref, tm)
    q = jnp.dot(h_ref[...], wq_ref[...], preferred_element_type=F32).astype(BF16)
    xw = X_HEADS * X_HEAD_DIM
    scale = X_HEAD_DIM ** -0.5

    def score(hd):
        cs = slice(hd * X_HEAD_DIM, (hd + 1) * X_HEAD_DIM)
        return lax.dot_general(q[:, cs], kv_ref[:, cs], (((1,), (1,)), ((), ())),
                               preferred_element_type=F32) * scale

    def attend(hd, s):
        vh = kv_ref[:, xw + hd * X_HEAD_DIM:xw + (hd + 1) * X_HEAD_DIM]
        mx = jnp.max(s, axis=-1, keepdims=True)
        pe = jnp.exp(s - mx)
        inv = 1.0 / jnp.sum(pe, axis=-1, keepdims=True)
        return (jnp.dot(pe.astype(BF16), vh, preferred_element_type=F32) * inv).astype(BF16)

    outs = []
    s_next = score(0)
    for hd in range(X_HEADS):
        s_cur = s_next
        if hd + 1 < X_HEADS:
            s_next = score(hd + 1)
        outs.append(attend(hd, s_cur))
    oc = jnp.concatenate(outs, axis=1)
    o_ref[...] = x_ref[...] + jnp.dot(oc, wo_ref[...], preferred_element_type=F32)


def _xattn(x, g, wq, kv, wo, layer, *, seq, mem_len, tm):
    n, d = x.shape
    xw = X_HEADS * X_HEAD_DIM
    tiles_per_seq = _exact_div(seq, tm)
    return pl.pallas_call(
        functools.partial(_xattn_kernel, tm=tm),
        grid=(_exact_div(n, tm),),
        in_specs=[pl.BlockSpec((tm, d), lambda i: (i, 0)),
                  pl.BlockSpec((None, 1, d), lambda i: (layer, 0, 0)),
                  pl.BlockSpec((None, d, xw), lambda i: (layer, 0, 0)),
                  pl.BlockSpec((mem_len, 2 * xw), lambda i: (i // tiles_per_seq, layer)),
                  pl.BlockSpec((None, xw, d), lambda i: (layer, 0, 0))],
        out_specs=pl.BlockSpec((tm, d), lambda i: (i, 0)),
        out_shape=jax.ShapeDtypeStruct((n, d), F32),
        scratch_shapes=[pltpu.VMEM((tm, d), BF16)],
        compiler_params=_params(("parallel",)),
        name="xattn",
    )(x, g, wq, kv, wo)


def _ffn_kernel(x_ref, g_ref, fg_ref, wg_ref, wu_ref, wo_ref, o_ref, h_ref, *, tm, final_norm):
    @pl.when(pl.program_id(1) == 0)
    def _():
        _rms_rows(x_ref, g_ref, h_ref, tm, copy_ref=o_ref)

    h = h_ref[...]
    gt = jnp.dot(h, wg_ref[...], preferred_element_type=F32)
    up = jnp.dot(h, wu_ref[...], preferred_element_type=F32)
    a = (jax.nn.silu(gt) * up).astype(BF16)
    o_ref[...] += jnp.dot(a, wo_ref[...], preferred_element_type=F32)

    if final_norm:
        @pl.when(pl.program_id(1) == pl.num_programs(1) - 1)
        def _():
            _rms_rows(o_ref, fg_ref, o_ref, tm)


def _ffn(x, g, final_g, w_in, w_out, layer, *, tm, tf, final_norm):
    n, d = x.shape
    nf = _exact_div(w_out.shape[1], tf)
    return pl.pallas_call(
        functools.partial(_ffn_kernel, tm=tm, final_norm=final_norm),
        grid=(_exact_div(n, tm), nf),
        in_specs=[pl.BlockSpec((tm, d), lambda i, f: (i, 0)),
                  pl.BlockSpec((None, 1, d), lambda i, f: (layer, 0, 0)),
                  pl.BlockSpec((1, d), lambda i, f: (0, 0)),
                  pl.BlockSpec((None, d, tf), lambda i, f: (layer, 0, f)),
                  pl.BlockSpec((None, d, tf), lambda i, f: (layer, 0, nf + f)),
                  pl.BlockSpec((None, tf, d), lambda i, f: (layer, f, 0))],
        out_specs=pl.BlockSpec((tm, d), lambda i, f: (i, 0)),
        out_shape=jax.ShapeDtypeStruct((n, d), F32),
        scratch_shapes=[pltpu.VMEM((tm, d), BF16)],
        compiler_params=_params(("parallel", "arbitrary")),
        name="ffn",
    )(x, g, final_g, w_in, w_in, w_out)


def kernel(x, mem, norm_mix, w_in, w_gate, sinks, sgu_ln_g, sgu_ln_b, sgu_w, sgu_b, hgrn_lb, hgrn_norm,
           w_br_a, w_br_b, w_br_c, w_out, norm_x, mem_norm, w_xq, w_xkv, w_xo, norm_ffn, w_ffn_in,
           w_ffn_out, final_norm):
    batch, seq, d = x.shape
    depth = w_in.shape[0]
    mem_len = mem.shape[1]
    n = batch * seq

    (wqa, wka, wva, wub, wvb, wfc, wic, wqc, wgc) = jnp.split(
        w_in, [1024, 1280, 1536, 2560, 3584, 4608, 5632, 6656], axis=-1)
    w_all = jnp.concatenate([w_gate, wqa, wub, wvb, wfc, wic, wqc, wgc, wka, wva], axis=-1).astype(BF16)
    wa, wb, wc, wo = (w.astype(BF16) for w in (w_br_a, w_br_b, w_br_c, w_out))
    wxq, wxo = w_xq.astype(BF16), w_xo.astype(BF16)
    wxkv = jnp.concatenate([w_xkv[l] for l in range(depth)], axis=-1).astype(BF16)
    wfi, wfo = w_ffn_in.astype(BF16), w_ffn_out.astype(BF16)
    sgu_bt = jnp.swapaxes(sgu_b, 1, 2)
    sm = jax.nn.softmax(hgrn_lb.astype(F32), axis=0)
    lb_all = jnp.cumsum(sm, axis=0) - sm[0:1]
    sgu_ln_g, sgu_ln_b, hgrn_norm, norm_x, norm_ffn = (
        p[:, None, :] for p in (sgu_ln_g, sgu_ln_b, hgrn_norm, norm_x, norm_ffn))

    xf = x.reshape(n, d)
    kv_all = _norm_matmul(mem.reshape(batch * mem_len, d), mem_norm.reshape(1, d), wxkv, 0,
                          tm=1024, tn=512)
    for l in range(depth):
        proj = _norm_matmul(xf, norm_mix[l:l + 1], w_all, l, tm=1024, tn=1536, activate=True)
        ya = _swa(proj, sinks[l], batch=batch, seq=seq, tq=512)
        yb = _sgu(proj, sgu_ln_g, sgu_ln_b, sgu_w, sgu_bt, l, ts=512)
        yc = _hgrn(proj, lb_all[l:l + 1], hgrn_norm, l, batch=batch, seq=seq, tt=1024, hb=8)
        xf = _merge(xf, ya, yb, yc, proj, wa, wb, wc, wo, l, tm=256, tk=512)
        xf = _xattn(xf, norm_x, wxq, kv_all, wxo, l, seq=seq, mem_len=mem_len, tm=512)
        xf = _ffn(xf, norm_ffn, final_norm.reshape(1, d), wfi, wfo, l, tm=1024, tf=512,
                  final_norm=(l == depth - 1))
    return xf.reshape(batch, seq, d)
```

```python
import functools

import jax
import jax.numpy as jnp
from jax import lax
from jax.experimental import pallas as pl
from jax.experimental.pallas import tpu as pltpu

F32 = jnp.float32
BF16 = jnp.bfloat16
EPS = 1e-6

LANES = 128
V7X_VMEM_LIMIT = 56 * 1024 * 1024

ATTN_HEAD_DIM = 64
ATTN_Q_HEADS = 16
ATTN_KV_HEADS = 4
WINDOW = 128
SGU_CHUNK = 128
SGU_GROUPS = 8
HGRN_HEADS = 8
HGRN_HEAD_DIM = 128
HGRN_CHUNK = 64
HGRN_SUB = 16
X_HEADS = 4
X_HEAD_DIM = 128

COL_GATE, COL_QA, COL_UB, COL_VB, COL_FC, COL_IC, COL_QC, COL_GC, COL_KV, COL_END = (
    0, 6144, 7168, 8192, 9216, 10240, 11264, 12288, 13312, 13824)
ACT_SUB = 512


def _exact_div(a, b):
    assert a % b == 0, (a, b)
    return a // b


def _params(sem):
    return pltpu.CompilerParams(dimension_semantics=sem, vmem_limit_bytes=V7X_VMEM_LIMIT)


def _rms_rows(x_ref, g_ref, h_ref, rows, copy_ref=None):
    slab = 64
    g = g_ref[...]

    def body(i, c):
        r = pl.ds(pl.multiple_of(i * slab, slab), slab)
        x = x_ref[r, :]
        if copy_ref is not None:
            copy_ref[r, :] = x
        ms = jnp.mean(x * x, axis=-1, keepdims=True)
        h_ref[r, :] = (x * lax.rsqrt(ms + EPS) * g).astype(h_ref.dtype)
        return c

    lax.fori_loop(0, _exact_div(rows, slab), body, 0, unroll=4)


def _in_range(c0, lo_hi):
    _exact_div(lo_hi[0], ACT_SUB), _exact_div(lo_hi[1], ACT_SUB)
    return (c0 >= lo_hi[0]) & (c0 < lo_hi[1])


def _projection_activation(x, c0):
    is_sig = _in_range(c0, (COL_GATE, COL_QA))
    mul = jnp.where(_in_range(c0, (COL_QA, COL_UB)), ATTN_HEAD_DIM ** -0.5, 1.0)
    return jnp.where(is_sig, 0.5 + 0.5 * jnp.tanh(0.5 * x), x * mul)


def _norm_matmul_kernel(x_ref, g_ref, w_ref, o_ref, h_ref, *, activate, tm, tn):
    j = pl.program_id(1)

    @pl.when(j == 0)
    def _():
        _rms_rows(x_ref, g_ref, h_ref, tm)

    if not activate:
        o_ref[...] = jnp.dot(h_ref[...], w_ref[...], preferred_element_type=F32).astype(o_ref.dtype)
    else:
        h = h_ref[...]
        for s in range(_exact_div(tn, ACT_SUB)):
            cs = slice(s * ACT_SUB, (s + 1) * ACT_SUB)
            acc = jnp.dot(h, w_ref[:, cs], preferred_element_type=F32)
            o_ref[:, cs] = _projection_activation(acc, j * tn + s * ACT_SUB).astype(o_ref.dtype)


def _norm_matmul(x, g, w, layer, *, tm, tn, activate=False):
    n, d = x.shape
    nout = w.shape[-1]
    kern = functools.partial(_norm_matmul_kernel, activate=activate, tm=tm, tn=tn)
    if w.ndim == 3:
        w_spec = pl.BlockSpec((None, d, tn), lambda i, j: (layer, 0, j))
    else:
        w_spec = pl.BlockSpec((d, tn), lambda i, j: (0, j))
    return pl.pallas_call(
        kern,
        grid=(_exact_div(n, tm), _exact_div(nout, tn)),
        in_specs=[pl.BlockSpec((tm, d), lambda i, j: (i, 0)),
                  pl.BlockSpec((1, d), lambda i, j: (0, 0)),
                  w_spec],
        out_specs=pl.BlockSpec((tm, tn), lambda i, j: (i, j)),
        out_shape=jax.ShapeDtypeStruct((n, nout), BF16),
        scratch_shapes=[pltpu.VMEM((tm, d), BF16)],
        compiler_params=_params(("parallel", "arbitrary")),
        name="norm_matmul",
    )(x, g, w)


def _swap_halves(t):
    return jnp.concatenate([t[:, 64:], t[:, :64]], axis=1)


def _swa_kernel(sink_ref, q_ref, kvc_ref, kvp_ref, o_ref, kv_buf, *, tq):
    t_idx = pl.program_id(1)
    kv_buf[0:WINDOW, :] = kvp_ref[...]
    kv_buf[WINDOW:, :] = kvc_ref[...]

    row = lax.broadcasted_iota(jnp.int32, (WINDOW, 2 * WINDOW), 0)
    col = lax.broadcasted_iota(jnp.int32, (WINDOW, 2 * WINDOW), 1)
    band = (col > row) & (col <= row + WINDOW)
    lane = lax.broadcasted_iota(jnp.int32, (2 * WINDOW, LANES), 1)
    lo = lane < 64
    out_lane = lax.broadcasted_iota(jnp.int32, (WINDOW, LANES), 1)

    def blk(i, c):
        r0 = pl.multiple_of(i * WINDOW, WINDOW)
        cmin = jnp.where((t_idx == 0) & (i == 0), WINDOW, 0)
        valid = band & (col >= cmin)
        kcats, vcats = [], []
        for m in range(2):
            kt = kv_buf[pl.ds(r0, 2 * WINDOW), m * LANES:(m + 1) * LANES]
            vt = kv_buf[pl.ds(r0, 2 * WINDOW), 256 + m * LANES:256 + (m + 1) * LANES]
            ks, vs = _swap_halves(kt), _swap_halves(vt)
            zero = jnp.zeros_like(kt)
            for e in range(2):
                k_lo = jnp.where(lo, kt if e == 0 else ks, zero)
                k_hi = jnp.where(lo, zero, ks if e == 0 else kt)
                v_lo = jnp.where(lo, vt if e == 0 else vs, zero)
                v_hi = jnp.where(lo, zero, vs if e == 0 else vt)
                kcats.append(jnp.concatenate([k_lo, k_hi], axis=0))
                vcats.append(jnp.concatenate([v_lo, v_hi], axis=0))

        def score(j):
            qt = jnp.concatenate([q_ref[pl.ds(r0, WINDOW), p * LANES:(p + 1) * LANES]
                                  for p in (2 * j, 2 * j + 1)], axis=0)
            return lax.dot_general(qt, kcats[j], (((1,), (1,)), ((), ())),
                                   preferred_element_type=F32)

        def attend(j, s):
            pcats, scales = [], []
            for t in range(2):
                p = 2 * j + t
                probs, invs = [], []
                for hh in range(2):
                    sink = sink_ref[2 * p + hh]
                    sh = jnp.where(valid, s[t * WINDOW:(t + 1) * WINDOW, hh * 256:(hh + 1) * 256], -jnp.inf)
                    mx = jnp.maximum(jnp.max(sh, axis=-1, keepdims=True), sink)
                    pe = jnp.exp(sh - mx)
                    den = jnp.sum(pe, axis=-1, keepdims=True) + jnp.exp(sink - mx)
                    probs.append(pe.astype(BF16))
                    invs.append(1.0 / den)
                pcats.append(jnp.concatenate(probs, axis=1))
                scales.append(jnp.where(out_lane < 64, invs[0], invs[1]))
            o = jnp.dot(jnp.concatenate(pcats, axis=0), vcats[j], preferred_element_type=F32)
            for t in range(2):
                p = 2 * j + t
                o_ref[pl.ds(r0, WINDOW), p * LANES:(p + 1) * LANES] = (
                    o[t * WINDOW:(t + 1) * WINDOW] * scales[t]).astype(o_ref.dtype)

        s_next = score(0)
        for j in range(ATTN_KV_HEADS):
            s_cur = s_next
            if j + 1 < ATTN_KV_HEADS:
                s_next = score(j + 1)
            attend(j, s_cur)
        return c

    lax.fori_loop(0, _exact_div(tq, WINDOW), blk, 0)


def _swa(proj, sinks_l, *, batch, seq, tq):
    n = proj.shape[0]
    nt = _exact_div(seq, tq)
    qw = ATTN_Q_HEADS * ATTN_HEAD_DIM
    kvw = 2 * ATTN_KV_HEADS * ATTN_HEAD_DIM
    kv_blk = _exact_div(COL_KV, kvw)
    bpt = _exact_div(tq, WINDOW)

    def prev_map(b, t):
        return (jnp.maximum(b * _exact_div(seq, WINDOW) + t * bpt - 1, 0), kv_blk)

    return pl.pallas_call(
        functools.partial(_swa_kernel, tq=tq),
        grid=(batch, nt),
        in_specs=[pl.BlockSpec(memory_space=pltpu.SMEM),
                  pl.BlockSpec((tq, qw), lambda b, t: (b * nt + t, _exact_div(COL_QA, qw))),
                  pl.BlockSpec((tq, kvw), lambda b, t: (b * nt + t, kv_blk)),
                  pl.BlockSpec((WINDOW, kvw), prev_map)],
        out_specs=pl.BlockSpec((tq, qw), lambda b, t: (b * nt + t, 0)),
        out_shape=jax.ShapeDtypeStruct((n, qw), BF16),
        scratch_shapes=[pltpu.VMEM((tq + WINDOW, kvw), BF16)],
        compiler_params=_params(("parallel", "parallel")),
        name="swa",
    )(sinks_l, proj, proj, proj)


def _sgu_kernel(u_ref, v_ref, lng_ref, lnb_ref, w_ref, bt_ref, o_ref, *, ts):
    row = lax.broadcasted_iota(jnp.int32, (SGU_CHUNK, SGU_CHUNK), 0)
    col = lax.broadcasted_iota(jnp.int32, (SGU_CHUNK, SGU_CHUNK), 1)
    tril = row >= col
    lng = lng_ref[...]
    lnb = lnb_ref[...]

    def chunk(c, carry):
        r = pl.ds(pl.multiple_of(c * SGU_CHUNK, SGU_CHUNK), SGU_CHUNK)
        v = jax.nn.gelu(v_ref[r, :].astype(F32))
        mu = jnp.mean(v, axis=-1, keepdims=True)
        vc = v - mu
        var = jnp.mean(vc * vc, axis=-1, keepdims=True)
        vn = (vc * lax.rsqrt(var + EPS) * lng + lnb).astype(BF16)
        for g in range(SGU_GROUPS):
            cs = slice(g * LANES, (g + 1) * LANES)
            w = jnp.where(tril, w_ref[g], 0.0).astype(BF16)
            mixed = jnp.dot(w, vn[:, cs], preferred_element_type=F32) + bt_ref[:, g:g + 1]
            u = jax.nn.gelu(u_ref[r, cs].astype(F32))
            o_ref[r, cs] = (u * mixed).astype(o_ref.dtype)
        return carry

    lax.fori_loop(0, _exact_div(ts, SGU_CHUNK), chunk, 0)


def _sgu(proj, ln_g, ln_b, w, bt, layer, *, ts):
    n = proj.shape[0]
    width = SGU_GROUPS * LANES
    return pl.pallas_call(
        functools.partial(_sgu_kernel, ts=ts),
        grid=(_exact_div(n, ts),),
        in_specs=[pl.BlockSpec((ts, width), lambda i: (i, _exact_div(COL_UB, width))),
                  pl.BlockSpec((ts, width), lambda i: (i, _exact_div(COL_VB, width))),
                  pl.BlockSpec((None, 1, width), lambda i: (layer, 0, 0)),
                  pl.BlockSpec((None, 1, width), lambda i: (layer, 0, 0)),
                  pl.BlockSpec((None, SGU_GROUPS, SGU_CHUNK, SGU_CHUNK), lambda i: (layer, 0, 0, 0)),
                  pl.BlockSpec((None, SGU_CHUNK, SGU_GROUPS), lambda i: (layer, 0, 0))],
        out_specs=pl.BlockSpec((ts, width), lambda i: (i, 0)),
        out_shape=jax.ShapeDtypeStruct((n, width), BF16),
        compiler_params=_params(("parallel",)),
        name="sgu",
    )(proj, proj, ln_g, ln_b, w, bt)


def _hgrn_kernel(f_ref, i_ref, q_ref, g_ref, lb_ref, ng_ref, o_ref, state_ref, kqb_ref, a_ref, *, tt, hb):
    C, SUB = HGRN_CHUNK, HGRN_SUB

    @pl.when(pl.program_id(2) == 0)
    def _():
        state_ref[...] = jnp.zeros_like(state_ref)

    ng = ng_ref[...]
    r64 = lax.broadcasted_iota(jnp.int32, (C, C), 0)
    c64 = lax.broadcasted_iota(jnp.int32, (C, C), 1)
    tril = jnp.where(r64 >= c64, 1.0, 0.0).astype(BF16)
    t_i = lax.broadcasted_iota(jnp.int32, (C, LANES), 0)
    l_i = lax.broadcasted_iota(jnp.int32, (C, LANES), 1)
    own_half = ((l_i // C) == ((t_i // SUB) % 2)) & ((l_i % C) <= t_i)
    valid0 = own_half & (t_i < 2 * SUB)
    valid1 = own_half & (t_i >= 2 * SUB)

    def gates(r, h):
        cs = slice(h * LANES, (h + 1) * LANES)
        lb = lb_ref[:, cs]
        half_span = 0.5 * (1.0 - lb)
        f = (lb + half_span) + half_span * jnp.tanh(0.5 * f_ref[r, cs].astype(F32))
        lf = jnp.log2(f)
        k = 1.0 - f
        x_q = q_ref[r, cs].astype(F32)
        qf = x_q * ((0.5 * HGRN_HEAD_DIM ** -0.5) + (0.5 * HGRN_HEAD_DIM ** -0.5) * jnp.tanh(0.5 * x_q))
        hi = lf.astype(BF16)
        lo = (lf - hi.astype(F32)).astype(BF16)
        b2 = jnp.dot(tril, jnp.concatenate([hi, lo], axis=1), preferred_element_type=F32)
        b = b2[:, :LANES] + b2[:, LANES:]
        return k, qf, b

    def scores(k, qf, b):
        refs = [jnp.zeros((1, LANES), F32)] + [b[SUB * i - 1:SUB * i, :] for i in range(1, _exact_div(C, SUB))]
        q_rel = jnp.concatenate(
            [(qf[SUB * i:SUB * (i + 1)] * jnp.exp2(b[SUB * i:SUB * (i + 1)] - rf)).astype(BF16)
             for i, rf in enumerate(refs)], axis=0)
        blocks = []
        for i, rf in enumerate(refs):
            rows = SUB * (i + 1)
            blocks.append((k[:rows] * jnp.exp2(rf - b[:rows])).astype(BF16))
            if rows < C:
                blocks.append(jnp.zeros((C - rows, LANES), BF16))
        kcat = jnp.concatenate(blocks, axis=0)
        sc = lax.dot_general(q_rel, kcat, (((1,), (1,)), ((), ())), preferred_element_type=F32)
        return jnp.where(valid0, sc[:, :LANES], jnp.where(valid1, sc[:, LANES:], 0.0))

    def outputs(r, h, k, qf, b, a):
        cs = slice(h * LANES, (h + 1) * LANES)
        v = i_ref[r, cs]
        b_last = b[C - 1:C, :]
        q_abs = (qf * jnp.exp2(b)).astype(BF16)
        vv = jnp.concatenate([v, v], axis=0)
        st = state_ref[h]
        o = (lax.dot_general(q_abs, st.astype(BF16), (((1,), (1,)), ((), ())), preferred_element_type=F32)
             + jnp.dot(a.astype(BF16), vv, preferred_element_type=F32))
        k_end = (k * jnp.exp2(b_last - b)).astype(BF16)
        state_ref[h] = st * jnp.exp2(b_last) + lax.dot_general(
            v, k_end, (((0,), (0,)), ((), ())), preferred_element_type=F32)
        on = o * lax.rsqrt(jnp.mean(o * o, axis=-1, keepdims=True) + EPS) * ng
        x_g = g_ref[r, cs].astype(F32)
        o_ref[r, cs] = (on * (x_g * (0.5 + 0.5 * jnp.tanh(0.5 * x_g)))).astype(o_ref.dtype)

    def rows(c):
        return pl.ds(pl.multiple_of(c * C, C), C)

    def stage_scores(r):
        kqb = [gates(r, h) for h in range(hb)]
        a = [scores(*t) for t in kqb]
        for h in range(hb):
            for j, val in enumerate(kqb[h]):
                kqb_ref[j, h] = val
            a_ref[h] = a[h].astype(BF16)

    n_chunks = _exact_div(tt, C)
    stage_scores(rows(0))

    def chunk(c, carry):
        staged = [(kqb_ref[0, h], kqb_ref[1, h], kqb_ref[2, h], a_ref[h]) for h in range(hb)]
        for h in range(hb):
            outputs(rows(c), h, *staged[h])
        stage_scores(rows(jnp.minimum(c + 1, n_chunks - 1)))
        return carry

    lax.fori_loop(0, n_chunks, chunk, 0)


def _hgrn(proj, lb_l, norm_g, layer, *, batch, seq, tt, hb):
    n = proj.shape[0]
    nt = _exact_div(seq, tt)
    width = HGRN_HEADS * HGRN_HEAD_DIM
    bw = hb * LANES

    def col(off):
        return lambda b, h, t: (b * nt + t, _exact_div(off, bw) + h)

    return pl.pallas_call(
        functools.partial(_hgrn_kernel, tt=tt, hb=hb),
        grid=(batch, _exact_div(HGRN_HEADS, hb), nt),
        in_specs=[pl.BlockSpec((tt, bw), col(COL_FC)),
                  pl.BlockSpec((tt, bw), col(COL_IC)),
                  pl.BlockSpec((tt, bw), col(COL_QC)),
                  pl.BlockSpec((tt, bw), col(COL_GC)),
                  pl.BlockSpec((1, bw), lambda b, h, t: (0, h)),
                  pl.BlockSpec((None, 1, LANES), lambda b, h, t: (layer, 0, 0))],
        out_specs=pl.BlockSpec((tt, bw), lambda b, h, t: (b * nt + t, h)),
        out_shape=jax.ShapeDtypeStruct((n, width), BF16),
        scratch_shapes=[pltpu.VMEM((hb, HGRN_HEAD_DIM, HGRN_HEAD_DIM), F32),
                        pltpu.VMEM((3, hb, HGRN_CHUNK, LANES), F32),
                        pltpu.VMEM((hb, HGRN_CHUNK, LANES), BF16)],
        compiler_params=_params(("parallel", "parallel", "arbitrary")),
        name="hgrn2",
    )(proj, proj, proj, proj, lb_l, norm_g)


def _merge_kernel(x_ref, ya_ref, yb_ref, yc_ref, ga_ref, gb_ref, gc_ref,
                  wa_ref, wb_ref, wc_ref, wo_ref, o_ref, *, tk):
    d = o_ref.shape[1]
    ya, yb, yc = ya_ref[...], yb_ref[...], yc_ref[...]

    def merged(c):
        cs = slice(c * tk, (c + 1) * tk)
        m = (ga_ref[:, cs].astype(F32) * jnp.dot(ya, wa_ref[:, cs], preferred_element_type=F32)
             + gb_ref[:, cs].astype(F32) * jnp.dot(yb, wb_ref[:, cs], preferred_element_type=F32)
             + gc_ref[:, cs].astype(F32) * jnp.dot(yc, wc_ref[:, cs], preferred_element_type=F32))
        return m.astype(BF16)

    n_groups = _exact_div(d, tk)
    acc = x_ref[...]
    m_next = merged(0)
    for c in range(n_groups):
        m_cur = m_next
        if c + 1 < n_groups:
            m_next = merged(c + 1)
        acc = acc + jnp.dot(m_cur, wo_ref[c * tk:(c + 1) * tk, :], preferred_element_type=F32)
    o_ref[...] = acc


def _merge(x, ya, yb, yc, proj, wa, wb, wc, wo, layer, *, tm, tk):
    n, d = x.shape
    wbr = ya.shape[1]
    g0 = _exact_div(COL_GATE, d)

    def gate(br):
        return pl.BlockSpec((tm, d), lambda i: (i, g0 + br))

    def resident(rows):
        return pl.BlockSpec((None, rows, d), lambda i: (layer, 0, 0), pipeline_mode=pl.Buffered(1))

    y_spec = pl.BlockSpec((tm, wbr), lambda i: (i, 0))
    return pl.pallas_call(
        functools.partial(_merge_kernel, tk=tk),
        grid=(_exact_div(n, tm),),
        in_specs=[pl.BlockSpec((tm, d), lambda i: (i, 0)),
                  y_spec, y_spec, y_spec, gate(0), gate(1), gate(2),
                  resident(wbr), resident(wbr), resident(wbr), resident(d)],
        out_specs=pl.BlockSpec((tm, d), lambda i: (i, 0)),
        out_shape=jax.ShapeDtypeStruct((n, d), F32),
        compiler_params=_params(("parallel",)),
        name="merge",
    )(x, ya, yb, yc, proj, proj, proj, wa, wb, wc, wo)


def _xattn_kernel(x_ref, g_ref, wq_ref, kv_ref, wo_ref, o_ref, h_ref, *, tm):
    _rms_rows(x_ref, g_ref, h_ref, tm)
    q = jnp.dot(h_ref[...], wq_ref[...], preferred_element_type=F32).astype(BF16)
    xw = X_HEADS * X_HEAD_DIM
    scale = X_HEAD_DIM ** -0.5

    def score(hd):
        cs = slice(hd * X_HEAD_DIM, (hd + 1) * X_HEAD_DIM)
        return lax.dot_general(q[:, cs], kv_ref[:, cs], (((1,), (1,)), ((), ())),
                               preferred_element_type=F32) * scale

    def attend(hd, s):
        vh = kv_ref[:, xw + hd * X_HEAD_DIM:xw + (hd + 1) * X_HEAD_DIM]
        mx = jnp.max(s, axis=-1, keepdims=True)
        pe = jnp.exp(s - mx)
        inv = 1.0 / jnp.sum(pe, axis=-1, keepdims=True)
        return (jnp.dot(pe.astype(BF16), vh, preferred_element_type=F32) * inv).astype(BF16)

    outs = []
    s_next = score(0)
    for hd in range(X_HEADS):
        s_cur = s_next
        if hd + 1 < X_HEADS:
            s_next = score(hd + 1)
        outs.append(attend(hd, s_cur))
    oc = jnp.concatenate(outs, axis=1)
    o_ref[...] = x_ref[...] + jnp.dot(oc, wo_ref[...], preferred_element_type=F32)


def _xattn(x, g, wq, kv, wo, layer, *, seq, mem_len, tm):
    n, d = x.shape
    xw = X_HEADS * X_HEAD_DIM
    tiles_per_seq = _exact_div(seq, tm)
    return pl.pallas_call(
        functools.partial(_xattn_kernel, tm=tm),
        grid=(_exact_div(n, tm),),
        in_specs=[pl.BlockSpec((tm, d), lambda i: (i, 0)),
                  pl.BlockSpec((None, 1, d), lambda i: (layer, 0, 0)),
                  pl.BlockSpec((None, d, xw), lambda i: (layer, 0, 0)),
                  pl.BlockSpec((mem_len, 2 * xw), lambda i: (i // tiles_per_seq, layer)),
                  pl.BlockSpec((None, xw, d), lambda i: (layer, 0, 0))],
        out_specs=pl.BlockSpec((tm, d), lambda i: (i, 0)),
        out_shape=jax.ShapeDtypeStruct((n, d), F32),
        scratch_shapes=[pltpu.VMEM((tm, d), BF16)],
        compiler_params=_params(("parallel",)),
        name="xattn",
    )(x, g, wq, kv, wo)


def _ffn_kernel(x_ref, g_ref, fg_ref, wg_ref, wu_ref, wo_ref, o_ref, h_ref, *, tm, final_norm):
    @pl.when(pl.program_id(1) == 0)
    def _():
        _rms_rows(x_ref, g_ref, h_ref, tm, copy_ref=o_ref)

    h = h_ref[...]
    gt = jnp.dot(h, wg_ref[...], preferred_element_type=F32)
    up = jnp.dot(h, wu_ref[...], preferred_element_type=F32)
    a = (jax.nn.silu(gt) * up).astype(BF16)
    o_ref[...] += jnp.dot(a, wo_ref[...], preferred_element_type=F32)

    if final_norm:
        @pl.when(pl.program_id(1) == pl.num_programs(1) - 1)
        def _():
            _rms_rows(o_ref, fg_ref, o_ref, tm)


def _ffn(x, g, final_g, w_in, w_out, layer, *, tm, tf, final_norm):
    n, d = x.shape
    nf = _exact_div(w_out.shape[1], tf)
    return pl.pallas_call(
        functools.partial(_ffn_kernel, tm=tm, final_norm=final_norm),
        grid=(_exact_div(n, tm), nf),
        in_specs=[pl.BlockSpec((tm, d), lambda i, f: (i, 0)),
                  pl.BlockSpec((None, 1, d), lambda i, f: (layer, 0, 0)),
                  pl.BlockSpec((1, d), lambda i, f: (0, 0)),
                  pl.BlockSpec((None, d, tf), lambda i, f: (layer, 0, f)),
                  pl.BlockSpec((None, d, tf), lambda i, f: (layer, 0, nf + f)),
                  pl.BlockSpec((None, tf, d), lambda i, f: (layer, f, 0))],
        out_specs=pl.BlockSpec((tm, d), lambda i, f: (i, 0)),
        out_shape=jax.ShapeDtypeStruct((n, d), F32),
        scratch_shapes=[pltpu.VMEM((tm, d), BF16)],
        compiler_params=_params(("parallel", "arbitrary")),
        name="ffn",
    )(x, g, final_g, w_in, w_in, w_out)


def kernel(x, mem, norm_mix, w_in, w_gate, sinks, sgu_ln_g, sgu_ln_b, sgu_w, sgu_b, hgrn_lb, hgrn_norm,
           w_br_a, w_br_b, w_br_c, w_out, norm_x, mem_norm, w_xq, w_xkv, w_xo, norm_ffn, w_ffn_in,
           w_ffn_out, final_norm):
    batch, seq, d = x.shape
    depth = w_in.shape[0]
    mem_len = mem.shape[1]
    n = batch * seq

    q_end = ATTN_Q_HEADS * ATTN_HEAD_DIM
    kv_end = q_end + 2 * ATTN_KV_HEADS * ATTN_HEAD_DIM
    w_all = jnp.concatenate([w_gate.astype(BF16), w_in[..., :q_end].astype(BF16),
                             w_in[..., kv_end:].astype(BF16), w_in[..., q_end:kv_end].astype(BF16)], axis=-1)
    assert w_all.shape[-1] == COL_END
    wa, wb, wc, wo = (w.astype(BF16) for w in (w_br_a, w_br_b, w_br_c, w_out))
    wxq, wxo = w_xq.astype(BF16), w_xo.astype(BF16)
    wxkv = jnp.concatenate([w_xkv[l] for l in range(depth)], axis=-1).astype(BF16)
    wfi, wfo = w_ffn_in.astype(BF16), w_ffn_out.astype(BF16)
    sgu_bt = jnp.swapaxes(sgu_b, 1, 2)
    sm = jax.nn.softmax(hgrn_lb.astype(F32), axis=0)
    lb_all = jnp.cumsum(sm, axis=0) - sm[0:1]
    sgu_ln_g, sgu_ln_b, hgrn_norm, norm_x, norm_ffn = (
        p[:, None, :] for p in (sgu_ln_g, sgu_ln_b, hgrn_norm, norm_x, norm_ffn))

    xf = x.reshape(n, d)
    kv_all = _norm_matmul(mem.reshape(batch * mem_len, d), mem_norm.reshape(1, d), wxkv, 0,
                          tm=1024, tn=512)
    for l in range(depth):
        proj = _norm_matmul(xf, norm_mix[l:l + 1], w_all, l, tm=1024, tn=1536, activate=True)
        ya = _swa(proj, sinks[l], batch=batch, seq=seq, tq=512)
        yb = _sgu(proj, sgu_ln_g, sgu_ln_b, sgu_w, sgu_bt, l, ts=512)
        yc = _hgrn(proj, lb_all[l:l + 1], hgrn_norm, l, batch=batch, seq=seq, tt=1024, hb=8)
        xf = _merge(xf, ya, yb, yc, proj, wa, wb, wc, wo, l, tm=256, tk=512)
        xf = _xattn(xf, norm_x, wxq, kv_all, wxo, l, seq=seq, mem_len=mem_len, tm=512)
        xf = _ffn(xf, norm_ffn, final_norm.reshape(1, d), wfi, wfo, l, tm=1024, tf=512,
                  final_norm=(l == depth - 1))
    return xf.reshape(batch, seq, d)
```

```python
import functools

import jax
import jax.numpy as jnp
from jax import lax
from jax.experimental import pallas as pl
from jax.experimental.pallas import tpu as pltpu

F32 = jnp.float32
BF16 = jnp.bfloat16
EPS = 1e-6

LANES = 128
V7X_VMEM_LIMIT = 56 * 1024 * 1024

ATTN_HEAD_DIM = 64
ATTN_Q_HEADS = 16
ATTN_KV_HEADS = 4
WINDOW = 128
SGU_CHUNK = 128
SGU_GROUPS = 8
HGRN_HEADS = 8
HGRN_HEAD_DIM = 128
HGRN_CHUNK = 64
HGRN_SUB = 16
X_HEADS = 4
X_HEAD_DIM = 128

COL_GATE, COL_QA, COL_UB, COL_VB, COL_FC, COL_IC, COL_QC, COL_GC, COL_KV, COL_END = (
    0, 6144, 7168, 8192, 9216, 10240, 11264, 12288, 13312, 13824)
ACT_SUB = 512


def _exact_div(a, b):
    assert a % b == 0, (a, b)
    return a // b


def _params(sem):
    return pltpu.CompilerParams(dimension_semantics=sem, vmem_limit_bytes=V7X_VMEM_LIMIT)


def _rms_rows(x_ref, g_ref, h_ref, rows, copy_ref=None):
    slab = 64
    g = g_ref[...]

    def body(i, c):
        r = pl.ds(pl.multiple_of(i * slab, slab), slab)
        x = x_ref[r, :]
        if copy_ref is not None:
            copy_ref[r, :] = x
        ms = jnp.mean(x * x, axis=-1, keepdims=True)
        h_ref[r, :] = (x * lax.rsqrt(ms + EPS) * g).astype(h_ref.dtype)
        return c

    lax.fori_loop(0, _exact_div(rows, slab), body, 0, unroll=4)


def _in_range(c0, lo_hi):
    _exact_div(lo_hi[0], ACT_SUB), _exact_div(lo_hi[1], ACT_SUB)
    return (c0 >= lo_hi[0]) & (c0 < lo_hi[1])


def _projection_activation(x, c0, gate_tile):
    if gate_tile:
        return 0.5 + 0.5 * jnp.tanh(0.5 * x)
    return x * jnp.where(_in_range(c0, (COL_QA, COL_UB)), ATTN_HEAD_DIM ** -0.5, 1.0)


def _norm_matmul_kernel(x_ref, g_ref, w_ref, o_ref, h_ref, *, activate, tm, tn):
    j = pl.program_id(1)

    @pl.when(j == 0)
    def _():
        _rms_rows(x_ref, g_ref, h_ref, tm)

    if not activate:
        o_ref[...] = jnp.dot(h_ref[...], w_ref[...], preferred_element_type=F32).astype(o_ref.dtype)
    else:
        assert COL_GATE == 0
        n_gate_tiles = _exact_div(COL_QA, tn)

        def step(gate_tile):
            h = h_ref[...]
            for s in range(_exact_div(tn, ACT_SUB)):
                cs = slice(s * ACT_SUB, (s + 1) * ACT_SUB)
                acc = jnp.dot(h, w_ref[:, cs], preferred_element_type=F32)
                o_ref[:, cs] = _projection_activation(acc, j * tn + s * ACT_SUB, gate_tile).astype(o_ref.dtype)

        pl.when(j < n_gate_tiles)(functools.partial(step, True))
        pl.when(j >= n_gate_tiles)(functools.partial(step, False))


def _norm_matmul(x, g, w, layer, *, tm, tn, activate=False):
    n, d = x.shape
    nout = w.shape[-1]
    kern = functools.partial(_norm_matmul_kernel, activate=activate, tm=tm, tn=tn)
    if w.ndim == 3:
        w_spec = pl.BlockSpec((None, d, tn), lambda i, j: (layer, 0, j))
    else:
        w_spec = pl.BlockSpec((d, tn), lambda i, j: (0, j))
    return pl.pallas_call(
        kern,
        grid=(_exact_div(n, tm), _exact_div(nout, tn)),
        in_specs=[pl.BlockSpec((tm, d), lambda i, j: (i, 0)),
                  pl.BlockSpec((1, d), lambda i, j: (0, 0)),
                  w_spec],
        out_specs=pl.BlockSpec((tm, tn), lambda i, j: (i, j)),
        out_shape=jax.ShapeDtypeStruct((n, nout), BF16),
        scratch_shapes=[pltpu.VMEM((tm, d), BF16)],
        compiler_params=_params(("parallel", "arbitrary")),
        name="norm_matmul",
    )(x, g, w)


def _swap_halves(t):
    return jnp.concatenate([t[:, 64:], t[:, :64]], axis=1)


def _swa_kernel(sink_ref, q_ref, kvc_ref, kvp_ref, o_ref, kv_buf, *, tq):
    t_idx = pl.program_id(1)
    kv_buf[0:WINDOW, :] = kvp_ref[...]
    kv_buf[WINDOW:, :] = kvc_ref[...]

    row = lax.broadcasted_iota(jnp.int32, (WINDOW, 2 * WINDOW), 0)
    col = lax.broadcasted_iota(jnp.int32, (WINDOW, 2 * WINDOW), 1)
    band = (col > row) & (col <= row + WINDOW)
    lane = lax.broadcasted_iota(jnp.int32, (2 * WINDOW, LANES), 1)
    lo = lane < 64
    out_lane = lax.broadcasted_iota(jnp.int32, (WINDOW, LANES), 1)

    def blk(i, c):
        r0 = pl.multiple_of(i * WINDOW, WINDOW)
        cmin = jnp.where((t_idx == 0) & (i == 0), WINDOW, 0)
        valid = band & (col >= cmin)
        kcats, vcats = [], []
        for m in range(2):
            kt = kv_buf[pl.ds(r0, 2 * WINDOW), m * LANES:(m + 1) * LANES]
            vt = kv_buf[pl.ds(r0, 2 * WINDOW), 256 + m * LANES:256 + (m + 1) * LANES]
            ks, vs = _swap_halves(kt), _swap_halves(vt)
            zero = jnp.zeros_like(kt)
            for e in range(2):
                k_lo = jnp.where(lo, kt if e == 0 else ks, zero)
                k_hi = jnp.where(lo, zero, ks if e == 0 else kt)
                v_lo = jnp.where(lo, vt if e == 0 else vs, zero)
                v_hi = jnp.where(lo, zero, vs if e == 0 else vt)
                kcats.append(jnp.concatenate([k_lo, k_hi], axis=0))
                vcats.append(jnp.concatenate([v_lo, v_hi], axis=0))

        def score(j):
            qt = jnp.concatenate([q_ref[pl.ds(r0, WINDOW), p * LANES:(p + 1) * LANES]
                                  for p in (2 * j, 2 * j + 1)], axis=0)
            return lax.dot_general(qt, kcats[j], (((1,), (1,)), ((), ())),
                                   preferred_element_type=F32)

        def attend(j, s):
            pcats, scales = [], []
            for t in range(2):
                p = 2 * j + t
                probs, invs = [], []
                for hh in range(2):
                    sink = sink_ref[2 * p + hh]
                    sh = jnp.where(valid, s[t * WINDOW:(t + 1) * WINDOW, hh * 256:(hh + 1) * 256], -jnp.inf)
                    mx = jnp.maximum(jnp.max(sh, axis=-1, keepdims=True), sink)
                    pe = jnp.exp(sh - mx)
                    den = jnp.sum(pe, axis=-1, keepdims=True) + jnp.exp(sink - mx)
                    probs.append(pe.astype(BF16))
                    invs.append(1.0 / den)
                pcats.append(jnp.concatenate(probs, axis=1))
                scales.append(jnp.where(out_lane < 64, invs[0], invs[1]))
            o = jnp.dot(jnp.concatenate(pcats, axis=0), vcats[j], preferred_element_type=F32)
            for t in range(2):
                p = 2 * j + t
                o_ref[pl.ds(r0, WINDOW), p * LANES:(p + 1) * LANES] = (
                    o[t * WINDOW:(t + 1) * WINDOW] * scales[t]).astype(o_ref.dtype)

        s_next = score(0)
        for j in range(ATTN_KV_HEADS):
            s_cur = s_next
            if j + 1 < ATTN_KV_HEADS:
                s_next = score(j + 1)
            attend(j, s_cur)
        return c

    lax.fori_loop(0, _exact_div(tq, WINDOW), blk, 0)


def _swa(proj, sinks_l, *, batch, seq, tq):
    n = proj.shape[0]
    nt = _exact_div(seq, tq)
    qw = ATTN_Q_HEADS * ATTN_HEAD_DIM
    kvw = 2 * ATTN_KV_HEADS * ATTN_HEAD_DIM
    kv_blk = _exact_div(COL_KV, kvw)
    bpt = _exact_div(tq, WINDOW)

    def prev_map(b, t):
        return (jnp.maximum(b * _exact_div(seq, WINDOW) + t * bpt - 1, 0), kv_blk)

    return pl.pallas_call(
        functools.partial(_swa_kernel, tq=tq),
        grid=(batch, nt),
        in_specs=[pl.BlockSpec(memory_space=pltpu.SMEM),
                  pl.BlockSpec((tq, qw), lambda b, t: (b * nt + t, _exact_div(COL_QA, qw))),
                  pl.BlockSpec((tq, kvw), lambda b, t: (b * nt + t, kv_blk)),
                  pl.BlockSpec((WINDOW, kvw), prev_map)],
        out_specs=pl.BlockSpec((tq, qw), lambda b, t: (b * nt + t, 0)),
        out_shape=jax.ShapeDtypeStruct((n, qw), BF16),
        scratch_shapes=[pltpu.VMEM((tq + WINDOW, kvw), BF16)],
        compiler_params=_params(("parallel", "parallel")),
        name="swa",
    )(sinks_l, proj, proj, proj)


def _sgu_kernel(u_ref, v_ref, lng_ref, lnb_ref, w_ref, bt_ref, o_ref, *, ts):
    row = lax.broadcasted_iota(jnp.int32, (SGU_CHUNK, SGU_CHUNK), 0)
    col = lax.broadcasted_iota(jnp.int32, (SGU_CHUNK, SGU_CHUNK), 1)
    tril = row >= col
    lng = lng_ref[...]
    lnb = lnb_ref[...]

    def chunk(c, carry):
        r = pl.ds(pl.multiple_of(c * SGU_CHUNK, SGU_CHUNK), SGU_CHUNK)
        v = jax.nn.gelu(v_ref[r, :].astype(F32))
        mu = jnp.mean(v, axis=-1, keepdims=True)
        vc = v - mu
        var = jnp.mean(vc * vc, axis=-1, keepdims=True)
        vn = (vc * lax.rsqrt(var + EPS) * lng + lnb).astype(BF16)
        for g in range(SGU_GROUPS):
            cs = slice(g * LANES, (g + 1) * LANES)
            w = jnp.where(tril, w_ref[g], 0.0).astype(BF16)
            mixed = jnp.dot(w, vn[:, cs], preferred_element_type=F32) + bt_ref[:, g:g + 1]
            u = jax.nn.gelu(u_ref[r, cs].astype(F32))
            o_ref[r, cs] = (u * mixed).astype(o_ref.dtype)
        return carry

    lax.fori_loop(0, _exact_div(ts, SGU_CHUNK), chunk, 0)


def _sgu(proj, ln_g, ln_b, w, bt, layer, *, ts):
    n = proj.shape[0]
    width = SGU_GROUPS * LANES
    return pl.pallas_call(
        functools.partial(_sgu_kernel, ts=ts),
        grid=(_exact_div(n, ts),),
        in_specs=[pl.BlockSpec((ts, width), lambda i: (i, _exact_div(COL_UB, width))),
                  pl.BlockSpec((ts, width), lambda i: (i, _exact_div(COL_VB, width))),
                  pl.BlockSpec((None, 1, width), lambda i: (layer, 0, 0)),
                  pl.BlockSpec((None, 1, width), lambda i: (layer, 0, 0)),
                  pl.BlockSpec((None, SGU_GROUPS, SGU_CHUNK, SGU_CHUNK), lambda i: (layer, 0, 0, 0)),
                  pl.BlockSpec((None, SGU_CHUNK, SGU_GROUPS), lambda i: (layer, 0, 0))],
        out_specs=pl.BlockSpec((ts, width), lambda i: (i, 0)),
        out_shape=jax.ShapeDtypeStruct((n, width), BF16),
        compiler_params=_params(("parallel",)),
        name="sgu",
    )(proj, proj, ln_g, ln_b, w, bt)


def _hgrn_kernel(f_ref, i_ref, q_ref, g_ref, lb_ref, ng_ref, o_ref, state_ref, kqb_ref, a_ref, *, tt, hb):
    C, SUB = HGRN_CHUNK, HGRN_SUB

    @pl.when(pl.program_id(2) == 0)
    def _():
        state_ref[...] = jnp.zeros_like(state_ref)

    ng = ng_ref[...]
    r64 = lax.broadcasted_iota(jnp.int32, (C, C), 0)
    c64 = lax.broadcasted_iota(jnp.int32, (C, C), 1)
    tril = jnp.where(r64 >= c64, 1.0, 0.0).astype(BF16)
    t_i = lax.broadcasted_iota(jnp.int32, (C, LANES), 0)
    l_i = lax.broadcasted_iota(jnp.int32, (C, LANES), 1)
    own_half = ((l_i // C) == ((t_i // SUB) % 2)) & ((l_i % C) <= t_i)
    valid0 = own_half & (t_i < 2 * SUB)
    valid1 = own_half & (t_i >= 2 * SUB)

    def gates(r, h):
        cs = slice(h * LANES, (h + 1) * LANES)
        lb = lb_ref[:, cs]
        half_span = 0.5 * (1.0 - lb)
        f = (lb + half_span) + half_span * jnp.tanh(0.5 * f_ref[r, cs].astype(F32))
        lf = jnp.log2(f)
        k = 1.0 - f
        x_q = q_ref[r, cs].astype(F32)
        qf = x_q * ((0.5 * HGRN_HEAD_DIM ** -0.5) + (0.5 * HGRN_HEAD_DIM ** -0.5) * jnp.tanh(0.5 * x_q))
        hi = lf.astype(BF16)
        lo = (lf - hi.astype(F32)).astype(BF16)
        b2 = jnp.dot(tril, jnp.concatenate([hi, lo], axis=1), preferred_element_type=F32)
        b = b2[:, :LANES] + b2[:, LANES:]
        return k, qf, b

    def scores(k, qf, b):
        refs = [jnp.zeros((1, LANES), F32)] + [b[SUB * i - 1:SUB * i, :] for i in range(1, _exact_div(C, SUB))]
        q_rel = jnp.concatenate(
            [(qf[SUB * i:SUB * (i + 1)] * jnp.exp2(b[SUB * i:SUB * (i + 1)] - rf)).astype(BF16)
             for i, rf in enumerate(refs)], axis=0)
        blocks = []
        for i, rf in enumerate(refs):
            rows = SUB * (i + 1)
            blocks.append((k[:rows] * jnp.exp2(rf - b[:rows])).astype(BF16))
            if rows < C:
                blocks.append(jnp.zeros((C - rows, LANES), BF16))
        kcat = jnp.concatenate(blocks, axis=0)
        sc = lax.dot_general(q_rel, kcat, (((1,), (1,)), ((), ())), preferred_element_type=F32)
        return jnp.where(valid0, sc[:, :LANES], jnp.where(valid1, sc[:, LANES:], 0.0))

    def outputs(r, h, k, qf, b, a):
        cs = slice(h * LANES, (h + 1) * LANES)
        v = i_ref[r, cs]
        b_last = b[C - 1:C, :]
        q_abs = (qf * jnp.exp2(b)).astype(BF16)
        vv = jnp.concatenate([v, v], axis=0)
        st = state_ref[h]
        o = (lax.dot_general(q_abs, st.astype(BF16), (((1,), (1,)), ((), ())), preferred_element_type=F32)
             + jnp.dot(a.astype(BF16), vv, preferred_element_type=F32))
        k_end = (k * jnp.exp2(b_last - b)).astype(BF16)
        state_ref[h] = st * jnp.exp2(b_last) + lax.dot_general(
            v, k_end, (((0,), (0,)), ((), ())), preferred_element_type=F32)
        on = o * lax.rsqrt(jnp.mean(o * o, axis=-1, keepdims=True) + EPS) * ng
        x_g = g_ref[r, cs].astype(F32)
        o_ref[r, cs] = (on * (x_g * (0.5 + 0.5 * jnp.tanh(0.5 * x_g)))).astype(o_ref.dtype)

    def rows(c):
        return pl.ds(pl.multiple_of(c * C, C), C)

    def stage_scores(r):
        kqb = [gates(r, h) for h in range(hb)]
        a = [scores(*t) for t in kqb]
        for h in range(hb):
            for j, val in enumerate(kqb[h]):
                kqb_ref[j, h] = val
            a_ref[h] = a[h].astype(BF16)

    n_chunks = _exact_div(tt, C)
    stage_scores(rows(0))

    def chunk(c, carry):
        staged = [(kqb_ref[0, h], kqb_ref[1, h], kqb_ref[2, h], a_ref[h]) for h in range(hb)]
        for h in range(hb):
            outputs(rows(c), h, *staged[h])
        stage_scores(rows(jnp.minimum(c + 1, n_chunks - 1)))
        return carry

    lax.fori_loop(0, n_chunks, chunk, 0)


def _hgrn(proj, lb_l, norm_g, layer, *, batch, seq, tt, hb):
    n = proj.shape[0]
    nt = _exact_div(seq, tt)
    width = HGRN_HEADS * HGRN_HEAD_DIM
    bw = hb * LANES

    def col(off):
        return lambda b, h, t: (b * nt + t, _exact_div(off, bw) + h)

    return pl.pallas_call(
        functools.partial(_hgrn_kernel, tt=tt, hb=hb),
        grid=(batch, _exact_div(HGRN_HEADS, hb), nt),
        in_specs=[pl.BlockSpec((tt, bw), col(COL_FC)),
                  pl.BlockSpec((tt, bw), col(COL_IC)),
                  pl.BlockSpec((tt, bw), col(COL_QC)),
                  pl.BlockSpec((tt, bw), col(COL_GC)),
                  pl.BlockSpec((1, bw), lambda b, h, t: (0, h)),
                  pl.BlockSpec((None, 1, LANES), lambda b, h, t: (layer, 0, 0))],
        out_specs=pl.BlockSpec((tt, bw), lambda b, h, t: (b * nt + t, h)),
        out_shape=jax.ShapeDtypeStruct((n, width), BF16),
        scratch_shapes=[pltpu.VMEM((hb, HGRN_HEAD_DIM, HGRN_HEAD_DIM), F32),
                        pltpu.VMEM((3, hb, HGRN_CHUNK, LANES), F32),
                        pltpu.VMEM((hb, HGRN_CHUNK, LANES), BF16)],
        compiler_params=_params(("parallel", "parallel", "arbitrary")),
        name="hgrn2",
    )(proj, proj, proj, proj, lb_l, norm_g)


def _merge_kernel(x_ref, ya_ref, yb_ref, yc_ref, ga_ref, gb_ref, gc_ref,
                  wa_ref, wb_ref, wc_ref, wo_ref, o_ref, *, tk):
    d = o_ref.shape[1]
    ya, yb, yc = ya_ref[...], yb_ref[...], yc_ref[...]

    def merged(c):
        cs = slice(c * tk, (c + 1) * tk)
        m = (ga_ref[:, cs].astype(F32) * jnp.dot(ya, wa_ref[:, cs], preferred_element_type=F32)
             + gb_ref[:, cs].astype(F32) * jnp.dot(yb, wb_ref[:, cs], preferred_element_type=F32)
             + gc_ref[:, cs].astype(F32) * jnp.dot(yc, wc_ref[:, cs], preferred_element_type=F32))
        return m.astype(BF16)

    n_groups = _exact_div(d, tk)
    acc = x_ref[...]
    m_next = merged(0)
    for c in range(n_groups):
        m_cur = m_next
        if c + 1 < n_groups:
            m_next = merged(c + 1)
        acc = acc + jnp.dot(m_cur, wo_ref[c * tk:(c + 1) * tk, :], preferred_element_type=F32)
    o_ref[...] = acc


def _merge(x, ya, yb, yc, proj, wa, wb, wc, wo, layer, *, tm, tk):
    n, d = x.shape
    wbr = ya.shape[1]
    g0 = _exact_div(COL_GATE, d)

    def gate(br):
        return pl.BlockSpec((tm, d), lambda i: (i, g0 + br))

    def resident(rows):
        return pl.BlockSpec((None, rows, d), lambda i: (layer, 0, 0), pipeline_mode=pl.Buffered(1))

    y_spec = pl.BlockSpec((tm, wbr), lambda i: (i, 0))
    return pl.pallas_call(
        functools.partial(_merge_kernel, tk=tk),
        grid=(_exact_div(n, tm),),
        in_specs=[pl.BlockSpec((tm, d), lambda i: (i, 0)),
                  y_spec, y_spec, y_spec, gate(0), gate(1), gate(2),
                  resident(wbr), resident(wbr), resident(wbr), resident(d)],
        out_specs=pl.BlockSpec((tm, d), lambda i: (i, 0)),
        out_shape=jax.ShapeDtypeStruct((n, d), F32),
        compiler_params=_params(("parallel",)),
        name="merge",
    )(x, ya, yb, yc, proj, proj, proj, wa, wb, wc, wo)


def _xattn_kernel(x_ref, g_ref, wq_ref, kv_ref, wo_ref, o_ref, h_ref, *, tm):
    _rms_rows(x_ref, g_ref, h_ref, tm)
    q = jnp.dot(h_ref[...], wq_ref[...], preferred_element_type=F32).astype(BF16)
    xw = X_HEADS * X_HEAD_DIM
    scale = X_HEAD_DIM ** -0.5

    def score(hd):
        cs = slice(hd * X_HEAD_DIM, (hd + 1) * X_HEAD_DIM)
        return lax.dot_general(q[:, cs], kv_ref[:, cs], (((1,), (1,)), ((), ())),
                               preferred_element_type=F32) * scale

    def attend(hd, s):
        vh = kv_ref[:, xw + hd * X_HEAD_DIM:xw + (hd + 1) * X_HEAD_DIM]
        mx = jnp.max(s, axis=-1, keepdims=True)
        pe = jnp.exp(s - mx)
        inv = 1.0 / jnp.sum(pe, axis=-1, keepdims=True)
        return (jnp.dot(pe.astype(BF16), vh, preferred_element_type=F32) * inv).astype(BF16)

    outs = []
    s_next = score(0)
    for hd in range(X_HEADS):
        s_cur = s_next
        if hd + 1 < X_HEADS:
            s_next = score(hd + 1)
        outs.append(attend(hd, s_cur))
    oc = jnp.concatenate(outs, axis=1)
    o_ref[...] = x_ref[...] + jnp.dot(oc, wo_ref[...], preferred_element_type=F32)


def _xattn(x, g, wq, kv, wo, layer, *, seq, mem_len, tm):
    n, d = x.shape
    xw = X_HEADS * X_HEAD_DIM
    tiles_per_seq = _exact_div(seq, tm)
    return pl.pallas_call(
        functools.partial(_xattn_kernel, tm=tm),
        grid=(_exact_div(n, tm),),
        in_specs=[pl.BlockSpec((tm, d), lambda i: (i, 0)),
                  pl.BlockSpec((None, 1, d), lambda i: (layer, 0, 0)),
                  pl.BlockSpec((None, d, xw), lambda i: (layer, 0, 0)),
                  pl.BlockSpec((mem_len, 2 * xw), lambda i: (i // tiles_per_seq, layer)),
                  pl.BlockSpec((None, xw, d), lambda i: (layer, 0, 0))],
        out_specs=pl.BlockSpec((tm, d), lambda i: (i, 0)),
        out_shape=jax.ShapeDtypeStruct((n, d), F32),
        scratch_shapes=[pltpu.VMEM((tm, d), BF16)],
        compiler_params=_params(("parallel",)),
        name="xattn",
    )(x, g, wq, kv, wo)


def _ffn_kernel(x_ref, g_ref, fg_ref, wg_ref, wu_ref, wo_ref, o_ref, h_ref, *, tm, final_norm):
    @pl.when(pl.program_id(1) == 0)
    def _():
        _rms_rows(x_ref, g_ref, h_ref, tm, copy_ref=o_ref)

    h = h_ref[...]
    gt = jnp.dot(h, wg_ref[...], preferred_element_type=F32)
    up = jnp.dot(h, wu_ref[...], preferred_element_type=F32)
    a = (jax.nn.silu(gt) * up).astype(BF16)
    o_ref[...] += jnp.dot(a, wo_ref[...], preferred_element_type=F32)

    if final_norm:
        @pl.when(pl.program_id(1) == pl.num_programs(1) - 1)
        def _():
            _rms_rows(o_ref, fg_ref, o_ref, tm)


def _ffn(x, g, final_g, w_in, w_out, layer, *, tm, tf, final_norm):
    n, d = x.shape
    nf = _exact_div(w_out.shape[1], tf)
    return pl.pallas_call(
        functools.partial(_ffn_kernel, tm=tm, final_norm=final_norm),
        grid=(_exact_div(n, tm), nf),
        in_specs=[pl.BlockSpec((tm, d), lambda i, f: (i, 0)),
                  pl.BlockSpec((None, 1, d), lambda i, f: (layer, 0, 0)),
                  pl.BlockSpec((1, d), lambda i, f: (0, 0)),
                  pl.BlockSpec((None, d, tf), lambda i, f: (layer, 0, f)),
                  pl.BlockSpec((None, d, tf), lambda i, f: (layer, 0, nf + f)),
                  pl.BlockSpec((None, tf, d), lambda i, f: (layer, f, 0))],
        out_specs=pl.BlockSpec((tm, d), lambda i, f: (i, 0)),
        out_shape=jax.ShapeDtypeStruct((n, d), F32),
        scratch_shapes=[pltpu.VMEM((tm, d), BF16)],
        compiler_params=_params(("parallel", "arbitrary")),
        name="ffn",
    )(x, g, final_g, w_in, w_in, w_out)


def kernel(x, mem, norm_mix, w_in, w_gate, sinks, sgu_ln_g, sgu_ln_b, sgu_w, sgu_b, hgrn_lb, hgrn_norm,
           w_br_a, w_br_b, w_br_c, w_out, norm_x, mem_norm, w_xq, w_xkv, w_xo, norm_ffn, w_ffn_in,
           w_ffn_out, final_norm):
    batch, seq, d = x.shape
    depth = w_in.shape[0]
    mem_len = mem.shape[1]
    n = batch * seq

    q_end = ATTN_Q_HEADS * ATTN_HEAD_DIM
    kv_end = q_end + 2 * ATTN_KV_HEADS * ATTN_HEAD_DIM
    w_all = jnp.concatenate([w_gate.astype(BF16), w_in[..., :q_end].astype(BF16),
                             w_in[..., kv_end:].astype(BF16), w_in[..., q_end:kv_end].astype(BF16)], axis=-1)
    assert w_all.shape[-1] == COL_END
    wa, wb, wc, wo = (w.astype(BF16) for w in (w_br_a, w_br_b, w_br_c, w_out))
    wxq, wxo = w_xq.astype(BF16), w_xo.astype(BF16)
    wxkv = jnp.concatenate([w_xkv[l] for l in range(depth)], axis=-1).astype(BF16)
    wfi, wfo = w_ffn_in.astype(BF16), w_ffn_out.astype(BF16)
    sgu_bt = jnp.swapaxes(sgu_b, 1, 2)
    sm = jax.nn.softmax(hgrn_lb.astype(F32), axis=0)
    lb_all = jnp.cumsum(sm, axis=0) - sm[0:1]
    sgu_ln_g, sgu_ln_b, hgrn_norm, norm_x, norm_ffn = (
        p[:, None, :] for p in (sgu_ln_g, sgu_ln_b, hgrn_norm, norm_x, norm_ffn))

    xf = x.reshape(n, d)
    kv_all = _norm_matmul(mem.reshape(batch * mem_len, d), mem_norm.reshape(1, d), wxkv, 0,
                          tm=1024, tn=512)
    for l in range(depth):
        proj = _norm_matmul(xf, norm_mix[l:l + 1], w_all, l, tm=1024, tn=1536, activate=True)
        ya = _swa(proj, sinks[l], batch=batch, seq=seq, tq=512)
        yb = _sgu(proj, sgu_ln_g, sgu_ln_b, sgu_w, sgu_bt, l, ts=512)
        yc = _hgrn(proj, lb_all[l:l + 1], hgrn_norm, l, batch=batch, seq=seq, tt=1024, hb=8)
        xf = _merge(xf, ya, yb, yc, proj, wa, wb, wc, wo, l, tm=256, tk=512)
        xf = _xattn(xf, norm_x, wxq, kv_all, wxo, l, seq=seq, mem_len=mem_len, tm=512)
        xf = _ffn(xf, norm_ffn, final_norm.reshape(1, d), wfi, wfo, l, tm=1024, tf=512,
                  final_norm=(l == depth - 1))
    return xf.reshape(batch, seq, d)
```

```python
import functools

import jax
import jax.numpy as jnp
from jax import lax
from jax.experimental import pallas as pl
from jax.experimental.pallas import tpu as pltpu

F32 = jnp.float32
BF16 = jnp.bfloat16
EPS = 1e-6

LANES = 128
V7X_VMEM_LIMIT = 56 * 1024 * 1024

ATTN_HEAD_DIM = 64
ATTN_Q_HEADS = 16
ATTN_KV_HEADS = 4
WINDOW = 128
SGU_CHUNK = 128
SGU_GROUPS = 8
HGRN_HEADS = 8
HGRN_HEAD_DIM = 128
HGRN_CHUNK = 64
HGRN_SUB = 16
X_HEADS = 4
X_HEAD_DIM = 128

COL_GATE, COL_QA, COL_UB, COL_VB, COL_FC, COL_IC, COL_QC, COL_GC, COL_KV, COL_END = (
    0, 6144, 7168, 8192, 9216, 10240, 11264, 12288, 13312, 13824)
ACT_SUB = 512


def _exact_div(a, b):
    assert a % b == 0, (a, b)
    return a // b


def _params(sem):
    return pltpu.CompilerParams(dimension_semantics=sem, vmem_limit_bytes=V7X_VMEM_LIMIT)


def _rms_rows(x_ref, g_ref, h_ref, rows, copy_ref=None):
    slab = 64
    g = None if g_ref is None else g_ref[...]

    def body(i, c):
        r = pl.ds(pl.multiple_of(i * slab, slab), slab)
        x = x_ref[r, :]
        if copy_ref is not None:
            copy_ref[r, :] = x
        ms = jnp.mean(x * x, axis=-1, keepdims=True)
        y = x * lax.rsqrt(ms + EPS)
        h_ref[r, :] = (y if g is None else y * g).astype(h_ref.dtype)
        return c

    lax.fori_loop(0, _exact_div(rows, slab), body, 0, unroll=4)


def _in_range(c0, lo_hi):
    _exact_div(lo_hi[0], ACT_SUB), _exact_div(lo_hi[1], ACT_SUB)
    return (c0 >= lo_hi[0]) & (c0 < lo_hi[1])


def _projection_activation(x, c0, gate_tile):
    if gate_tile:
        return 0.5 + 0.5 * jnp.tanh(0.5 * x)
    return x * jnp.where(_in_range(c0, (COL_QA, COL_UB)), ATTN_HEAD_DIM ** -0.5, 1.0)


def _norm_matmul_kernel(x_ref, w_ref, o_ref, h_ref, *, activate, tm, tn):
    j = pl.program_id(1)

    @pl.when(j == 0)
    def _():
        _rms_rows(x_ref, None, h_ref, tm)

    if not activate:
        o_ref[...] = jnp.dot(h_ref[...], w_ref[...], preferred_element_type=F32).astype(o_ref.dtype)
    else:
        assert COL_GATE == 0
        n_gate_tiles = _exact_div(COL_QA, tn)

        def step(gate_tile):
            h = h_ref[...]
            for s in range(_exact_div(tn, ACT_SUB)):
                cs = slice(s * ACT_SUB, (s + 1) * ACT_SUB)
                acc = jnp.dot(h, w_ref[:, cs], preferred_element_type=F32)
                o_ref[:, cs] = _projection_activation(acc, j * tn + s * ACT_SUB, gate_tile).astype(o_ref.dtype)

        pl.when(j < n_gate_tiles)(functools.partial(step, True))
        pl.when(j >= n_gate_tiles)(functools.partial(step, False))


def _norm_matmul(x, w, layer, *, tm, tn, activate=False):
    n, d = x.shape
    nout = w.shape[-1]
    kern = functools.partial(_norm_matmul_kernel, activate=activate, tm=tm, tn=tn)
    if w.ndim == 3:
        w_spec = pl.BlockSpec((None, d, tn), lambda i, j: (layer, 0, j))
    else:
        w_spec = pl.BlockSpec((d, tn), lambda i, j: (0, j))
    return pl.pallas_call(
        kern,
        grid=(_exact_div(n, tm), _exact_div(nout, tn)),
        in_specs=[pl.BlockSpec((tm, d), lambda i, j: (i, 0)),
                  w_spec],
        out_specs=pl.BlockSpec((tm, tn), lambda i, j: (i, j)),
        out_shape=jax.ShapeDtypeStruct((n, nout), BF16),
        scratch_shapes=[pltpu.VMEM((tm, d), BF16)],
        compiler_params=_params(("parallel", "arbitrary")),
        name="norm_matmul",
    )(x, w)


def _swap_halves(t):
    return jnp.concatenate([t[:, 64:], t[:, :64]], axis=1)


def _swa_kernel(sink_ref, q_ref, kvc_ref, kvp_ref, o_ref, kv_buf, *, tq):
    t_idx = pl.program_id(1)
    kv_buf[0:WINDOW, :] = kvp_ref[...]
    kv_buf[WINDOW:, :] = kvc_ref[...]

    row = lax.broadcasted_iota(jnp.int32, (WINDOW, 2 * WINDOW), 0)
    col = lax.broadcasted_iota(jnp.int32, (WINDOW, 2 * WINDOW), 1)
    band = (col > row) & (col <= row + WINDOW)
    lane = lax.broadcasted_iota(jnp.int32, (2 * WINDOW, LANES), 1)
    lo = lane < 64
    out_lane = lax.broadcasted_iota(jnp.int32, (WINDOW, LANES), 1)

    def blk(i, c):
        r0 = pl.multiple_of(i * WINDOW, WINDOW)
        cmin = jnp.where((t_idx == 0) & (i == 0), WINDOW, 0)
        valid = band & (col >= cmin)
        kcats, vcats = [], []
        for m in range(2):
            kt = kv_buf[pl.ds(r0, 2 * WINDOW), m * LANES:(m + 1) * LANES]
            vt = kv_buf[pl.ds(r0, 2 * WINDOW), 256 + m * LANES:256 + (m + 1) * LANES]
            ks, vs = _swap_halves(kt), _swap_halves(vt)
            zero = jnp.zeros_like(kt)
            for e in range(2):
                k_lo = jnp.where(lo, kt if e == 0 else ks, zero)
                k_hi = jnp.where(lo, zero, ks if e == 0 else kt)
                v_lo = jnp.where(lo, vt if e == 0 else vs, zero)
                v_hi = jnp.where(lo, zero, vs if e == 0 else vt)
                kcats.append(jnp.concatenate([k_lo, k_hi], axis=0))
                vcats.append(jnp.concatenate([v_lo, v_hi], axis=0))

        def score(j):
            qt = jnp.concatenate([q_ref[pl.ds(r0, WINDOW), p * LANES:(p + 1) * LANES]
                                  for p in (2 * j, 2 * j + 1)], axis=0)
            return lax.dot_general(qt, kcats[j], (((1,), (1,)), ((), ())),
                                   preferred_element_type=F32)

        def attend(j, s):
            pcats, scales = [], []
            for t in range(2):
                p = 2 * j + t
                probs, invs = [], []
                for hh in range(2):
                    sink = sink_ref[2 * p + hh]
                    sh = jnp.where(valid, s[t * WINDOW:(t + 1) * WINDOW, hh * 256:(hh + 1) * 256], -jnp.inf)
                    mx = jnp.maximum(jnp.max(sh, axis=-1, keepdims=True), sink)
                    pe = jnp.exp(sh - mx)
                    den = jnp.sum(pe, axis=-1, keepdims=True) + jnp.exp(sink - mx)
                    probs.append(pe.astype(BF16))
                    invs.append(1.0 / den)
                pcats.append(jnp.concatenate(probs, axis=1))
                scales.append(jnp.where(out_lane < 64, invs[0], invs[1]))
            o = jnp.dot(jnp.concatenate(pcats, axis=0), vcats[j], preferred_element_type=F32)
            for t in range(2):
                p = 2 * j + t
                o_ref[pl.ds(r0, WINDOW), p * LANES:(p + 1) * LANES] = (
                    o[t * WINDOW:(t + 1) * WINDOW] * scales[t]).astype(o_ref.dtype)

        s_next = score(0)
        for j in range(ATTN_KV_HEADS):
            s_cur = s_next
            if j + 1 < ATTN_KV_HEADS:
                s_next = score(j + 1)
            attend(j, s_cur)
        return c

    lax.fori_loop(0, _exact_div(tq, WINDOW), blk, 0)


def _swa(proj, sinks_l, *, batch, seq, tq):
    n = proj.shape[0]
    nt = _exact_div(seq, tq)
    qw = ATTN_Q_HEADS * ATTN_HEAD_DIM
    kvw = 2 * ATTN_KV_HEADS * ATTN_HEAD_DIM
    kv_blk = _exact_div(COL_KV, kvw)
    bpt = _exact_div(tq, WINDOW)

    def prev_map(b, t):
        return (jnp.maximum(b * _exact_div(seq, WINDOW) + t * bpt - 1, 0), kv_blk)

    return pl.pallas_call(
        functools.partial(_swa_kernel, tq=tq),
        grid=(batch, nt),
        in_specs=[pl.BlockSpec(memory_space=pltpu.SMEM),
                  pl.BlockSpec((tq, qw), lambda b, t: (b * nt + t, _exact_div(COL_QA, qw))),
                  pl.BlockSpec((tq, kvw), lambda b, t: (b * nt + t, kv_blk)),
                  pl.BlockSpec((WINDOW, kvw), prev_map)],
        out_specs=pl.BlockSpec((tq, qw), lambda b, t: (b * nt + t, 0)),
        out_shape=jax.ShapeDtypeStruct((n, qw), BF16),
        scratch_shapes=[pltpu.VMEM((tq + WINDOW, kvw), BF16)],
        compiler_params=_params(("parallel", "parallel")),
        name="swa",
    )(sinks_l, proj, proj, proj)


def _sgu_kernel(u_ref, v_ref, lng_ref, lnb_ref, w_ref, bt_ref, o_ref, *, ts):
    row = lax.broadcasted_iota(jnp.int32, (SGU_CHUNK, SGU_CHUNK), 0)
    col = lax.broadcasted_iota(jnp.int32, (SGU_CHUNK, SGU_CHUNK), 1)
    tril = row >= col
    lng = lng_ref[...]
    lnb = lnb_ref[...]

    def chunk(c, carry):
        r = pl.ds(pl.multiple_of(c * SGU_CHUNK, SGU_CHUNK), SGU_CHUNK)
        v = jax.nn.gelu(v_ref[r, :].astype(F32))
        mu = jnp.mean(v, axis=-1, keepdims=True)
        vc = v - mu
        var = jnp.mean(vc * vc, axis=-1, keepdims=True)
        vn = (vc * lax.rsqrt(var + EPS) * lng + lnb).astype(BF16)
        for g in range(SGU_GROUPS):
            cs = slice(g * LANES, (g + 1) * LANES)
            w = jnp.where(tril, w_ref[g], 0.0).astype(BF16)
            mixed = jnp.dot(w, vn[:, cs], preferred_element_type=F32) + bt_ref[:, g:g + 1]
            u = jax.nn.gelu(u_ref[r, cs].astype(F32))
            o_ref[r, cs] = (u * mixed).astype(o_ref.dtype)
        return carry

    lax.fori_loop(0, _exact_div(ts, SGU_CHUNK), chunk, 0)


def _sgu(proj, ln_g, ln_b, w, bt, layer, *, ts):
    n = proj.shape[0]
    width = SGU_GROUPS * LANES
    return pl.pallas_call(
        functools.partial(_sgu_kernel, ts=ts),
        grid=(_exact_div(n, ts),),
        in_specs=[pl.BlockSpec((ts, width), lambda i: (i, _exact_div(COL_UB, width))),
                  pl.BlockSpec((ts, width), lambda i: (i, _exact_div(COL_VB, width))),
                  pl.BlockSpec((None, 1, width), lambda i: (layer, 0, 0)),
                  pl.BlockSpec((None, 1, width), lambda i: (layer, 0, 0)),
                  pl.BlockSpec((None, SGU_GROUPS, SGU_CHUNK, SGU_CHUNK), lambda i: (layer, 0, 0, 0)),
                  pl.BlockSpec((None, SGU_CHUNK, SGU_GROUPS), lambda i: (layer, 0, 0))],
        out_specs=pl.BlockSpec((ts, width), lambda i: (i, 0)),
        out_shape=jax.ShapeDtypeStruct((n, width), BF16),
        compiler_params=_params(("parallel",)),
        name="sgu",
    )(proj, proj, ln_g, ln_b, w, bt)


def _hgrn_kernel(f_ref, i_ref, q_ref, g_ref, lb_ref, ng_ref, o_ref, state_ref, kqb_ref, a_ref, *, tt, hb):
    C, SUB = HGRN_CHUNK, HGRN_SUB

    @pl.when(pl.program_id(2) == 0)
    def _():
        state_ref[...] = jnp.zeros_like(state_ref)

    ng = ng_ref[...]
    r64 = lax.broadcasted_iota(jnp.int32, (C, C), 0)
    c64 = lax.broadcasted_iota(jnp.int32, (C, C), 1)
    tril = jnp.where(r64 >= c64, 1.0, 0.0).astype(BF16)
    t_i = lax.broadcasted_iota(jnp.int32, (C, LANES), 0)
    l_i = lax.broadcasted_iota(jnp.int32, (C, LANES), 1)
    own_half = ((l_i // C) == ((t_i // SUB) % 2)) & ((l_i % C) <= t_i)
    valid0 = own_half & (t_i < 2 * SUB)
    valid1 = own_half & (t_i >= 2 * SUB)

    def gates(r, h):
        cs = slice(h * LANES, (h + 1) * LANES)
        lb = lb_ref[:, cs]
        half_span = 0.5 * (1.0 - lb)
        f = (lb + half_span) + half_span * jnp.tanh(0.5 * f_ref[r, cs].astype(F32))
        lf = jnp.log2(f)
        k = 1.0 - f
        x_q = q_ref[r, cs].astype(F32)
        qf = x_q * ((0.5 * HGRN_HEAD_DIM ** -0.5) + (0.5 * HGRN_HEAD_DIM ** -0.5) * jnp.tanh(0.5 * x_q))
        hi = lf.astype(BF16)
        lo = (lf - hi.astype(F32)).astype(BF16)
        b2 = jnp.dot(tril, jnp.concatenate([hi, lo], axis=1), preferred_element_type=F32)
        b = b2[:, :LANES] + b2[:, LANES:]
        return k, qf, b

    def scores(k, qf, b):
        refs = [jnp.zeros((1, LANES), F32)] + [b[SUB * i - 1:SUB * i, :] for i in range(1, _exact_div(C, SUB))]
        q_rel = jnp.concatenate(
            [(qf[SUB * i:SUB * (i + 1)] * jnp.exp2(b[SUB * i:SUB * (i + 1)] - rf)).astype(BF16)
             for i, rf in enumerate(refs)], axis=0)
        blocks = []
        for i, rf in enumerate(refs):
            rows = SUB * (i + 1)
            blocks.append((k[:rows] * jnp.exp2(rf - b[:rows])).astype(BF16))
            if rows < C:
                blocks.append(jnp.zeros((C - rows, LANES), BF16))
        kcat = jnp.concatenate(blocks, axis=0)
        sc = lax.dot_general(q_rel, kcat, (((1,), (1,)), ((), ())), preferred_element_type=F32)
        return jnp.where(valid0, sc[:, :LANES], jnp.where(valid1, sc[:, LANES:], 0.0))

    def outputs(r, h, k, qf, b, a):
        cs = slice(h * LANES, (h + 1) * LANES)
        v = i_ref[r, cs]
        b_last = b[C - 1:C, :]
        q_abs = (qf * jnp.exp2(b)).astype(BF16)
        vv = jnp.concatenate([v, v], axis=0)
        st = state_ref[h]
        o = (lax.dot_general(q_abs, st.astype(BF16), (((1,), (1,)), ((), ())), preferred_element_type=F32)
             + jnp.dot(a.astype(BF16), vv, preferred_element_type=F32))
        k_end = (k * jnp.exp2(b_last - b)).astype(BF16)
        state_ref[h] = st * jnp.exp2(b_last) + lax.dot_general(
            v, k_end, (((0,), (0,)), ((), ())), preferred_element_type=F32)
        on = o * lax.rsqrt(jnp.mean(o * o, axis=-1, keepdims=True) + EPS) * ng
        x_g = g_ref[r, cs].astype(F32)
        o_ref[r, cs] = (on * (x_g * (0.5 + 0.5 * jnp.tanh(0.5 * x_g)))).astype(o_ref.dtype)

    def rows(c):
        return pl.ds(pl.multiple_of(c * C, C), C)

    def stage_scores(r):
        kqb = [gates(r, h) for h in range(hb)]
        a = [scores(*t) for t in kqb]
        for h in range(hb):
            for j, val in enumerate(kqb[h]):
                kqb_ref[j, h] = val
            a_ref[h] = a[h].astype(BF16)

    n_chunks = _exact_div(tt, C)
    stage_scores(rows(0))

    def chunk(c, carry):
        staged = [(kqb_ref[0, h], kqb_ref[1, h], kqb_ref[2, h], a_ref[h]) for h in range(hb)]
        for h in range(hb):
            outputs(rows(c), h, *staged[h])
        stage_scores(rows(jnp.minimum(c + 1, n_chunks - 1)))
        return carry

    lax.fori_loop(0, n_chunks, chunk, 0)


def _hgrn(proj, lb_l, norm_g, layer, *, batch, seq, tt, hb):
    n = proj.shape[0]
    nt = _exact_div(seq, tt)
    width = HGRN_HEADS * HGRN_HEAD_DIM
    bw = hb * LANES

    def col(off):
        return lambda b, h, t: (b * nt + t, _exact_div(off, bw) + h)

    return pl.pallas_call(
        functools.partial(_hgrn_kernel, tt=tt, hb=hb),
        grid=(batch, _exact_div(HGRN_HEADS, hb), nt),
        in_specs=[pl.BlockSpec((tt, bw), col(COL_FC)),
                  pl.BlockSpec((tt, bw), col(COL_IC)),
                  pl.BlockSpec((tt, bw), col(COL_QC)),
                  pl.BlockSpec((tt, bw), col(COL_GC)),
                  pl.BlockSpec((1, bw), lambda b, h, t: (0, h)),
                  pl.BlockSpec((None, 1, LANES), lambda b, h, t: (layer, 0, 0))],
        out_specs=pl.BlockSpec((tt, bw), lambda b, h, t: (b * nt + t, h)),
        out_shape=jax.ShapeDtypeStruct((n, width), BF16),
        scratch_shapes=[pltpu.VMEM((hb, HGRN_HEAD_DIM, HGRN_HEAD_DIM), F32),
                        pltpu.VMEM((3, hb, HGRN_CHUNK, LANES), F32),
                        pltpu.VMEM((hb, HGRN_CHUNK, LANES), BF16)],
        compiler_params=_params(("parallel", "parallel", "arbitrary")),
        name="hgrn2",
    )(proj, proj, proj, proj, lb_l, norm_g)


def _merge_kernel(x_ref, ya_ref, yb_ref, yc_ref, ga_ref, gb_ref, gc_ref,
                  wa_ref, wb_ref, wc_ref, wo_ref, o_ref, *, tk):
    d = o_ref.shape[1]
    ya, yb, yc = ya_ref[...], yb_ref[...], yc_ref[...]

    def merged(c):
        cs = slice(c * tk, (c + 1) * tk)
        m = (ga_ref[:, cs].astype(F32) * jnp.dot(ya, wa_ref[:, cs], preferred_element_type=F32)
             + gb_ref[:, cs].astype(F32) * jnp.dot(yb, wb_ref[:, cs], preferred_element_type=F32)
             + gc_ref[:, cs].astype(F32) * jnp.dot(yc, wc_ref[:, cs], preferred_element_type=F32))
        return m.astype(BF16)

    n_groups = _exact_div(d, tk)
    acc = x_ref[...]
    m_next = merged(0)
    for c in range(n_groups):
        m_cur = m_next
        if c + 1 < n_groups:
            m_next = merged(c + 1)
        acc = acc + jnp.dot(m_cur, wo_ref[c * tk:(c + 1) * tk, :], preferred_element_type=F32)
    o_ref[...] = acc


def _merge(x, ya, yb, yc, proj, wa, wb, wc, wo, layer, *, tm, tk):
    n, d = x.shape
    wbr = ya.shape[1]
    g0 = _exact_div(COL_GATE, d)

    def gate(br):
        return pl.BlockSpec((tm, d), lambda i: (i, g0 + br))

    def resident(rows):
        return pl.BlockSpec((None, rows, d), lambda i: (layer, 0, 0), pipeline_mode=pl.Buffered(1))

    y_spec = pl.BlockSpec((tm, wbr), lambda i: (i, 0))
    return pl.pallas_call(
        functools.partial(_merge_kernel, tk=tk),
        grid=(_exact_div(n, tm),),
        in_specs=[pl.BlockSpec((tm, d), lambda i: (i, 0)),
                  y_spec, y_spec, y_spec, gate(0), gate(1), gate(2),
                  resident(wbr), resident(wbr), resident(wbr), resident(d)],
        out_specs=pl.BlockSpec((tm, d), lambda i: (i, 0)),
        out_shape=jax.ShapeDtypeStruct((n, d), F32),
        compiler_params=_params(("parallel",)),
        name="merge",
    )(x, ya, yb, yc, proj, proj, proj, wa, wb, wc, wo)


def _xattn_kernel(x_ref, wq_ref, kv_ref, wo_ref, o_ref, h_ref, *, tm):
    _rms_rows(x_ref, None, h_ref, tm)
    q = jnp.dot(h_ref[...], wq_ref[...], preferred_element_type=F32).astype(BF16)
    xw = X_HEADS * X_HEAD_DIM
    scale = X_HEAD_DIM ** -0.5

    def score(hd):
        cs = slice(hd * X_HEAD_DIM, (hd + 1) * X_HEAD_DIM)
        return lax.dot_general(q[:, cs], kv_ref[:, cs], (((1,), (1,)), ((), ())),
                               preferred_element_type=F32) * scale

    def attend(hd, s):
        vh = kv_ref[:, xw + hd * X_HEAD_DIM:xw + (hd + 1) * X_HEAD_DIM]
        mx = jnp.max(s, axis=-1, keepdims=True)
        pe = jnp.exp(s - mx)
        inv = 1.0 / jnp.sum(pe, axis=-1, keepdims=True)
        return (jnp.dot(pe.astype(BF16), vh, preferred_element_type=F32) * inv).astype(BF16)

    outs = []
    s_next = score(0)
    for hd in range(X_HEADS):
        s_cur = s_next
        if hd + 1 < X_HEADS:
            s_next = score(hd + 1)
        outs.append(attend(hd, s_cur))
    oc = jnp.concatenate(outs, axis=1)
    o_ref[...] = x_ref[...] + jnp.dot(oc, wo_ref[...], preferred_element_type=F32)


def _xattn(x, wq, kv, wo, layer, *, seq, mem_len, tm):
    n, d = x.shape
    xw = X_HEADS * X_HEAD_DIM
    tiles_per_seq = _exact_div(seq, tm)
    return pl.pallas_call(
        functools.partial(_xattn_kernel, tm=tm),
        grid=(_exact_div(n, tm),),
        in_specs=[pl.BlockSpec((tm, d), lambda i: (i, 0)),
                  pl.BlockSpec((None, d, xw), lambda i: (layer, 0, 0)),
                  pl.BlockSpec((mem_len, 2 * xw), lambda i: (i // tiles_per_seq, layer)),
                  pl.BlockSpec((None, xw, d), lambda i: (layer, 0, 0))],
        out_specs=pl.BlockSpec((tm, d), lambda i: (i, 0)),
        out_shape=jax.ShapeDtypeStruct((n, d), F32),
        scratch_shapes=[pltpu.VMEM((tm, d), BF16)],
        compiler_params=_params(("parallel",)),
        name="xattn",
    )(x, wq, kv, wo)


def _ffn_kernel(x_ref, fg_ref, wg_ref, wu_ref, wo_ref, o_ref, h_ref, *, tm, final_norm):
    @pl.when(pl.program_id(1) == 0)
    def _():
        _rms_rows(x_ref, None, h_ref, tm, copy_ref=o_ref)

    h = h_ref[...]
    gt = jnp.dot(h, wg_ref[...], preferred_element_type=F32)
    up = jnp.dot(h, wu_ref[...], preferred_element_type=F32)
    a = (jax.nn.silu(gt) * up).astype(BF16)
    o_ref[...] += jnp.dot(a, wo_ref[...], preferred_element_type=F32)

    if final_norm:
        @pl.when(pl.program_id(1) == pl.num_programs(1) - 1)
        def _():
            _rms_rows(o_ref, fg_ref, o_ref, tm)


def _ffn(x, final_g, w_in, w_out, layer, *, tm, tf, final_norm):
    n, d = x.shape
    nf = _exact_div(w_out.shape[1], tf)
    return pl.pallas_call(
        functools.partial(_ffn_kernel, tm=tm, final_norm=final_norm),
        grid=(_exact_div(n, tm), nf),
        in_specs=[pl.BlockSpec((tm, d), lambda i, f: (i, 0)),
                  pl.BlockSpec((1, d), lambda i, f: (0, 0)),
                  pl.BlockSpec((None, d, tf), lambda i, f: (layer, 0, f)),
                  pl.BlockSpec((None, d, tf), lambda i, f: (layer, 0, nf + f)),
                  pl.BlockSpec((None, tf, d), lambda i, f: (layer, f, 0))],
        out_specs=pl.BlockSpec((tm, d), lambda i, f: (i, 0)),
        out_shape=jax.ShapeDtypeStruct((n, d), F32),
        scratch_shapes=[pltpu.VMEM((tm, d), BF16)],
        compiler_params=_params(("parallel", "arbitrary")),
        name="ffn",
    )(x, final_g, w_in, w_in, w_out)


def kernel(x, mem, norm_mix, w_in, w_gate, sinks, sgu_ln_g, sgu_ln_b, sgu_w, sgu_b, hgrn_lb, hgrn_norm,
           w_br_a, w_br_b, w_br_c, w_out, norm_x, mem_norm, w_xq, w_xkv, w_xo, norm_ffn, w_ffn_in,
           w_ffn_out, final_norm):
    batch, seq, d = x.shape
    depth = w_in.shape[0]
    mem_len = mem.shape[1]
    n = batch * seq

    q_end = ATTN_Q_HEADS * ATTN_HEAD_DIM
    kv_end = q_end + 2 * ATTN_KV_HEADS * ATTN_HEAD_DIM
    g_mix = norm_mix[:, :, None]
    w_all = jnp.concatenate(
        [(w * g_mix).astype(BF16)
         for w in (w_gate, w_in[..., :q_end], w_in[..., kv_end:], w_in[..., q_end:kv_end])], axis=-1)
    assert w_all.shape[-1] == COL_END
    wa, wb, wc, wo = (w.astype(BF16) for w in (w_br_a, w_br_b, w_br_c, w_out))
    wxq, wxo = (w_xq * norm_x[:, :, None]).astype(BF16), w_xo.astype(BF16)
    wxkv = jnp.concatenate([w_xkv[l] * mem_norm[:, None] for l in range(depth)], axis=-1).astype(BF16)
    wfi, wfo = (w_ffn_in * norm_ffn[:, :, None]).astype(BF16), w_ffn_out.astype(BF16)
    sgu_bt = jnp.swapaxes(sgu_b, 1, 2)
    sm = jax.nn.softmax(hgrn_lb.astype(F32), axis=0)
    lb_all = jnp.cumsum(sm, axis=0) - sm[0:1]
    sgu_ln_g, sgu_ln_b, hgrn_norm = (p[:, None, :] for p in (sgu_ln_g, sgu_ln_b, hgrn_norm))

    xf = x.reshape(n, d)
    kv_all = _norm_matmul(mem.reshape(batch * mem_len, d), wxkv, 0, tm=1024, tn=512)
    for l in range(depth):
        proj = _norm_matmul(xf, w_all, l, tm=1024, tn=1536, activate=True)
        ya = _swa(proj, sinks[l], batch=batch, seq=seq, tq=512)
        yb = _sgu(proj, sgu_ln_g, sgu_ln_b, sgu_w, sgu_bt, l, ts=1024)
        yc = _hgrn(proj, lb_all[l:l + 1], hgrn_norm, l, batch=batch, seq=seq, tt=seq, hb=8)
        xf = _merge(xf, ya, yb, yc, proj, wa, wb, wc, wo, l, tm=256, tk=512)
        xf = _xattn(xf, wxq, kv_all, wxo, l, seq=seq, mem_len=mem_len, tm=1024)
        xf = _ffn(xf, final_norm.reshape(1, d), wfi, wfo, l, tm=1024, tf=512,
                  final_norm=(l == depth - 1))
    return xf.reshape(batch, seq, d)
```

```python
import functools

import jax
import jax.numpy as jnp
from jax import lax
from jax.experimental import pallas as pl
from jax.experimental.pallas import tpu as pltpu

F32 = jnp.float32
BF16 = jnp.bfloat16
EPS = 1e-6

LANES = 128
V7X_VMEM_LIMIT = 56 * 1024 * 1024

ATTN_HEAD_DIM = 64
ATTN_Q_HEADS = 16
ATTN_KV_HEADS = 4
WINDOW = 128
SGU_CHUNK = 128
SGU_GROUPS = 8
HGRN_HEADS = 8
HGRN_HEAD_DIM = 128
HGRN_CHUNK = 64
HGRN_SUB = 16
X_HEADS = 4
X_HEAD_DIM = 128

COL_GATE, COL_QA, COL_UB, COL_VB, COL_FC, COL_IC, COL_QC, COL_GC, COL_KV, COL_END = (
    0, 6144, 7168, 8192, 9216, 10240, 11264, 12288, 13312, 13824)
ACT_SUB = 512


def _exact_div(a, b):
    assert a % b == 0, (a, b)
    return a // b


def _params(sem):
    return pltpu.CompilerParams(dimension_semantics=sem, vmem_limit_bytes=V7X_VMEM_LIMIT)


def _rms_rows(x_ref, g_ref, h_ref, rows, copy_ref=None):
    slab = 64
    g = None if g_ref is None else g_ref[...]

    def body(i, c):
        r = pl.ds(pl.multiple_of(i * slab, slab), slab)
        x = x_ref[r, :]
        if copy_ref is not None:
            copy_ref[r, :] = x
        ms = jnp.mean(x * x, axis=-1, keepdims=True)
        y = x * lax.rsqrt(ms + EPS)
        h_ref[r, :] = (y if g is None else y * g).astype(h_ref.dtype)
        return c

    lax.fori_loop(0, _exact_div(rows, slab), body, 0, unroll=4)


def _in_range(c0, lo_hi):
    _exact_div(lo_hi[0], ACT_SUB), _exact_div(lo_hi[1], ACT_SUB)
    return (c0 >= lo_hi[0]) & (c0 < lo_hi[1])


def _projection_activation(x, c0, gate_tile):
    if gate_tile:
        return 0.5 + 0.5 * jnp.tanh(0.5 * x)
    return x * jnp.where(_in_range(c0, (COL_QA, COL_UB)), ATTN_HEAD_DIM ** -0.5, 1.0)


def _norm_matmul_kernel(x_ref, w_ref, o_ref, h_ref, *, activate, tm, tn):
    j = pl.program_id(1)

    @pl.when(j == 0)
    def _():
        _rms_rows(x_ref, None, h_ref, tm)

    if not activate:
        o_ref[...] = jnp.dot(h_ref[...], w_ref[...], preferred_element_type=F32).astype(o_ref.dtype)
    else:
        assert COL_GATE == 0
        n_gate_tiles = _exact_div(COL_QA, tn)

        def step(gate_tile):
            h = h_ref[...]
            for s in range(_exact_div(tn, ACT_SUB)):
                cs = slice(s * ACT_SUB, (s + 1) * ACT_SUB)
                acc = jnp.dot(h, w_ref[:, cs], preferred_element_type=F32)
                o_ref[:, cs] = _projection_activation(acc, j * tn + s * ACT_SUB, gate_tile).astype(o_ref.dtype)

        pl.when(j < n_gate_tiles)(functools.partial(step, True))
        pl.when(j >= n_gate_tiles)(functools.partial(step, False))


def _norm_matmul(x, w, layer, *, tm, tn, activate=False):
    n, d = x.shape
    nout = w.shape[-1]
    kern = functools.partial(_norm_matmul_kernel, activate=activate, tm=tm, tn=tn)
    if w.ndim == 3:
        w_spec = pl.BlockSpec((None, d, tn), lambda i, j: (layer, 0, j))
    else:
        w_spec = pl.BlockSpec((d, tn), lambda i, j: (0, j))
    return pl.pallas_call(
        kern,
        grid=(_exact_div(n, tm), _exact_div(nout, tn)),
        in_specs=[pl.BlockSpec((tm, d), lambda i, j: (i, 0)),
                  w_spec],
        out_specs=pl.BlockSpec((tm, tn), lambda i, j: (i, j)),
        out_shape=jax.ShapeDtypeStruct((n, nout), BF16),
        scratch_shapes=[pltpu.VMEM((tm, d), BF16)],
        compiler_params=_params(("parallel", "arbitrary")),
        name="norm_matmul",
    )(x, w)


def _swap_halves(t):
    return jnp.concatenate([t[:, 64:], t[:, :64]], axis=1)


def _swa_kernel(sink_ref, q_ref, kvc_ref, kvp_ref, o_ref, kv_buf, *, tq):
    t_idx = pl.program_id(1)
    kv_buf[0:WINDOW, :] = kvp_ref[...]
    kv_buf[WINDOW:, :] = kvc_ref[...]

    row = lax.broadcasted_iota(jnp.int32, (WINDOW, 2 * WINDOW), 0)
    col = lax.broadcasted_iota(jnp.int32, (WINDOW, 2 * WINDOW), 1)
    band = (col > row) & (col <= row + WINDOW)
    lane = lax.broadcasted_iota(jnp.int32, (2 * WINDOW, LANES), 1)
    lo = lane < 64
    out_lane = lax.broadcasted_iota(jnp.int32, (WINDOW, LANES), 1)

    def blk(i, c):
        r0 = pl.multiple_of(i * WINDOW, WINDOW)
        cmin = jnp.where((t_idx == 0) & (i == 0), WINDOW, 0)
        valid = band & (col >= cmin)
        kcats, vcats = [], []
        for m in range(2):
            kt = kv_buf[pl.ds(r0, 2 * WINDOW), m * LANES:(m + 1) * LANES]
            vt = kv_buf[pl.ds(r0, 2 * WINDOW), 256 + m * LANES:256 + (m + 1) * LANES]
            ks, vs = _swap_halves(kt), _swap_halves(vt)
            zero = jnp.zeros_like(kt)
            for e in range(2):
                k_lo = jnp.where(lo, kt if e == 0 else ks, zero)
                k_hi = jnp.where(lo, zero, ks if e == 0 else kt)
                v_lo = jnp.where(lo, vt if e == 0 else vs, zero)
                v_hi = jnp.where(lo, zero, vs if e == 0 else vt)
                kcats.append(jnp.concatenate([k_lo, k_hi], axis=0))
                vcats.append(jnp.concatenate([v_lo, v_hi], axis=0))

        def score(j):
            qt = jnp.concatenate([q_ref[pl.ds(r0, WINDOW), p * LANES:(p + 1) * LANES]
                                  for p in (2 * j, 2 * j + 1)], axis=0)
            return lax.dot_general(qt, kcats[j], (((1,), (1,)), ((), ())),
                                   preferred_element_type=F32)

        def attend(j, s):
            pcats, scales = [], []
            for t in range(2):
                p = 2 * j + t
                probs, invs = [], []
                for hh in range(2):
                    sink = sink_ref[2 * p + hh]
                    sh = jnp.where(valid, s[t * WINDOW:(t + 1) * WINDOW, hh * 256:(hh + 1) * 256], -jnp.inf)
                    mx = jnp.maximum(jnp.max(sh, axis=-1, keepdims=True), sink)
                    pe = jnp.exp(sh - mx)
                    den = jnp.sum(pe, axis=-1, keepdims=True) + jnp.exp(sink - mx)
                    probs.append(pe.astype(BF16))
                    invs.append(1.0 / den)
                pcats.append(jnp.concatenate(probs, axis=1))
                scales.append(jnp.where(out_lane < 64, invs[0], invs[1]))
            o = jnp.dot(jnp.concatenate(pcats, axis=0), vcats[j], preferred_element_type=F32)
            for t in range(2):
                p = 2 * j + t
                o_ref[pl.ds(r0, WINDOW), p * LANES:(p + 1) * LANES] = (
                    o[t * WINDOW:(t + 1) * WINDOW] * scales[t]).astype(o_ref.dtype)

        s_next = score(0)
        for j in range(ATTN_KV_HEADS):
            s_cur = s_next
            if j + 1 < ATTN_KV_HEADS:
                s_next = score(j + 1)
            attend(j, s_cur)
        return c

    lax.fori_loop(0, _exact_div(tq, WINDOW), blk, 0)


def _swa(proj, sinks_l, *, batch, seq, tq):
    n = proj.shape[0]
    nt = _exact_div(seq, tq)
    qw = ATTN_Q_HEADS * ATTN_HEAD_DIM
    kvw = 2 * ATTN_KV_HEADS * ATTN_HEAD_DIM
    kv_blk = _exact_div(COL_KV, kvw)
    bpt = _exact_div(tq, WINDOW)

    def prev_map(b, t):
        return (jnp.maximum(b * _exact_div(seq, WINDOW) + t * bpt - 1, 0), kv_blk)

    return pl.pallas_call(
        functools.partial(_swa_kernel, tq=tq),
        grid=(batch, nt),
        in_specs=[pl.BlockSpec(memory_space=pltpu.SMEM),
                  pl.BlockSpec((tq, qw), lambda b, t: (b * nt + t, _exact_div(COL_QA, qw))),
                  pl.BlockSpec((tq, kvw), lambda b, t: (b * nt + t, kv_blk)),
                  pl.BlockSpec((WINDOW, kvw), prev_map)],
        out_specs=pl.BlockSpec((tq, qw), lambda b, t: (b * nt + t, 0)),
        out_shape=jax.ShapeDtypeStruct((n, qw), BF16),
        scratch_shapes=[pltpu.VMEM((tq + WINDOW, kvw), BF16)],
        compiler_params=_params(("parallel", "parallel")),
        name="swa",
    )(sinks_l, proj, proj, proj)


def _sgu_kernel(u_ref, v_ref, lng_ref, lnb_ref, w_ref, bt_ref, o_ref, *, ts):
    row = lax.broadcasted_iota(jnp.int32, (SGU_CHUNK, SGU_CHUNK), 0)
    col = lax.broadcasted_iota(jnp.int32, (SGU_CHUNK, SGU_CHUNK), 1)
    tril = row >= col
    lng = lng_ref[...]
    lnb = lnb_ref[...]
    w_tril = [jnp.where(tril, w_ref[g], 0.0).astype(BF16) for g in range(SGU_GROUPS)]

    def chunk(c, carry):
        r = pl.ds(pl.multiple_of(c * SGU_CHUNK, SGU_CHUNK), SGU_CHUNK)
        v = jax.nn.gelu(v_ref[r, :].astype(F32))
        mu = jnp.mean(v, axis=-1, keepdims=True)
        vc = v - mu
        var = jnp.mean(vc * vc, axis=-1, keepdims=True)
        vn = (vc * lax.rsqrt(var + EPS) * lng + lnb).astype(BF16)
        for g in range(SGU_GROUPS):
            cs = slice(g * LANES, (g + 1) * LANES)
            mixed = jnp.dot(w_tril[g], vn[:, cs], preferred_element_type=F32) + bt_ref[:, g:g + 1]
            u = jax.nn.gelu(u_ref[r, cs].astype(F32))
            o_ref[r, cs] = (u * mixed).astype(o_ref.dtype)
        return carry

    lax.fori_loop(0, _exact_div(ts, SGU_CHUNK), chunk, 0)


def _sgu(proj, ln_g, ln_b, w, bt, layer, *, ts):
    n = proj.shape[0]
    width = SGU_GROUPS * LANES
    return pl.pallas_call(
        functools.partial(_sgu_kernel, ts=ts),
        grid=(_exact_div(n, ts),),
        in_specs=[pl.BlockSpec((ts, width), lambda i: (i, _exact_div(COL_UB, width))),
                  pl.BlockSpec((ts, width), lambda i: (i, _exact_div(COL_VB, width))),
                  pl.BlockSpec((None, 1, width), lambda i: (layer, 0, 0)),
                  pl.BlockSpec((None, 1, width), lambda i: (layer, 0, 0)),
                  pl.BlockSpec((None, SGU_GROUPS, SGU_CHUNK, SGU_CHUNK), lambda i: (layer, 0, 0, 0)),
                  pl.BlockSpec((None, SGU_CHUNK, SGU_GROUPS), lambda i: (layer, 0, 0))],
        out_specs=pl.BlockSpec((ts, width), lambda i: (i, 0)),
        out_shape=jax.ShapeDtypeStruct((n, width), BF16),
        compiler_params=_params(("parallel",)),
        name="sgu",
    )(proj, proj, ln_g, ln_b, w, bt)


def _hgrn_kernel(f_ref, i_ref, q_ref, g_ref, lb_ref, ng_ref, o_ref, state_ref, kqb_ref, a_ref, *, tt, hb):
    C, SUB = HGRN_CHUNK, HGRN_SUB

    @pl.when(pl.program_id(2) == 0)
    def _():
        state_ref[...] = jnp.zeros_like(state_ref)

    ng = ng_ref[...]
    r64 = lax.broadcasted_iota(jnp.int32, (C, C), 0)
    c64 = lax.broadcasted_iota(jnp.int32, (C, C), 1)
    tril = jnp.where(r64 >= c64, 1.0, 0.0).astype(BF16)
    t_i = lax.broadcasted_iota(jnp.int32, (C, LANES), 0)
    l_i = lax.broadcasted_iota(jnp.int32, (C, LANES), 1)
    own_half = ((l_i // C) == ((t_i // SUB) % 2)) & ((l_i % C) <= t_i)
    valid0 = own_half & (t_i < 2 * SUB)
    valid1 = own_half & (t_i >= 2 * SUB)

    def gates(r, h):
        cs = slice(h * LANES, (h + 1) * LANES)
        lb = lb_ref[:, cs]
        half_span = 0.5 * (1.0 - lb)
        f = (lb + half_span) + half_span * jnp.tanh(0.5 * f_ref[r, cs].astype(F32))
        lf = jnp.log2(f)
        k = 1.0 - f
        x_q = q_ref[r, cs].astype(F32)
        qf = x_q * ((0.5 * HGRN_HEAD_DIM ** -0.5) + (0.5 * HGRN_HEAD_DIM ** -0.5) * jnp.tanh(0.5 * x_q))
        hi = lf.astype(BF16)
        lo = (lf - hi.astype(F32)).astype(BF16)
        b2 = jnp.dot(tril, jnp.concatenate([hi, lo], axis=1), preferred_element_type=F32)
        b = b2[:, :LANES] + b2[:, LANES:]
        return k, qf, b

    def scores(k, qf, b):
        refs = [jnp.zeros((1, LANES), F32)] + [b[SUB * i - 1:SUB * i, :] for i in range(1, _exact_div(C, SUB))]
        q_rel = jnp.concatenate(
            [(qf[SUB * i:SUB * (i + 1)] * jnp.exp2(b[SUB * i:SUB * (i + 1)] - rf)).astype(BF16)
             for i, rf in enumerate(refs)], axis=0)
        blocks = []
        for i, rf in enumerate(refs):
            rows = SUB * (i + 1)
            blocks.append((k[:rows] * jnp.exp2(rf - b[:rows])).astype(BF16))
            if rows < C:
                blocks.append(jnp.zeros((C - rows, LANES), BF16))
        kcat = jnp.concatenate(blocks, axis=0)
        sc = lax.dot_general(q_rel, kcat, (((1,), (1,)), ((), ())), preferred_element_type=F32)
        return jnp.where(valid0, sc[:, :LANES], jnp.where(valid1, sc[:, LANES:], 0.0))

    def outputs(r, h, k, qf, b, a):
        cs = slice(h * LANES, (h + 1) * LANES)
        v = i_ref[r, cs]
        b_last = b[C - 1:C, :]
        q_abs = (qf * jnp.exp2(b)).astype(BF16)
        vv = jnp.concatenate([v, v], axis=0)
        st = state_ref[h]
        o = (lax.dot_general(q_abs, st.astype(BF16), (((1,), (1,)), ((), ())), preferred_element_type=F32)
             + jnp.dot(a.astype(BF16), vv, preferred_element_type=F32))
        k_end = (k * jnp.exp2(b_last - b)).astype(BF16)
        state_ref[h] = st * jnp.exp2(b_last) + lax.dot_general(
            v, k_end, (((0,), (0,)), ((), ())), preferred_element_type=F32)
        on = o * lax.rsqrt(jnp.mean(o * o, axis=-1, keepdims=True) + EPS) * ng
        x_g = g_ref[r, cs].astype(F32)
        o_ref[r, cs] = (on * (x_g * (0.5 + 0.5 * jnp.tanh(0.5 * x_g)))).astype(o_ref.dtype)

    def rows(c):
        return pl.ds(pl.multiple_of(c * C, C), C)

    def stage_scores(r):
        kqb = [gates(r, h) for h in range(hb)]
        a = [scores(*t) for t in kqb]
        for h in range(hb):
            for j, val in enumerate(kqb[h]):
                kqb_ref[j, h] = val
            a_ref[h] = a[h].astype(BF16)

    n_chunks = _exact_div(tt, C)
    stage_scores(rows(0))

    def chunk(c, carry):
        staged = [(kqb_ref[0, h], kqb_ref[1, h], kqb_ref[2, h], a_ref[h]) for h in range(hb)]
        for h in range(hb):
            outputs(rows(c), h, *staged[h])
        stage_scores(rows(jnp.minimum(c + 1, n_chunks - 1)))
        return carry

    lax.fori_loop(0, n_chunks, chunk, 0)


def _hgrn(proj, lb_l, norm_g, layer, *, batch, seq, tt, hb):
    n = proj.shape[0]
    nt = _exact_div(seq, tt)
    width = HGRN_HEADS * HGRN_HEAD_DIM
    bw = hb * LANES

    def col(off):
        return lambda b, h, t: (b * nt + t, _exact_div(off, bw) + h)

    return pl.pallas_call(
        functools.partial(_hgrn_kernel, tt=tt, hb=hb),
        grid=(batch, _exact_div(HGRN_HEADS, hb), nt),
        in_specs=[pl.BlockSpec((tt, bw), col(COL_FC)),
                  pl.BlockSpec((tt, bw), col(COL_IC)),
                  pl.BlockSpec((tt, bw), col(COL_QC)),
                  pl.BlockSpec((tt, bw), col(COL_GC)),
                  pl.BlockSpec((1, bw), lambda b, h, t: (0, h)),
                  pl.BlockSpec((None, 1, LANES), lambda b, h, t: (layer, 0, 0))],
        out_specs=pl.BlockSpec((tt, bw), lambda b, h, t: (b * nt + t, h)),
        out_shape=jax.ShapeDtypeStruct((n, width), BF16),
        scratch_shapes=[pltpu.VMEM((hb, HGRN_HEAD_DIM, HGRN_HEAD_DIM), F32),
                        pltpu.VMEM((3, hb, HGRN_CHUNK, LANES), F32),
                        pltpu.VMEM((hb, HGRN_CHUNK, LANES), BF16)],
        compiler_params=_params(("parallel", "parallel", "arbitrary")),
        name="hgrn2",
    )(proj, proj, proj, proj, lb_l, norm_g)


def _merge_kernel(x_ref, ya_ref, yb_ref, yc_ref, ga_ref, gb_ref, gc_ref,
                  wa_ref, wb_ref, wc_ref, wo_ref, o_ref, *, tk):
    d = o_ref.shape[1]
    ya, yb, yc = ya_ref[...], yb_ref[...], yc_ref[...]

    def merged(c):
        cs = slice(c * tk, (c + 1) * tk)
        m = (ga_ref[:, cs].astype(F32) * jnp.dot(ya, wa_ref[:, cs], preferred_element_type=F32)
             + gb_ref[:, cs].astype(F32) * jnp.dot(yb, wb_ref[:, cs], preferred_element_type=F32)
             + gc_ref[:, cs].astype(F32) * jnp.dot(yc, wc_ref[:, cs], preferred_element_type=F32))
        return m.astype(BF16)

    n_groups = _exact_div(d, tk)
    acc = x_ref[...]
    m_next = merged(0)
    for c in range(n_groups):
        m_cur = m_next
        if c + 1 < n_groups:
            m_next = merged(c + 1)
        acc = acc + jnp.dot(m_cur, wo_ref[c * tk:(c + 1) * tk, :], preferred_element_type=F32)
    o_ref[...] = acc


def _merge(x, ya, yb, yc, proj, wa, wb, wc, wo, layer, *, tm, tk):
    n, d = x.shape
    wbr = ya.shape[1]
    g0 = _exact_div(COL_GATE, d)

    def gate(br):
        return pl.BlockSpec((tm, d), lambda i: (i, g0 + br))

    def resident(rows):
        return pl.BlockSpec((None, rows, d), lambda i: (layer, 0, 0), pipeline_mode=pl.Buffered(1))

    y_spec = pl.BlockSpec((tm, wbr), lambda i: (i, 0))
    return pl.pallas_call(
        functools.partial(_merge_kernel, tk=tk),
        grid=(_exact_div(n, tm),),
        in_specs=[pl.BlockSpec((tm, d), lambda i: (i, 0)),
                  y_spec, y_spec, y_spec, gate(0), gate(1), gate(2),
                  resident(wbr), resident(wbr), resident(wbr), resident(d)],
        out_specs=pl.BlockSpec((tm, d), lambda i: (i, 0)),
        out_shape=jax.ShapeDtypeStruct((n, d), F32),
        compiler_params=_params(("parallel",)),
        name="merge",
    )(x, ya, yb, yc, proj, proj, proj, wa, wb, wc, wo)


def _xattn_kernel(x_ref, wq_ref, kv_ref, wo_ref, o_ref, h_ref, *, tm):
    _rms_rows(x_ref, None, h_ref, tm)
    q = jnp.dot(h_ref[...], wq_ref[...], preferred_element_type=F32).astype(BF16)
    xw = X_HEADS * X_HEAD_DIM
    scale = X_HEAD_DIM ** -0.5

    def score(hd):
        cs = slice(hd * X_HEAD_DIM, (hd + 1) * X_HEAD_DIM)
        return lax.dot_general(q[:, cs], kv_ref[:, cs], (((1,), (1,)), ((), ())),
                               preferred_element_type=F32) * scale

    def attend(hd, s):
        vh = kv_ref[:, xw + hd * X_HEAD_DIM:xw + (hd + 1) * X_HEAD_DIM]
        mx = jnp.max(s, axis=-1, keepdims=True)
        pe = jnp.exp(s - mx)
        inv = 1.0 / jnp.sum(pe, axis=-1, keepdims=True)
        return (jnp.dot(pe.astype(BF16), vh, preferred_element_type=F32) * inv).astype(BF16)

    outs = []
    s_next = score(0)
    for hd in range(X_HEADS):
        s_cur = s_next
        if hd + 1 < X_HEADS:
            s_next = score(hd + 1)
        outs.append(attend(hd, s_cur))
    oc = jnp.concatenate(outs, axis=1)
    o_ref[...] = x_ref[...] + jnp.dot(oc, wo_ref[...], preferred_element_type=F32)


def _xattn(x, wq, kv, wo, layer, *, seq, mem_len, tm):
    n, d = x.shape
    xw = X_HEADS * X_HEAD_DIM
    tiles_per_seq = _exact_div(seq, tm)
    return pl.pallas_call(
        functools.partial(_xattn_kernel, tm=tm),
        grid=(_exact_div(n, tm),),
        in_specs=[pl.BlockSpec((tm, d), lambda i: (i, 0)),
                  pl.BlockSpec((None, d, xw), lambda i: (layer, 0, 0)),
                  pl.BlockSpec((mem_len, 2 * xw), lambda i: (i // tiles_per_seq, layer)),
                  pl.BlockSpec((None, xw, d), lambda i: (layer, 0, 0))],
        out_specs=pl.BlockSpec((tm, d), lambda i: (i, 0)),
        out_shape=jax.ShapeDtypeStruct((n, d), F32),
        scratch_shapes=[pltpu.VMEM((tm, d), BF16)],
        compiler_params=_params(("parallel",)),
        name="xattn",
    )(x, wq, kv, wo)


def _ffn_kernel(x_ref, fg_ref, wg_ref, wu_ref, wo_ref, o_ref, h_ref, *, tm, final_norm):
    def step(first):
        h = h_ref[...]
        gt = jnp.dot(h, wg_ref[...], preferred_element_type=F32)
        up = jnp.dot(h, wu_ref[...], preferred_element_type=F32)
        a = (jax.nn.silu(gt) * up).astype(BF16)
        down = jnp.dot(a, wo_ref[...], preferred_element_type=F32)
        o_ref[...] = (x_ref[...] if first else o_ref[...]) + down

    @pl.when(pl.program_id(1) == 0)
    def _():
        _rms_rows(x_ref, None, h_ref, tm)
        step(True)

    pl.when(pl.program_id(1) != 0)(functools.partial(step, False))

    if final_norm:
        @pl.when(pl.program_id(1) == pl.num_programs(1) - 1)
        def _():
            _rms_rows(o_ref, fg_ref, o_ref, tm)


def _ffn(x, final_g, w_in, w_out, layer, *, tm, tf, final_norm):
    n, d = x.shape
    nf = _exact_div(w_out.shape[1], tf)
    return pl.pallas_call(
        functools.partial(_ffn_kernel, tm=tm, final_norm=final_norm),
        grid=(_exact_div(n, tm), nf),
        in_specs=[pl.BlockSpec((tm, d), lambda i, f: (i, 0)),
                  pl.BlockSpec((1, d), lambda i, f: (0, 0)),
                  pl.BlockSpec((None, d, tf), lambda i, f: (layer, 0, f)),
                  pl.BlockSpec((None, d, tf), lambda i, f: (layer, 0, nf + f)),
                  pl.BlockSpec((None, tf, d), lambda i, f: (layer, f, 0))],
        out_specs=pl.BlockSpec((tm, d), lambda i, f: (i, 0)),
        out_shape=jax.ShapeDtypeStruct((n, d), F32),
        scratch_shapes=[pltpu.VMEM((tm, d), BF16)],
        compiler_params=_params(("parallel", "arbitrary")),
        name="ffn",
    )(x, final_g, w_in, w_in, w_out)


def kernel(x, mem, norm_mix, w_in, w_gate, sinks, sgu_ln_g, sgu_ln_b, sgu_w, sgu_b, hgrn_lb, hgrn_norm,
           w_br_a, w_br_b, w_br_c, w_out, norm_x, mem_norm, w_xq, w_xkv, w_xo, norm_ffn, w_ffn_in,
           w_ffn_out, final_norm):
    batch, seq, d = x.shape
    depth = w_in.shape[0]
    mem_len = mem.shape[1]
    n = batch * seq

    q_end = ATTN_Q_HEADS * ATTN_HEAD_DIM
    kv_end = q_end + 2 * ATTN_KV_HEADS * ATTN_HEAD_DIM
    g_mix = norm_mix[:, :, None]
    w_all = jnp.concatenate(
        [(w * g_mix).astype(BF16)
         for w in (w_gate, w_in[..., :q_end], w_in[..., kv_end:], w_in[..., q_end:kv_end])], axis=-1)
    assert w_all.shape[-1] == COL_END
    wa, wb, wc, wo = (w.astype(BF16) for w in (w_br_a, w_br_b, w_br_c, w_out))
    wxq, wxo = (w_xq * norm_x[:, :, None]).astype(BF16), w_xo.astype(BF16)
    wxkv = jnp.concatenate([w_xkv[l] * mem_norm[:, None] for l in range(depth)], axis=-1).astype(BF16)
    wfi, wfo = (w_ffn_in * norm_ffn[:, :, None]).astype(BF16), w_ffn_out.astype(BF16)
    sgu_bt = jnp.swapaxes(sgu_b, 1, 2)
    sm = jax.nn.softmax(hgrn_lb.astype(F32), axis=0)
    lb_all = jnp.cumsum(sm, axis=0) - sm[0:1]
    sgu_ln_g, sgu_ln_b, hgrn_norm = (p[:, None, :] for p in (sgu_ln_g, sgu_ln_b, hgrn_norm))

    xf = x.reshape(n, d)
    kv_all = _norm_matmul(mem.reshape(batch * mem_len, d), wxkv, 0, tm=1024, tn=512)
    for l in range(depth):
        proj = _norm_matmul(xf, w_all, l, tm=1024, tn=1536, activate=True)
        ya = _swa(proj, sinks[l], batch=batch, seq=seq, tq=512)
        yb = _sgu(proj, sgu_ln_g, sgu_ln_b, sgu_w, sgu_bt, l, ts=1024)
        yc = _hgrn(proj, lb_all[l:l + 1], hgrn_norm, l, batch=batch, seq=seq, tt=seq, hb=8)
        xf = _merge(xf, ya, yb, yc, proj, wa, wb, wc, wo, l, tm=256, tk=512)
        xf = _xattn(xf, wxq, kv_all, wxo, l, seq=seq, mem_len=mem_len, tm=1024)
        xf = _ffn(xf, final_norm.reshape(1, d), wfi, wfo, l, tm=1024, tf=512,
                  final_norm=(l == depth - 1))
    return xf.reshape(batch, seq, d)
```

```python
import functools

import jax
import jax.numpy as jnp
from jax import lax
from jax.experimental import pallas as pl
from jax.experimental.pallas import tpu as pltpu

F32 = jnp.float32
BF16 = jnp.bfloat16
EPS = 1e-6

LANES = 128
V7X_VMEM_LIMIT = 56 * 1024 * 1024

ATTN_HEAD_DIM = 64
ATTN_Q_HEADS = 16
ATTN_KV_HEADS = 4
WINDOW = 128
SGU_CHUNK = 128
SGU_GROUPS = 8
HGRN_HEADS = 8
HGRN_HEAD_DIM = 128
HGRN_CHUNK = 64
HGRN_SUB = 16
X_HEADS = 4
X_HEAD_DIM = 128

COL_GATE, COL_UB, COL_VB, COL_FC, COL_IC, COL_QC, COL_GC, COL_QA, COL_KV, COL_END = (
    0, 6144, 7168, 8192, 9216, 10240, 11264, 12288, 13312, 13824)
ACT_SUB = 512


def _exact_div(a, b):
    assert a % b == 0, (a, b)
    return a // b


def _params(sem):
    return pltpu.CompilerParams(dimension_semantics=sem, vmem_limit_bytes=V7X_VMEM_LIMIT)


def _rms_rows(x_ref, g_ref, h_ref, rows, copy_ref=None):
    slab = 64
    g = None if g_ref is None else g_ref[...]

    def body(i, c):
        r = pl.ds(pl.multiple_of(i * slab, slab), slab)
        x = x_ref[r, :]
        if copy_ref is not None:
            copy_ref[r, :] = x
        ms = jnp.mean(x * x, axis=-1, keepdims=True)
        y = x * lax.rsqrt(ms + EPS)
        h_ref[r, :] = (y if g is None else y * g).astype(h_ref.dtype)
        return c

    lax.fori_loop(0, _exact_div(rows, slab), body, 0, unroll=4)


def _in_range(c0, lo_hi):
    _exact_div(lo_hi[0], ACT_SUB), _exact_div(lo_hi[1], ACT_SUB)
    return (c0 >= lo_hi[0]) & (c0 < lo_hi[1])


def _projection_activation(x, c0, gate_tile):
    if gate_tile:
        return 0.5 + 0.5 * jnp.tanh(0.5 * x)
    return x * jnp.where(_in_range(c0, (COL_QA, COL_KV)), ATTN_HEAD_DIM ** -0.5, 1.0)


def _norm_matmul_kernel(x_ref, w_ref, o_ref, h_ref, *, activate, tm, tn):
    j = pl.program_id(1)

    @pl.when(j == 0)
    def _():
        _rms_rows(x_ref, None, h_ref, tm)

    if not activate:
        o_ref[...] = jnp.dot(h_ref[...], w_ref[...], preferred_element_type=F32).astype(o_ref.dtype)
    else:
        assert COL_GATE == 0
        n_gate_tiles = _exact_div(COL_UB, tn)

        def step(gate_tile):
            h = h_ref[...]
            for s in range(_exact_div(tn, ACT_SUB)):
                cs = slice(s * ACT_SUB, (s + 1) * ACT_SUB)
                acc = jnp.dot(h, w_ref[:, cs], preferred_element_type=F32)
                o_ref[:, cs] = _projection_activation(acc, j * tn + s * ACT_SUB, gate_tile).astype(o_ref.dtype)

        pl.when(j < n_gate_tiles)(functools.partial(step, True))
        pl.when(j >= n_gate_tiles)(functools.partial(step, False))


def _norm_matmul(x, w, layer, *, tm, tn, activate=False):
    n, d = x.shape
    nout = w.shape[-1]
    kern = functools.partial(_norm_matmul_kernel, activate=activate, tm=tm, tn=tn)
    if w.ndim == 3:
        w_spec = pl.BlockSpec((None, d, tn), lambda i, j: (layer, 0, j))
    else:
        w_spec = pl.BlockSpec((d, tn), lambda i, j: (0, j))
    return pl.pallas_call(
        kern,
        grid=(_exact_div(n, tm), _exact_div(nout, tn)),
        in_specs=[pl.BlockSpec((tm, d), lambda i, j: (i, 0)),
                  w_spec],
        out_specs=pl.BlockSpec((tm, tn), lambda i, j: (i, j)),
        out_shape=jax.ShapeDtypeStruct((n, nout), BF16),
        scratch_shapes=[pltpu.VMEM((tm, d), BF16)],
        compiler_params=_params(("parallel", "arbitrary")),
        name="norm_matmul",
    )(x, w)


def _swap_halves(t):
    return jnp.concatenate([t[:, 64:], t[:, :64]], axis=1)


def _swa_kernel(sink_ref, q_ref, kvc_ref, kvp_ref, o_ref, kv_buf, *, tq):
    t_idx = pl.program_id(1)
    kv_buf[0:WINDOW, :] = kvp_ref[...]
    kv_buf[WINDOW:, :] = kvc_ref[...]

    row = lax.broadcasted_iota(jnp.int32, (WINDOW, 2 * WINDOW), 0)
    col = lax.broadcasted_iota(jnp.int32, (WINDOW, 2 * WINDOW), 1)
    band = (col > row) & (col <= row + WINDOW)
    lane = lax.broadcasted_iota(jnp.int32, (2 * WINDOW, LANES), 1)
    lo = lane < 64
    out_lane = lax.broadcasted_iota(jnp.int32, (WINDOW, LANES), 1)

    def blk(i, c):
        r0 = pl.multiple_of(i * WINDOW, WINDOW)
        cmin = jnp.where((t_idx == 0) & (i == 0), WINDOW, 0)
        valid = band & (col >= cmin)
        kcats, vcats = [], []
        for m in range(2):
            kt = kv_buf[pl.ds(r0, 2 * WINDOW), m * LANES:(m + 1) * LANES]
            vt = kv_buf[pl.ds(r0, 2 * WINDOW), 256 + m * LANES:256 + (m + 1) * LANES]
            ks, vs = _swap_halves(kt), _swap_halves(vt)
            zero = jnp.zeros_like(kt)
            for e in range(2):
                k_lo = jnp.where(lo, kt if e == 0 else ks, zero)
                k_hi = jnp.where(lo, zero, ks if e == 0 else kt)
                v_lo = jnp.where(lo, vt if e == 0 else vs, zero)
                v_hi = jnp.where(lo, zero, vs if e == 0 else vt)
                kcats.append(jnp.concatenate([k_lo, k_hi], axis=0))
                vcats.append(jnp.concatenate([v_lo, v_hi], axis=0))

        def score(j):
            qt = jnp.concatenate([q_ref[pl.ds(r0, WINDOW), p * LANES:(p + 1) * LANES]
                                  for p in (2 * j, 2 * j + 1)], axis=0)
            return lax.dot_general(qt, kcats[j], (((1,), (1,)), ((), ())),
                                   preferred_element_type=F32)

        def attend(j, s):
            pcats, scales = [], []
            for t in range(2):
                p = 2 * j + t
                probs, invs = [], []
                for hh in range(2):
                    sink = sink_ref[2 * p + hh]
                    sh = jnp.where(valid, s[t * WINDOW:(t + 1) * WINDOW, hh * 256:(hh + 1) * 256], -jnp.inf)
                    mx = jnp.maximum(jnp.max(sh, axis=-1, keepdims=True), sink)
                    pe = jnp.exp(sh - mx)
                    den = jnp.sum(pe, axis=-1, keepdims=True) + jnp.exp(sink - mx)
                    probs.append(pe.astype(BF16))
                    invs.append(1.0 / den)
                pcats.append(jnp.concatenate(probs, axis=1))
                scales.append(jnp.where(out_lane < 64, invs[0], invs[1]))
            o = jnp.dot(jnp.concatenate(pcats, axis=0), vcats[j], preferred_element_type=F32)
            for t in range(2):
                p = 2 * j + t
                o_ref[pl.ds(r0, WINDOW), p * LANES:(p + 1) * LANES] = (
                    o[t * WINDOW:(t + 1) * WINDOW] * scales[t]).astype(o_ref.dtype)

        s_next = score(0)
        for j in range(ATTN_KV_HEADS):
            s_cur = s_next
            if j + 1 < ATTN_KV_HEADS:
                s_next = score(j + 1)
            attend(j, s_cur)
        return c

    lax.fori_loop(0, _exact_div(tq, WINDOW), blk, 0)


def _swa(proj, sinks_l, *, batch, seq, tq):
    n = proj.shape[0]
    nt = _exact_div(seq, tq)
    qw = ATTN_Q_HEADS * ATTN_HEAD_DIM
    kvw = 2 * ATTN_KV_HEADS * ATTN_HEAD_DIM
    kv_blk = _exact_div(COL_KV, kvw)
    bpt = _exact_div(tq, WINDOW)

    def prev_map(b, t):
        return (jnp.maximum(b * _exact_div(seq, WINDOW) + t * bpt - 1, 0), kv_blk)

    return pl.pallas_call(
        functools.partial(_swa_kernel, tq=tq),
        grid=(batch, nt),
        in_specs=[pl.BlockSpec(memory_space=pltpu.SMEM),
                  pl.BlockSpec((tq, qw), lambda b, t: (b * nt + t, _exact_div(COL_QA, qw))),
                  pl.BlockSpec((tq, kvw), lambda b, t: (b * nt + t, kv_blk)),
                  pl.BlockSpec((WINDOW, kvw), prev_map)],
        out_specs=pl.BlockSpec((tq, qw), lambda b, t: (b * nt + t, 0)),
        out_shape=jax.ShapeDtypeStruct((n, qw), BF16),
        scratch_shapes=[pltpu.VMEM((tq + WINDOW, kvw), BF16)],
        compiler_params=_params(("parallel", "parallel")),
        name="swa",
    )(sinks_l, proj, proj, proj)


def _sgu_kernel(uv_ref, lng_ref, lnb_ref, w_ref, bt_ref, o_ref, *, ts):
    width = SGU_GROUPS * LANES
    row = lax.broadcasted_iota(jnp.int32, (SGU_CHUNK, SGU_CHUNK), 0)
    col = lax.broadcasted_iota(jnp.int32, (SGU_CHUNK, SGU_CHUNK), 1)
    tril = row >= col
    lng = lng_ref[...]
    lnb = lnb_ref[...]
    w_tril = [jnp.where(tril, w_ref[g], 0.0).astype(BF16) for g in range(SGU_GROUPS)]

    def chunk(c, carry):
        r = pl.ds(pl.multiple_of(c * SGU_CHUNK, SGU_CHUNK), SGU_CHUNK)
        v = jax.nn.gelu(uv_ref[r, width:].astype(F32))
        mu = jnp.mean(v, axis=-1, keepdims=True)
        vc = v - mu
        var = jnp.mean(vc * vc, axis=-1, keepdims=True)
        vn = (vc * lax.rsqrt(var + EPS) * lng + lnb).astype(BF16)
        for g in range(SGU_GROUPS):
            cs = slice(g * LANES, (g + 1) * LANES)
            mixed = jnp.dot(w_tril[g], vn[:, cs], preferred_element_type=F32) + bt_ref[:, g:g + 1]
            u = jax.nn.gelu(uv_ref[r, cs].astype(F32))
            o_ref[r, cs] = (u * mixed).astype(o_ref.dtype)
        return carry

    lax.fori_loop(0, _exact_div(ts, SGU_CHUNK), chunk, 0)


def _sgu(proj, ln_g, ln_b, w, bt, layer, *, ts):
    n = proj.shape[0]
    width = SGU_GROUPS * LANES
    assert COL_VB == COL_UB + width
    return pl.pallas_call(
        functools.partial(_sgu_kernel, ts=ts),
        grid=(_exact_div(n, ts),),
        in_specs=[pl.BlockSpec((ts, 2 * width), lambda i: (i, _exact_div(COL_UB, 2 * width))),
                  pl.BlockSpec((None, 1, width), lambda i: (layer, 0, 0)),
                  pl.BlockSpec((None, 1, width), lambda i: (layer, 0, 0)),
                  pl.BlockSpec((None, SGU_GROUPS, SGU_CHUNK, SGU_CHUNK), lambda i: (layer, 0, 0, 0)),
                  pl.BlockSpec((None, SGU_CHUNK, SGU_GROUPS), lambda i: (layer, 0, 0))],
        out_specs=pl.BlockSpec((ts, width), lambda i: (i, 0)),
        out_shape=jax.ShapeDtypeStruct((n, width), BF16),
        compiler_params=_params(("parallel",)),
        name="sgu",
    )(proj, ln_g, ln_b, w, bt)


def _hgrn_kernel(x_ref, lb_ref, ng_ref, o_ref, state_ref, kqb_ref, a_ref, *, tt, hb):
    C, SUB = HGRN_CHUNK, HGRN_SUB
    width = hb * LANES
    col_f, col_i, col_q, col_g = (j * width for j in range(4))

    @pl.when(pl.program_id(2) == 0)
    def _():
        state_ref[...] = jnp.zeros_like(state_ref)

    ng = ng_ref[...]
    r64 = lax.broadcasted_iota(jnp.int32, (C, C), 0)
    c64 = lax.broadcasted_iota(jnp.int32, (C, C), 1)
    tril = jnp.where(r64 >= c64, 1.0, 0.0).astype(BF16)
    t_i = lax.broadcasted_iota(jnp.int32, (C, LANES), 0)
    l_i = lax.broadcasted_iota(jnp.int32, (C, LANES), 1)
    own_half = ((l_i // C) == ((t_i // SUB) % 2)) & ((l_i % C) <= t_i)
    valid0 = own_half & (t_i < 2 * SUB)
    valid1 = own_half & (t_i >= 2 * SUB)

    def gates(r, h):
        cs = slice(h * LANES, (h + 1) * LANES)
        lb = lb_ref[:, cs]
        half_span = 0.5 * (1.0 - lb)
        f = (lb + half_span) + half_span * jnp.tanh(0.5 * x_ref[r, col_f + h * LANES:col_f + (h + 1) * LANES].astype(F32))
        lf = jnp.log2(f)
        k = 1.0 - f
        x_q = x_ref[r, col_q + h * LANES:col_q + (h + 1) * LANES].astype(F32)
        qf = x_q * ((0.5 * HGRN_HEAD_DIM ** -0.5) + (0.5 * HGRN_HEAD_DIM ** -0.5) * jnp.tanh(0.5 * x_q))
        hi = lf.astype(BF16)
        lo = (lf - hi.astype(F32)).astype(BF16)
        b2 = jnp.dot(tril, jnp.concatenate([hi, lo], axis=1), preferred_element_type=F32)
        b = b2[:, :LANES] + b2[:, LANES:]
        return k, qf, b

    def scores(k, qf, b):
        refs = [jnp.zeros((1, LANES), F32)] + [b[SUB * i - 1:SUB * i, :] for i in range(1, _exact_div(C, SUB))]
        q_rel = jnp.concatenate(
            [(qf[SUB * i:SUB * (i + 1)] * jnp.exp2(b[SUB * i:SUB * (i + 1)] - rf)).astype(BF16)
             for i, rf in enumerate(refs)], axis=0)
        blocks = []
        for i, rf in enumerate(refs):
            rows = SUB * (i + 1)
            blocks.append((k[:rows] * jnp.exp2(rf - b[:rows])).astype(BF16))
            if rows < C:
                blocks.append(jnp.zeros((C - rows, LANES), BF16))
        kcat = jnp.concatenate(blocks, axis=0)
        sc = lax.dot_general(q_rel, kcat, (((1,), (1,)), ((), ())), preferred_element_type=F32)
        return jnp.where(valid0, sc[:, :LANES], jnp.where(valid1, sc[:, LANES:], 0.0))

    def outputs(r, h, k, qf, b, a):
        cs = slice(h * LANES, (h + 1) * LANES)
        v = x_ref[r, col_i + h * LANES:col_i + (h + 1) * LANES]
        b_last = b[C - 1:C, :]
        q_abs = (qf * jnp.exp2(b)).astype(BF16)
        vv = jnp.concatenate([v, v], axis=0)
        st = state_ref[h]
        o = (lax.dot_general(q_abs, st.astype(BF16), (((1,), (1,)), ((), ())), preferred_element_type=F32)
             + jnp.dot(a.astype(BF16), vv, preferred_element_type=F32))
        k_end = (k * jnp.exp2(b_last - b)).astype(BF16)
        state_ref[h] = st * jnp.exp2(b_last) + lax.dot_general(
            v, k_end, (((0,), (0,)), ((), ())), preferred_element_type=F32)
        on = o * lax.rsqrt(jnp.mean(o * o, axis=-1, keepdims=True) + EPS) * ng
        x_g = x_ref[r, col_g + h * LANES:col_g + (h + 1) * LANES].astype(F32)
        o_ref[r, cs] = (on * (x_g * (0.5 + 0.5 * jnp.tanh(0.5 * x_g)))).astype(o_ref.dtype)

    def rows(c):
        return pl.ds(pl.multiple_of(c * C, C), C)

    def stage_scores(r):
        kqb = [gates(r, h) for h in range(hb)]
        a = [scores(*t) for t in kqb]
        for h in range(hb):
            for j, val in enumerate(kqb[h]):
                kqb_ref[j, h] = val
            a_ref[h] = a[h].astype(BF16)

    n_chunks = _exact_div(tt, C)
    stage_scores(rows(0))

    def chunk(c, carry):
        staged = [(kqb_ref[0, h], kqb_ref[1, h], kqb_ref[2, h], a_ref[h]) for h in range(hb)]
        for h in range(hb):
            outputs(rows(c), h, *staged[h])
        stage_scores(rows(jnp.minimum(c + 1, n_chunks - 1)))
        return carry

    lax.fori_loop(0, n_chunks, chunk, 0)


def _hgrn(proj, lb_l, norm_g, layer, *, batch, seq, tt, hb):
    n = proj.shape[0]
    nt = _exact_div(seq, tt)
    width = HGRN_HEADS * HGRN_HEAD_DIM
    bw = hb * LANES

    assert hb == HGRN_HEADS and (COL_IC, COL_QC, COL_GC) == (COL_FC + bw, COL_FC + 2 * bw, COL_FC + 3 * bw)
    return pl.pallas_call(
        functools.partial(_hgrn_kernel, tt=tt, hb=hb),
        grid=(batch, _exact_div(HGRN_HEADS, hb), nt),
        in_specs=[pl.BlockSpec((tt, 4 * bw), lambda b, h, t: (b * nt + t, _exact_div(COL_FC, 4 * bw))),
                  pl.BlockSpec((1, bw), lambda b, h, t: (0, h)),
                  pl.BlockSpec((None, 1, LANES), lambda b, h, t: (layer, 0, 0))],
        out_specs=pl.BlockSpec((tt, bw), lambda b, h, t: (b * nt + t, h)),
        out_shape=jax.ShapeDtypeStruct((n, width), BF16),
        scratch_shapes=[pltpu.VMEM((hb, HGRN_HEAD_DIM, HGRN_HEAD_DIM), F32),
                        pltpu.VMEM((3, hb, HGRN_CHUNK, LANES), F32),
                        pltpu.VMEM((hb, HGRN_CHUNK, LANES), BF16)],
        compiler_params=_params(("parallel", "parallel", "arbitrary")),
        name="hgrn2",
    )(proj, lb_l, norm_g)


def _merge_kernel(x_ref, ya_ref, yb_ref, yc_ref, ga_ref, gb_ref, gc_ref,
                  wa_ref, wb_ref, wc_ref, wo_ref, o_ref, *, tk):
    d = o_ref.shape[1]
    ya, yb, yc = ya_ref[...], yb_ref[...], yc_ref[...]

    def merged(c):
        cs = slice(c * tk, (c + 1) * tk)
        m = (ga_ref[:, cs].astype(F32) * jnp.dot(ya, wa_ref[:, cs], preferred_element_type=F32)
             + gb_ref[:, cs].astype(F32) * jnp.dot(yb, wb_ref[:, cs], preferred_element_type=F32)
             + gc_ref[:, cs].astype(F32) * jnp.dot(yc, wc_ref[:, cs], preferred_element_type=F32))
        return m.astype(BF16)

    n_groups = _exact_div(d, tk)
    acc = x_ref[...]
    m_next = merged(0)
    for c in range(n_groups):
        m_cur = m_next
        if c + 1 < n_groups:
            m_next = merged(c + 1)
        acc = acc + jnp.dot(m_cur, wo_ref[c * tk:(c + 1) * tk, :], preferred_element_type=F32)
    o_ref[...] = acc


def _merge(x, ya, yb, yc, proj, wa, wb, wc, wo, layer, *, tm, tk):
    n, d = x.shape
    wbr = ya.shape[1]
    g0 = _exact_div(COL_GATE, d)

    def gate(br):
        return pl.BlockSpec((tm, d), lambda i: (i, g0 + br))

    def resident(rows):
        return pl.BlockSpec((None, rows, d), lambda i: (layer, 0, 0), pipeline_mode=pl.Buffered(1))

    y_spec = pl.BlockSpec((tm, wbr), lambda i: (i, 0))
    return pl.pallas_call(
        functools.partial(_merge_kernel, tk=tk),
        grid=(_exact_div(n, tm),),
        in_specs=[pl.BlockSpec((tm, d), lambda i: (i, 0)),
                  y_spec, y_spec, y_spec, gate(0), gate(1), gate(2),
                  resident(wbr), resident(wbr), resident(wbr), resident(d)],
        out_specs=pl.BlockSpec((tm, d), lambda i: (i, 0)),
        out_shape=jax.ShapeDtypeStruct((n, d), F32),
        compiler_params=_params(("parallel",)),
        name="merge",
    )(x, ya, yb, yc, proj, proj, proj, wa, wb, wc, wo)


def _xattn_kernel(x_ref, wq_ref, kv_ref, wo_ref, o_ref, h_ref, *, tm):
    _rms_rows(x_ref, None, h_ref, tm)
    q = jnp.dot(h_ref[...], wq_ref[...], preferred_element_type=F32).astype(BF16)
    xw = X_HEADS * X_HEAD_DIM
    scale = X_HEAD_DIM ** -0.5

    def score(hd):
        cs = slice(hd * X_HEAD_DIM, (hd + 1) * X_HEAD_DIM)
        return lax.dot_general(q[:, cs], kv_ref[:, cs], (((1,), (1,)), ((), ())),
                               preferred_element_type=F32) * scale

    def attend(hd, s):
        vh = kv_ref[:, xw + hd * X_HEAD_DIM:xw + (hd + 1) * X_HEAD_DIM]
        mx = jnp.max(s, axis=-1, keepdims=True)
        pe = jnp.exp(s - mx)
        inv = 1.0 / jnp.sum(pe, axis=-1, keepdims=True)
        return (jnp.dot(pe.astype(BF16), vh, preferred_element_type=F32) * inv).astype(BF16)

    outs = []
    s_next = score(0)
    for hd in range(X_HEADS):
        s_cur = s_next
        if hd + 1 < X_HEADS:
            s_next = score(hd + 1)
        outs.append(attend(hd, s_cur))
    oc = jnp.concatenate(outs, axis=1)
    o_ref[...] = x_ref[...] + jnp.dot(oc, wo_ref[...], preferred_element_type=F32)


def _xattn(x, wq, kv, wo, layer, *, seq, mem_len, tm):
    n, d = x.shape
    xw = X_HEADS * X_HEAD_DIM
    tiles_per_seq = _exact_div(seq, tm)
    return pl.pallas_call(
        functools.partial(_xattn_kernel, tm=tm),
        grid=(_exact_div(n, tm),),
        in_specs=[pl.BlockSpec((tm, d), lambda i: (i, 0)),
                  pl.BlockSpec((None, d, xw), lambda i: (layer, 0, 0)),
                  pl.BlockSpec((mem_len, 2 * xw), lambda i: (i // tiles_per_seq, layer)),
                  pl.BlockSpec((None, xw, d), lambda i: (layer, 0, 0))],
        out_specs=pl.BlockSpec((tm, d), lambda i: (i, 0)),
        out_shape=jax.ShapeDtypeStruct((n, d), F32),
        scratch_shapes=[pltpu.VMEM((tm, d), BF16)],
        compiler_params=_params(("parallel",)),
        name="xattn",
    )(x, wq, kv, wo)


def _ffn_kernel(x_ref, fg_ref, wg_ref, wu_ref, wo_ref, o_ref, h_ref, *, tm, final_norm):
    def step(first):
        h = h_ref[...]
        gt = jnp.dot(h, wg_ref[...], preferred_element_type=F32)
        up = jnp.dot(h, wu_ref[...], preferred_element_type=F32)
        a = (jax.nn.silu(gt) * up).astype(BF16)
        down = jnp.dot(a, wo_ref[...], preferred_element_type=F32)
        o_ref[...] = (x_ref[...] if first else o_ref[...]) + down

    @pl.when(pl.program_id(1) == 0)
    def _():
        _rms_rows(x_ref, None, h_ref, tm)
        step(True)

    pl.when(pl.program_id(1) != 0)(functools.partial(step, False))

    if final_norm:
        @pl.when(pl.program_id(1) == pl.num_programs(1) - 1)
        def _():
            _rms_rows(o_ref, fg_ref, o_ref, tm)


def _ffn(x, final_g, w_in, w_out, layer, *, tm, tf, final_norm):
    n, d = x.shape
    nf = _exact_div(w_out.shape[1], tf)
    return pl.pallas_call(
        functools.partial(_ffn_kernel, tm=tm, final_norm=final_norm),
        grid=(_exact_div(n, tm), nf),
        in_specs=[pl.BlockSpec((tm, d), lambda i, f: (i, 0)),
                  pl.BlockSpec((1, d), lambda i, f: (0, 0)),
                  pl.BlockSpec((None, d, tf), lambda i, f: (layer, 0, f)),
                  pl.BlockSpec((None, d, tf), lambda i, f: (layer, 0, nf + f)),
                  pl.BlockSpec((None, tf, d), lambda i, f: (layer, f, 0))],
        out_specs=pl.BlockSpec((tm, d), lambda i, f: (i, 0)),
        out_shape=jax.ShapeDtypeStruct((n, d), F32),
        scratch_shapes=[pltpu.VMEM((tm, d), BF16)],
        compiler_params=_params(("parallel", "arbitrary")),
        name="ffn",
    )(x, final_g, w_in, w_in, w_out)


def kernel(x, mem, norm_mix, w_in, w_gate, sinks, sgu_ln_g, sgu_ln_b, sgu_w, sgu_b, hgrn_lb, hgrn_norm,
           w_br_a, w_br_b, w_br_c, w_out, norm_x, mem_norm, w_xq, w_xkv, w_xo, norm_ffn, w_ffn_in,
           w_ffn_out, final_norm):
    batch, seq, d = x.shape
    depth = w_in.shape[0]
    mem_len = mem.shape[1]
    n = batch * seq

    kv_end = (ATTN_Q_HEADS + 2 * ATTN_KV_HEADS) * ATTN_HEAD_DIM
    g_mix = norm_mix[:, :, None]
    w_all = jnp.concatenate(
        [(w * g_mix).astype(BF16)
         for w in (w_gate, w_in[..., kv_end:], w_in[..., :kv_end])], axis=-1)
    assert w_all.shape[-1] == COL_END
    wa, wb, wc, wo = (w.astype(BF16) for w in (w_br_a, w_br_b, w_br_c, w_out))
    wxq, wxo = (w_xq * norm_x[:, :, None]).astype(BF16), w_xo.astype(BF16)
    wxkv = jnp.concatenate([w_xkv[l] * mem_norm[:, None] for l in range(depth)], axis=-1).astype(BF16)
    wfi, wfo = (w_ffn_in * norm_ffn[:, :, None]).astype(BF16), w_ffn_out.astype(BF16)
    sgu_bt = jnp.swapaxes(sgu_b, 1, 2)
    sm = jax.nn.softmax(hgrn_lb.astype(F32), axis=0)
    lb_all = jnp.cumsum(sm, axis=0) - sm[0:1]
    sgu_ln_g, sgu_ln_b, hgrn_norm = (p[:, None, :] for p in (sgu_ln_g, sgu_ln_b, hgrn_norm))

    xf = x.reshape(n, d)
    kv_all = _norm_matmul(mem.reshape(batch * mem_len, d), wxkv, 0, tm=1024, tn=512)
    for l in range(depth):
        proj = _norm_matmul(xf, w_all, l, tm=1024, tn=1536, activate=True)
        ya = _swa(proj, sinks[l], batch=batch, seq=seq, tq=512)
        yb = _sgu(proj, sgu_ln_g, sgu_ln_b, sgu_w, sgu_bt, l, ts=1024)
        yc = _hgrn(proj, lb_all[l:l + 1], hgrn_norm, l, batch=batch, seq=seq, tt=seq, hb=8)
        xf = _merge(xf, ya, yb, yc, proj, wa, wb, wc, wo, l, tm=256, tk=512)
        xf = _xattn(xf, wxq, kv_all, wxo, l, seq=seq, mem_len=mem_len, tm=1024)
        xf = _ffn(xf, final_norm.reshape(1, d), wfi, wfo, l, tm=1024, tf=512,
                  final_norm=(l == depth - 1))
    return xf.reshape(batch, seq, d)
```

```python
import functools

import jax
import jax.numpy as jnp
from jax import lax
from jax.experimental import pallas as pl
from jax.experimental.pallas import tpu as pltpu

F32 = jnp.float32
BF16 = jnp.bfloat16
EPS = 1e-6

LANES = 128
V7X_VMEM_LIMIT = 56 * 1024 * 1024

ATTN_HEAD_DIM = 64
ATTN_Q_HEADS = 16
ATTN_KV_HEADS = 4
WINDOW = 128
SGU_CHUNK = 128
SGU_GROUPS = 8
HGRN_HEADS = 8
HGRN_HEAD_DIM = 128
HGRN_CHUNK = 64
HGRN_SUB = 16
X_HEADS = 4
X_HEAD_DIM = 128

COL_GATE, COL_UB, COL_VB, COL_FC, COL_IC, COL_QC, COL_GC, COL_QA, COL_KV, COL_END = (
    0, 6144, 7168, 8192, 9216, 10240, 11264, 12288, 13312, 13824)
ACT_SUB = 512
LOG2E = 1.4426950408889634


def _exact_div(a, b):
    assert a % b == 0, (a, b)
    return a // b


def _params(sem):
    return pltpu.CompilerParams(dimension_semantics=sem, vmem_limit_bytes=V7X_VMEM_LIMIT)


def _rms_rows(x_ref, g_ref, h_ref, rows, copy_ref=None):
    slab = 64
    g = None if g_ref is None else g_ref[...]

    def body(i, c):
        r = pl.ds(pl.multiple_of(i * slab, slab), slab)
        x = x_ref[r, :]
        if copy_ref is not None:
            copy_ref[r, :] = x
        ms = jnp.mean(x * x, axis=-1, keepdims=True)
        y = x * lax.rsqrt(ms + EPS)
        h_ref[r, :] = (y if g is None else y * g).astype(h_ref.dtype)
        return c

    lax.fori_loop(0, _exact_div(rows, slab), body, 0, unroll=4)


def _in_range(c0, lo_hi):
    _exact_div(lo_hi[0], ACT_SUB), _exact_div(lo_hi[1], ACT_SUB)
    return (c0 >= lo_hi[0]) & (c0 < lo_hi[1])


def _projection_activation(x, c0, gate_tile):
    if gate_tile:
        return 0.5 + 0.5 * jnp.tanh(0.5 * x)
    return x * jnp.where(_in_range(c0, (COL_QA, COL_KV)), ATTN_HEAD_DIM ** -0.5 * LOG2E, 1.0)


def _norm_matmul_kernel(x_ref, w_ref, o_ref, h_ref, *, activate, tm, tn):
    j = pl.program_id(1)

    @pl.when(j == 0)
    def _():
        _rms_rows(x_ref, None, h_ref, tm)

    if not activate:
        o_ref[...] = jnp.dot(h_ref[...], w_ref[...], preferred_element_type=F32).astype(o_ref.dtype)
    else:
        assert COL_GATE == 0
        n_gate_tiles = _exact_div(COL_UB, tn)

        def step(gate_tile):
            h = h_ref[...]
            for s in range(_exact_div(tn, ACT_SUB)):
                cs = slice(s * ACT_SUB, (s + 1) * ACT_SUB)
                acc = jnp.dot(h, w_ref[:, cs], preferred_element_type=F32)
                o_ref[:, cs] = _projection_activation(acc, j * tn + s * ACT_SUB, gate_tile).astype(o_ref.dtype)

        pl.when(j < n_gate_tiles)(functools.partial(step, True))
        pl.when(j >= n_gate_tiles)(functools.partial(step, False))


def _norm_matmul(x, w, layer, *, tm, tn, activate=False):
    n, d = x.shape
    nout = w.shape[-1]
    kern = functools.partial(_norm_matmul_kernel, activate=activate, tm=tm, tn=tn)
    if w.ndim == 3:
        w_spec = pl.BlockSpec((None, d, tn), lambda i, j: (layer, 0, j))
    else:
        w_spec = pl.BlockSpec((d, tn), lambda i, j: (0, j))
    return pl.pallas_call(
        kern,
        grid=(_exact_div(n, tm), _exact_div(nout, tn)),
        in_specs=[pl.BlockSpec((tm, d), lambda i, j: (i, 0)),
                  w_spec],
        out_specs=pl.BlockSpec((tm, tn), lambda i, j: (i, j)),
        out_shape=jax.ShapeDtypeStruct((n, nout), BF16),
        scratch_shapes=[pltpu.VMEM((tm, d), BF16)],
        compiler_params=_params(("parallel", "arbitrary")),
        name="norm_matmul",
    )(x, w)


def _swap_halves(t):
    return jnp.concatenate([t[:, 64:], t[:, :64]], axis=1)


def _swa_kernel(sink_ref, q_ref, kvc_ref, kvp_ref, o_ref, kv_buf, *, tq):
    t_idx = pl.program_id(1)
    kv_buf[0:WINDOW, :] = kvp_ref[...]
    kv_buf[WINDOW:, :] = kvc_ref[...]

    row = lax.broadcasted_iota(jnp.int32, (WINDOW, 2 * WINDOW), 0)
    col = lax.broadcasted_iota(jnp.int32, (WINDOW, 2 * WINDOW), 1)
    band = (col > row) & (col <= row + WINDOW)
    lane = lax.broadcasted_iota(jnp.int32, (2 * WINDOW, LANES), 1)
    lo = lane < 64
    out_lane = lax.broadcasted_iota(jnp.int32, (WINDOW, LANES), 1)

    def blk(i, c):
        r0 = pl.multiple_of(i * WINDOW, WINDOW)
        cmin = jnp.where((t_idx == 0) & (i == 0), WINDOW, 0)
        valid = band & (col >= cmin)
        kcats, vcats = [], []
        for m in range(2):
            kt = kv_buf[pl.ds(r0, 2 * WINDOW), m * LANES:(m + 1) * LANES]
            vt = kv_buf[pl.ds(r0, 2 * WINDOW), 256 + m * LANES:256 + (m + 1) * LANES]
            ks, vs = _swap_halves(kt), _swap_halves(vt)
            zero = jnp.zeros_like(kt)
            for e in range(2):
                k_lo = jnp.where(lo, kt if e == 0 else ks, zero)
                k_hi = jnp.where(lo, zero, ks if e == 0 else kt)
                v_lo = jnp.where(lo, vt if e == 0 else vs, zero)
                v_hi = jnp.where(lo, zero, vs if e == 0 else vt)
                kcats.append(jnp.concatenate([k_lo, k_hi], axis=0))
                vcats.append(jnp.concatenate([v_lo, v_hi], axis=0))

        def score(j):
            qt = jnp.concatenate([q_ref[pl.ds(r0, WINDOW), p * LANES:(p + 1) * LANES]
                                  for p in (2 * j, 2 * j + 1)], axis=0)
            return lax.dot_general(qt, kcats[j], (((1,), (1,)), ((), ())),
                                   preferred_element_type=F32)

        def attend(j, s):
            pcats, scales = [], []
            for t in range(2):
                p = 2 * j + t
                probs, invs = [], []
                for hh in range(2):
                    sink = sink_ref[2 * p + hh] * LOG2E
                    sh = jnp.where(valid, s[t * WINDOW:(t + 1) * WINDOW, hh * 256:(hh + 1) * 256], -jnp.inf)
                    mx = jnp.maximum(jnp.max(sh, axis=-1, keepdims=True), sink)
                    pe = jnp.exp2(sh - mx)
                    den = jnp.sum(pe, axis=-1, keepdims=True) + jnp.exp2(sink - mx)
                    probs.append(pe.astype(BF16))
                    invs.append(1.0 / den)
                pcats.append(jnp.concatenate(probs, axis=1))
                scales.append(jnp.where(out_lane < 64, invs[0], invs[1]))
            o = jnp.dot(jnp.concatenate(pcats, axis=0), vcats[j], preferred_element_type=F32)
            for t in range(2):
                p = 2 * j + t
                o_ref[pl.ds(r0, WINDOW), p * LANES:(p + 1) * LANES] = (
                    o[t * WINDOW:(t + 1) * WINDOW] * scales[t]).astype(o_ref.dtype)

        s_next = score(0)
        for j in range(ATTN_KV_HEADS):
            s_cur = s_next
            if j + 1 < ATTN_KV_HEADS:
                s_next = score(j + 1)
            attend(j, s_cur)
        return c

    lax.fori_loop(0, _exact_div(tq, WINDOW), blk, 0)


def _swa(proj, sinks_l, *, batch, seq, tq):
    n = proj.shape[0]
    nt = _exact_div(seq, tq)
    qw = ATTN_Q_HEADS * ATTN_HEAD_DIM
    kvw = 2 * ATTN_KV_HEADS * ATTN_HEAD_DIM
    kv_blk = _exact_div(COL_KV, kvw)
    bpt = _exact_div(tq, WINDOW)

    def prev_map(b, t):
        return (jnp.maximum(b * _exact_div(seq, WINDOW) + t * bpt - 1, 0), kv_blk)

    return pl.pallas_call(
        functools.partial(_swa_kernel, tq=tq),
        grid=(batch, nt),
        in_specs=[pl.BlockSpec(memory_space=pltpu.SMEM),
                  pl.BlockSpec((tq, qw), lambda b, t: (b * nt + t, _exact_div(COL_QA, qw))),
                  pl.BlockSpec((tq, kvw), lambda b, t: (b * nt + t, kv_blk)),
                  pl.BlockSpec((WINDOW, kvw), prev_map)],
        out_specs=pl.BlockSpec((tq, qw), lambda b, t: (b * nt + t, 0)),
        out_shape=jax.ShapeDtypeStruct((n, qw), BF16),
        scratch_shapes=[pltpu.VMEM((tq + WINDOW, kvw), BF16)],
        compiler_params=_params(("parallel", "parallel")),
        name="swa",
    )(sinks_l, proj, proj, proj)


def _sgu_kernel(uv_ref, lng_ref, lnb_ref, w_ref, bt_ref, o_ref, *, ts):
    width = SGU_GROUPS * LANES
    row = lax.broadcasted_iota(jnp.int32, (SGU_CHUNK, SGU_CHUNK), 0)
    col = lax.broadcasted_iota(jnp.int32, (SGU_CHUNK, SGU_CHUNK), 1)
    tril = row >= col
    lng = lng_ref[...]
    lnb = lnb_ref[...]
    w_tril = [jnp.where(tril, w_ref[g], 0.0).astype(BF16) for g in range(SGU_GROUPS)]

    def chunk(c, carry):
        r = pl.ds(pl.multiple_of(c * SGU_CHUNK, SGU_CHUNK), SGU_CHUNK)
        v = jax.nn.gelu(uv_ref[r, width:].astype(F32))
        mu = jnp.mean(v, axis=-1, keepdims=True)
        vc = v - mu
        var = jnp.mean(vc * vc, axis=-1, keepdims=True)
        vn = (vc * lax.rsqrt(var + EPS) * lng + lnb).astype(BF16)
        for g in range(SGU_GROUPS):
            cs = slice(g * LANES, (g + 1) * LANES)
            mixed = jnp.dot(w_tril[g], vn[:, cs], preferred_element_type=F32) + bt_ref[:, g:g + 1]
            u = jax.nn.gelu(uv_ref[r, cs].astype(F32))
            o_ref[r, cs] = (u * mixed).astype(o_ref.dtype)
        return carry

    lax.fori_loop(0, _exact_div(ts, SGU_CHUNK), chunk, 0)


def _sgu(proj, ln_g, ln_b, w, bt, layer, *, ts):
    n = proj.shape[0]
    width = SGU_GROUPS * LANES
    assert COL_VB == COL_UB + width
    return pl.pallas_call(
        functools.partial(_sgu_kernel, ts=ts),
        grid=(_exact_div(n, ts),),
        in_specs=[pl.BlockSpec((ts, 2 * width), lambda i: (i, _exact_div(COL_UB, 2 * width))),
                  pl.BlockSpec((None, 1, width), lambda i: (layer, 0, 0)),
                  pl.BlockSpec((None, 1, width), lambda i: (layer, 0, 0)),
                  pl.BlockSpec((None, SGU_GROUPS, SGU_CHUNK, SGU_CHUNK), lambda i: (layer, 0, 0, 0)),
                  pl.BlockSpec((None, SGU_CHUNK, SGU_GROUPS), lambda i: (layer, 0, 0))],
        out_specs=pl.BlockSpec((ts, width), lambda i: (i, 0)),
        out_shape=jax.ShapeDtypeStruct((n, width), BF16),
        compiler_params=_params(("parallel",)),
        name="sgu",
    )(proj, ln_g, ln_b, w, bt)


def _hgrn_kernel(x_ref, lb_ref, ng_ref, o_ref, state_ref, kqb_ref, a_ref, *, tt, hb):
    C, SUB = HGRN_CHUNK, HGRN_SUB
    width = hb * LANES
    col_f, col_i, col_q, col_g = (j * width for j in range(4))

    @pl.when(pl.program_id(2) == 0)
    def _():
        state_ref[...] = jnp.zeros_like(state_ref)

    ng = ng_ref[...]
    r64 = lax.broadcasted_iota(jnp.int32, (C, C), 0)
    c64 = lax.broadcasted_iota(jnp.int32, (C, C), 1)
    tril = jnp.where(r64 >= c64, 1.0, 0.0).astype(BF16)
    t_i = lax.broadcasted_iota(jnp.int32, (C, LANES), 0)
    l_i = lax.broadcasted_iota(jnp.int32, (C, LANES), 1)
    own_half = ((l_i // C) == ((t_i // SUB) % 2)) & ((l_i % C) <= t_i)
    valid0 = own_half & (t_i < 2 * SUB)
    valid1 = own_half & (t_i >= 2 * SUB)

    def gates(r, h):
        cs = slice(h * LANES, (h + 1) * LANES)
        lb = lb_ref[:, cs]
        half_span = 0.5 * (1.0 - lb)
        f = (lb + half_span) + half_span * jnp.tanh(0.5 * x_ref[r, col_f + h * LANES:col_f + (h + 1) * LANES].astype(F32))
        lf = jnp.log2(f)
        k = 1.0 - f
        x_q = x_ref[r, col_q + h * LANES:col_q + (h + 1) * LANES].astype(F32)
        qf = x_q * ((0.5 * HGRN_HEAD_DIM ** -0.5) + (0.5 * HGRN_HEAD_DIM ** -0.5) * jnp.tanh(0.5 * x_q))
        hi = lf.astype(BF16)
        lo = (lf - hi.astype(F32)).astype(BF16)
        b2 = jnp.dot(tril, jnp.concatenate([hi, lo], axis=1), preferred_element_type=F32)
        b = b2[:, :LANES] + b2[:, LANES:]
        return k, qf, b

    def scores(k, qf, b):
        refs = [jnp.zeros((1, LANES), F32)] + [b[SUB * i - 1:SUB * i, :] for i in range(1, _exact_div(C, SUB))]
        q_rel = jnp.concatenate(
            [(qf[SUB * i:SUB * (i + 1)] * jnp.exp2(b[SUB * i:SUB * (i + 1)] - rf)).astype(BF16)
             for i, rf in enumerate(refs)], axis=0)
        blocks = []
        for i, rf in enumerate(refs):
            rows = SUB * (i + 1)
            blocks.append((k[:rows] * jnp.exp2(rf - b[:rows])).astype(BF16))
            if rows < C:
                blocks.append(jnp.zeros((C - rows, LANES), BF16))
        kcat = jnp.concatenate(blocks, axis=0)
        sc = lax.dot_general(q_rel, kcat, (((1,), (1,)), ((), ())), preferred_element_type=F32)
        return jnp.where(valid0, sc[:, :LANES], jnp.where(valid1, sc[:, LANES:], 0.0))

    def outputs(r, h, k, qf, b, a):
        cs = slice(h * LANES, (h + 1) * LANES)
        v = x_ref[r, col_i + h * LANES:col_i + (h + 1) * LANES]
        b_last = b[C - 1:C, :]
        q_abs = (qf * jnp.exp2(b)).astype(BF16)
        vv = jnp.concatenate([v, v], axis=0)
        st = state_ref[h]
        o = (lax.dot_general(q_abs, st.astype(BF16), (((1,), (1,)), ((), ())), preferred_element_type=F32)
             + jnp.dot(a.astype(BF16), vv, preferred_element_type=F32))
        k_end = (k * jnp.exp2(b_last - b)).astype(BF16)
        state_ref[h] = st * jnp.exp2(b_last) + lax.dot_general(
            v, k_end, (((0,), (0,)), ((), ())), preferred_element_type=F32)
        on = o * lax.rsqrt(jnp.mean(o * o, axis=-1, keepdims=True) + EPS) * ng
        x_g = x_ref[r, col_g + h * LANES:col_g + (h + 1) * LANES].astype(F32)
        o_ref[r, cs] = (on * (x_g * (0.5 + 0.5 * jnp.tanh(0.5 * x_g)))).astype(o_ref.dtype)

    def rows(c):
        return pl.ds(pl.multiple_of(c * C, C), C)

    def stage_scores(r):
        kqb = [gates(r, h) for h in range(hb)]
        a = [scores(*t) for t in kqb]
        for h in range(hb):
            for j, val in enumerate(kqb[h]):
                kqb_ref[j, h] = val
            a_ref[h] = a[h].astype(BF16)

    n_chunks = _exact_div(tt, C)
    stage_scores(rows(0))

    def chunk(c, carry):
        staged = [(kqb_ref[0, h], kqb_ref[1, h], kqb_ref[2, h], a_ref[h]) for h in range(hb)]
        for h in range(hb):
            outputs(rows(c), h, *staged[h])
        stage_scores(rows(jnp.minimum(c + 1, n_chunks - 1)))
        return carry

    lax.fori_loop(0, n_chunks, chunk, 0)


def _hgrn(proj, lb_l, norm_g, layer, *, batch, seq, tt, hb):
    n = proj.shape[0]
    nt = _exact_div(seq, tt)
    width = HGRN_HEADS * HGRN_HEAD_DIM
    bw = hb * LANES

    assert hb == HGRN_HEADS and (COL_IC, COL_QC, COL_GC) == (COL_FC + bw, COL_FC + 2 * bw, COL_FC + 3 * bw)
    return pl.pallas_call(
        functools.partial(_hgrn_kernel, tt=tt, hb=hb),
        grid=(batch, _exact_div(HGRN_HEADS, hb), nt),
        in_specs=[pl.BlockSpec((tt, 4 * bw), lambda b, h, t: (b * nt + t, _exact_div(COL_FC, 4 * bw))),
                  pl.BlockSpec((1, bw), lambda b, h, t: (0, h)),
                  pl.BlockSpec((None, 1, LANES), lambda b, h, t: (layer, 0, 0))],
        out_specs=pl.BlockSpec((tt, bw), lambda b, h, t: (b * nt + t, h)),
        out_shape=jax.ShapeDtypeStruct((n, width), BF16),
        scratch_shapes=[pltpu.VMEM((hb, HGRN_HEAD_DIM, HGRN_HEAD_DIM), F32),
                        pltpu.VMEM((3, hb, HGRN_CHUNK, LANES), F32),
                        pltpu.VMEM((hb, HGRN_CHUNK, LANES), BF16)],
        compiler_params=_params(("parallel", "parallel", "arbitrary")),
        name="hgrn2",
    )(proj, lb_l, norm_g)


def _merge_kernel(x_ref, ya_ref, yb_ref, yc_ref, ga_ref, gb_ref, gc_ref,
                  wa_ref, wb_ref, wc_ref, wo_ref, o_ref, *, tk):
    d = o_ref.shape[1]
    ya, yb, yc = ya_ref[...], yb_ref[...], yc_ref[...]

    def merged(c):
        cs = slice(c * tk, (c + 1) * tk)
        m = (ga_ref[:, cs].astype(F32) * jnp.dot(ya, wa_ref[:, cs], preferred_element_type=F32)
             + gb_ref[:, cs].astype(F32) * jnp.dot(yb, wb_ref[:, cs], preferred_element_type=F32)
             + gc_ref[:, cs].astype(F32) * jnp.dot(yc, wc_ref[:, cs], preferred_element_type=F32))
        return m.astype(BF16)

    n_groups = _exact_div(d, tk)
    acc = x_ref[...]
    m_next = merged(0)
    for c in range(n_groups):
        m_cur = m_next
        if c + 1 < n_groups:
            m_next = merged(c + 1)
        acc = acc + jnp.dot(m_cur, wo_ref[c * tk:(c + 1) * tk, :], preferred_element_type=F32)
    o_ref[...] = acc


def _merge(x, ya, yb, yc, proj, wa, wb, wc, wo, layer, *, tm, tk):
    n, d = x.shape
    wbr = ya.shape[1]
    g0 = _exact_div(COL_GATE, d)

    def gate(br):
        return pl.BlockSpec((tm, d), lambda i: (i, g0 + br))

    def resident(rows):
        return pl.BlockSpec((None, rows, d), lambda i: (layer, 0, 0), pipeline_mode=pl.Buffered(1))

    y_spec = pl.BlockSpec((tm, wbr), lambda i: (i, 0))
    return pl.pallas_call(
        functools.partial(_merge_kernel, tk=tk),
        grid=(_exact_div(n, tm),),
        in_specs=[pl.BlockSpec((tm, d), lambda i: (i, 0)),
                  y_spec, y_spec, y_spec, gate(0), gate(1), gate(2),
                  resident(wbr), resident(wbr), resident(wbr), resident(d)],
        out_specs=pl.BlockSpec((tm, d), lambda i: (i, 0)),
        out_shape=jax.ShapeDtypeStruct((n, d), F32),
        compiler_params=_params(("parallel",)),
        name="merge",
    )(x, ya, yb, yc, proj, proj, proj, wa, wb, wc, wo)


def _xattn_kernel(x_ref, wq_ref, kv_ref, wo_ref, o_ref, h_ref, *, tm):
    _rms_rows(x_ref, None, h_ref, tm)
    q = jnp.dot(h_ref[...], wq_ref[...], preferred_element_type=F32).astype(BF16)
    xw = X_HEADS * X_HEAD_DIM
    scale = X_HEAD_DIM ** -0.5 * LOG2E

    def score(hd):
        cs = slice(hd * X_HEAD_DIM, (hd + 1) * X_HEAD_DIM)
        return lax.dot_general(q[:, cs], kv_ref[:, cs], (((1,), (1,)), ((), ())),
                               preferred_element_type=F32) * scale

    def attend(hd, s):
        vh = kv_ref[:, xw + hd * X_HEAD_DIM:xw + (hd + 1) * X_HEAD_DIM]
        mx = jnp.max(s, axis=-1, keepdims=True)
        pe = jnp.exp2(s - mx)
        inv = 1.0 / jnp.sum(pe, axis=-1, keepdims=True)
        return (jnp.dot(pe.astype(BF16), vh, preferred_element_type=F32) * inv).astype(BF16)

    outs = []
    s_next = score(0)
    for hd in range(X_HEADS):
        s_cur = s_next
        if hd + 1 < X_HEADS:
            s_next = score(hd + 1)
        outs.append(attend(hd, s_cur))
    oc = jnp.concatenate(outs, axis=1)
    o_ref[...] = x_ref[...] + jnp.dot(oc, wo_ref[...], preferred_element_type=F32)


def _xattn(x, wq, kv, wo, layer, *, seq, mem_len, tm):
    n, d = x.shape
    xw = X_HEADS * X_HEAD_DIM
    tiles_per_seq = _exact_div(seq, tm)
    return pl.pallas_call(
        functools.partial(_xattn_kernel, tm=tm),
        grid=(_exact_div(n, tm),),
        in_specs=[pl.BlockSpec((tm, d), lambda i: (i, 0)),
                  pl.BlockSpec((None, d, xw), lambda i: (layer, 0, 0)),
                  pl.BlockSpec((mem_len, 2 * xw), lambda i: (i // tiles_per_seq, layer)),
                  pl.BlockSpec((None, xw, d), lambda i: (layer, 0, 0))],
        out_specs=pl.BlockSpec((tm, d), lambda i: (i, 0)),
        out_shape=jax.ShapeDtypeStruct((n, d), F32),
        scratch_shapes=[pltpu.VMEM((tm, d), BF16)],
        compiler_params=_params(("parallel",)),
        name="xattn",
    )(x, wq, kv, wo)


def _ffn_kernel(x_ref, fg_ref, wg_ref, wu_ref, wo_ref, o_ref, h_ref, *, tm, final_norm):
    def step(first):
        h = h_ref[...]
        gt = jnp.dot(h, wg_ref[...], preferred_element_type=F32)
        up = jnp.dot(h, wu_ref[...], preferred_element_type=F32)
        a = (jax.nn.silu(gt) * up).astype(BF16)
        down = jnp.dot(a, wo_ref[...], preferred_element_type=F32)
        o_ref[...] = (x_ref[...] if first else o_ref[...]) + down

    @pl.when(pl.program_id(1) == 0)
    def _():
        _rms_rows(x_ref, None, h_ref, tm)
        step(True)

    pl.when(pl.program_id(1) != 0)(functools.partial(step, False))

    if final_norm:
        @pl.when(pl.program_id(1) == pl.num_programs(1) - 1)
        def _():
            _rms_rows(o_ref, fg_ref, o_ref, tm)


def _ffn(x, final_g, w_in, w_out, layer, *, tm, tf, final_norm):
    n, d = x.shape
    nf = _exact_div(w_out.shape[1], tf)
    return pl.pallas_call(
        functools.partial(_ffn_kernel, tm=tm, final_norm=final_norm),
        grid=(_exact_div(n, tm), nf),
        in_specs=[pl.BlockSpec((tm, d), lambda i, f: (i, 0)),
                  pl.BlockSpec((1, d), lambda i, f: (0, 0)),
                  pl.BlockSpec((None, d, tf), lambda i, f: (layer, 0, f)),
                  pl.BlockSpec((None, d, tf), lambda i, f: (layer, 0, nf + f)),
                  pl.BlockSpec((None, tf, d), lambda i, f: (layer, f, 0))],
        out_specs=pl.BlockSpec((tm, d), lambda i, f: (i, 0)),
        out_shape=jax.ShapeDtypeStruct((n, d), F32),
        scratch_shapes=[pltpu.VMEM((tm, d), BF16)],
        compiler_params=_params(("parallel", "arbitrary")),
        name="ffn",
    )(x, final_g, w_in, w_in, w_out)


def kernel(x, mem, norm_mix, w_in, w_gate, sinks, sgu_ln_g, sgu_ln_b, sgu_w, sgu_b, hgrn_lb, hgrn_norm,
           w_br_a, w_br_b, w_br_c, w_out, norm_x, mem_norm, w_xq, w_xkv, w_xo, norm_ffn, w_ffn_in,
           w_ffn_out, final_norm):
    batch, seq, d = x.shape
    depth = w_in.shape[0]
    mem_len = mem.shape[1]
    n = batch * seq

    kv_end = (ATTN_Q_HEADS + 2 * ATTN_KV_HEADS) * ATTN_HEAD_DIM
    g_mix = norm_mix[:, :, None]
    w_all = jnp.concatenate(
        [(w * g_mix).astype(BF16)
         for w in (w_gate, w_in[..., kv_end:], w_in[..., :kv_end])], axis=-1)
    assert w_all.shape[-1] == COL_END
    wa, wb, wc, wo = (w.astype(BF16) for w in (w_br_a, w_br_b, w_br_c, w_out))
    wxq, wxo = (w_xq * norm_x[:, :, None]).astype(BF16), w_xo.astype(BF16)
    wxkv = jnp.concatenate([w_xkv[l] * mem_norm[:, None] for l in range(depth)], axis=-1).astype(BF16)
    wfi, wfo = (w_ffn_in * norm_ffn[:, :, None]).astype(BF16), w_ffn_out.astype(BF16)
    sgu_bt = jnp.swapaxes(sgu_b, 1, 2)
    sm = jax.nn.softmax(hgrn_lb.astype(F32), axis=0)
    lb_all = jnp.cumsum(sm, axis=0) - sm[0:1]
    sgu_ln_g, sgu_ln_b, hgrn_norm = (p[:, None, :] for p in (sgu_ln_g, sgu_ln_b, hgrn_norm))

    xf = x.reshape(n, d)
    kv_all = _norm_matmul(mem.reshape(batch * mem_len, d), wxkv, 0, tm=1024, tn=512)
    for l in range(depth):
        proj = _norm_matmul(xf, w_all, l, tm=1024, tn=1536, activate=True)
        ya = _swa(proj, sinks[l], batch=batch, seq=seq, tq=512)
        yb = _sgu(proj, sgu_ln_g, sgu_ln_b, sgu_w, sgu_bt, l, ts=1024)
        yc = _hgrn(proj, lb_all[l:l + 1], hgrn_norm, l, batch=batch, seq=seq, tt=seq, hb=8)
        xf = _merge(xf, ya, yb, yc, proj, wa, wb, wc, wo, l, tm=256, tk=512)
        xf = _xattn(xf, wxq, kv_all, wxo, l, seq=seq, mem_len=mem_len, tm=1024)
        xf = _ffn(xf, final_norm.reshape(1, d), wfi, wfo, l, tm=1024, tf=512,
                  final_norm=(l == depth - 1))
    return xf.reshape(batch, seq, d)
```

```python
import functools

import jax
import jax.numpy as jnp
from jax import lax
from jax.experimental import pallas as pl
from jax.experimental.pallas import tpu as pltpu

F32 = jnp.float32
BF16 = jnp.bfloat16
EPS = 1e-6

LANES = 128
V7X_VMEM_LIMIT = 56 * 1024 * 1024
V7X_VMEM_LIMIT_MERGE = 62 * 1024 * 1024

ATTN_HEAD_DIM = 64
ATTN_Q_HEADS = 16
ATTN_KV_HEADS = 4
WINDOW = 128
SGU_CHUNK = 128
SGU_GROUPS = 8
HGRN_HEADS = 8
HGRN_HEAD_DIM = 128
HGRN_CHUNK = 64
HGRN_SUB = 16
X_HEADS = 4
X_HEAD_DIM = 128

COL_GATE, COL_UB, COL_VB, COL_FC, COL_IC, COL_QC, COL_GC, COL_QA, COL_KV, COL_END = (
    0, 6144, 7168, 8192, 9216, 10240, 11264, 12288, 13312, 13824)
ACT_SUB = 512
LOG2E = 1.4426950408889634


def _exact_div(a, b):
    assert a % b == 0, (a, b)
    return a // b


def _params(sem, vmem_limit=V7X_VMEM_LIMIT):
    return pltpu.CompilerParams(dimension_semantics=sem, vmem_limit_bytes=vmem_limit)


def _rms_rows(x_ref, g_ref, h_ref, rows, copy_ref=None):
    slab = 64
    g = None if g_ref is None else g_ref[...]

    def body(i, c):
        r = pl.ds(pl.multiple_of(i * slab, slab), slab)
        x = x_ref[r, :]
        if copy_ref is not None:
            copy_ref[r, :] = x
        ms = jnp.mean(x * x, axis=-1, keepdims=True)
        y = x * lax.rsqrt(ms + EPS)
        h_ref[r, :] = (y if g is None else y * g).astype(h_ref.dtype)
        return c

    lax.fori_loop(0, _exact_div(rows, slab), body, 0, unroll=4)


def _in_range(c0, lo_hi):
    _exact_div(lo_hi[0], ACT_SUB), _exact_div(lo_hi[1], ACT_SUB)
    return (c0 >= lo_hi[0]) & (c0 < lo_hi[1])


def _projection_activation(x, c0, gate_tile):
    if gate_tile:
        return 0.5 + 0.5 * jnp.tanh(0.5 * x)
    return x * jnp.where(_in_range(c0, (COL_QA, COL_KV)), ATTN_HEAD_DIM ** -0.5 * LOG2E, 1.0)


def _norm_matmul_kernel(x_ref, w_ref, o_ref, h_ref, *, activate, tm, tn):
    j = pl.program_id(1)

    @pl.when(j == 0)
    def _():
        _rms_rows(x_ref, None, h_ref, tm)

    if not activate:
        o_ref[...] = jnp.dot(h_ref[...], w_ref[...], preferred_element_type=F32).astype(o_ref.dtype)
    else:
        assert COL_GATE == 0
        n_gate_tiles = _exact_div(COL_UB, tn)

        def step(gate_tile):
            h = h_ref[...]
            for s in range(_exact_div(tn, ACT_SUB)):
                cs = slice(s * ACT_SUB, (s + 1) * ACT_SUB)
                acc = jnp.dot(h, w_ref[:, cs], preferred_element_type=F32)
                o_ref[:, cs] = _projection_activation(acc, j * tn + s * ACT_SUB, gate_tile).astype(o_ref.dtype)

        pl.when(j < n_gate_tiles)(functools.partial(step, True))
        pl.when(j >= n_gate_tiles)(functools.partial(step, False))


def _norm_matmul(x, w, layer, *, tm, tn, activate=False):
    n, d = x.shape
    nout = w.shape[-1]
    kern = functools.partial(_norm_matmul_kernel, activate=activate, tm=tm, tn=tn)
    if w.ndim == 3:
        w_spec = pl.BlockSpec((None, d, tn), lambda i, j: (layer, 0, j))
    else:
        w_spec = pl.BlockSpec((d, tn), lambda i, j: (0, j))
    return pl.pallas_call(
        kern,
        grid=(_exact_div(n, tm), _exact_div(nout, tn)),
        in_specs=[pl.BlockSpec((tm, d), lambda i, j: (i, 0)),
                  w_spec],
        out_specs=pl.BlockSpec((tm, tn), lambda i, j: (i, j)),
        out_shape=jax.ShapeDtypeStruct((n, nout), BF16),
        scratch_shapes=[pltpu.VMEM((tm, d), BF16)],
        compiler_params=_params(("parallel", "arbitrary")),
        name="norm_matmul",
    )(x, w)


def _swap_halves(t):
    return jnp.concatenate([t[:, 64:], t[:, :64]], axis=1)


def _swa_kernel(sink_ref, q_ref, kvc_ref, kvp_ref, o_ref, kv_buf, *, tq):
    t_idx = pl.program_id(1)
    kv_buf[0:WINDOW, :] = kvp_ref[...]
    kv_buf[WINDOW:, :] = kvc_ref[...]

    row = lax.broadcasted_iota(jnp.int32, (WINDOW, 2 * WINDOW), 0)
    col = lax.broadcasted_iota(jnp.int32, (WINDOW, 2 * WINDOW), 1)
    band = (col > row) & (col <= row + WINDOW)
    lane = lax.broadcasted_iota(jnp.int32, (2 * WINDOW, LANES), 1)
    lo = lane < 64
    out_lane = lax.broadcasted_iota(jnp.int32, (WINDOW, LANES), 1)

    def blk(i, c):
        r0 = pl.multiple_of(i * WINDOW, WINDOW)
        cmin = jnp.where((t_idx == 0) & (i == 0), WINDOW, 0)
        valid = band & (col >= cmin)
        kcats, vcats = [], []
        for m in range(2):
            kt = kv_buf[pl.ds(r0, 2 * WINDOW), m * LANES:(m + 1) * LANES]
            vt = kv_buf[pl.ds(r0, 2 * WINDOW), 256 + m * LANES:256 + (m + 1) * LANES]
            ks, vs = _swap_halves(kt), _swap_halves(vt)
            zero = jnp.zeros_like(kt)
            for e in range(2):
                k_lo = jnp.where(lo, kt if e == 0 else ks, zero)
                k_hi = jnp.where(lo, zero, ks if e == 0 else kt)
                v_lo = jnp.where(lo, vt if e == 0 else vs, zero)
                v_hi = jnp.where(lo, zero, vs if e == 0 else vt)
                kcats.append(jnp.concatenate([k_lo, k_hi], axis=0))
                vcats.append(jnp.concatenate([v_lo, v_hi], axis=0))

        def score(j):
            qt = jnp.concatenate([q_ref[pl.ds(r0, WINDOW), p * LANES:(p + 1) * LANES]
                                  for p in (2 * j, 2 * j + 1)], axis=0)
            return lax.dot_general(qt, kcats[j], (((1,), (1,)), ((), ())),
                                   preferred_element_type=F32)

        def attend(j, s):
            pcats, scales = [], []
            for t in range(2):
                p = 2 * j + t
                probs, invs = [], []
                for hh in range(2):
                    sink = sink_ref[2 * p + hh] * LOG2E
                    sh = jnp.where(valid, s[t * WINDOW:(t + 1) * WINDOW, hh * 256:(hh + 1) * 256], -jnp.inf)
                    mx = jnp.maximum(jnp.max(sh, axis=-1, keepdims=True), sink)
                    pe = jnp.exp2(sh - mx)
                    den = jnp.sum(pe, axis=-1, keepdims=True) + jnp.exp2(sink - mx)
                    probs.append(pe.astype(BF16))
                    invs.append(1.0 / den)
                pcats.append(jnp.concatenate(probs, axis=1))
                scales.append(jnp.where(out_lane < 64, invs[0], invs[1]))
            o = jnp.dot(jnp.concatenate(pcats, axis=0), vcats[j], preferred_element_type=F32)
            for t in range(2):
                p = 2 * j + t
                o_ref[pl.ds(r0, WINDOW), p * LANES:(p + 1) * LANES] = (
                    o[t * WINDOW:(t + 1) * WINDOW] * scales[t]).astype(o_ref.dtype)

        s_next = score(0)
        for j in range(ATTN_KV_HEADS):
            s_cur = s_next
            if j + 1 < ATTN_KV_HEADS:
                s_next = score(j + 1)
            attend(j, s_cur)
        return c

    lax.fori_loop(0, _exact_div(tq, WINDOW), blk, 0)


def _swa(proj, sinks_l, *, batch, seq, tq):
    n = proj.shape[0]
    nt = _exact_div(seq, tq)
    qw = ATTN_Q_HEADS * ATTN_HEAD_DIM
    kvw = 2 * ATTN_KV_HEADS * ATTN_HEAD_DIM
    kv_blk = _exact_div(COL_KV, kvw)
    bpt = _exact_div(tq, WINDOW)

    def prev_map(b, t):
        return (jnp.maximum(b * _exact_div(seq, WINDOW) + t * bpt - 1, 0), kv_blk)

    return pl.pallas_call(
        functools.partial(_swa_kernel, tq=tq),
        grid=(batch, nt),
        in_specs=[pl.BlockSpec(memory_space=pltpu.SMEM),
                  pl.BlockSpec((tq, qw), lambda b, t: (b * nt + t, _exact_div(COL_QA, qw))),
                  pl.BlockSpec((tq, kvw), lambda b, t: (b * nt + t, kv_blk)),
                  pl.BlockSpec((WINDOW, kvw), prev_map)],
        out_specs=pl.BlockSpec((tq, qw), lambda b, t: (b * nt + t, 0)),
        out_shape=jax.ShapeDtypeStruct((n, qw), BF16),
        scratch_shapes=[pltpu.VMEM((tq + WINDOW, kvw), BF16)],
        compiler_params=_params(("parallel", "parallel")),
        name="swa",
    )(sinks_l, proj, proj, proj)


def _sgu_kernel(uv_ref, lng_ref, lnb_ref, w_ref, bt_ref, o_ref, *, ts):
    width = SGU_GROUPS * LANES
    row = lax.broadcasted_iota(jnp.int32, (SGU_CHUNK, SGU_CHUNK), 0)
    col = lax.broadcasted_iota(jnp.int32, (SGU_CHUNK, SGU_CHUNK), 1)
    tril = row >= col
    lng = lng_ref[...]
    lnb = lnb_ref[...]
    w_tril = [jnp.where(tril, w_ref[g], 0.0).astype(BF16) for g in range(SGU_GROUPS)]

    def chunk(c, carry):
        r = pl.ds(pl.multiple_of(c * SGU_CHUNK, SGU_CHUNK), SGU_CHUNK)
        v = jax.nn.gelu(uv_ref[r, width:].astype(F32))
        mu = jnp.mean(v, axis=-1, keepdims=True)
        vc = v - mu
        var = jnp.mean(vc * vc, axis=-1, keepdims=True)
        vn = (vc * lax.rsqrt(var + EPS) * lng + lnb).astype(BF16)
        for g in range(SGU_GROUPS):
            cs = slice(g * LANES, (g + 1) * LANES)
            mixed = jnp.dot(w_tril[g], vn[:, cs], preferred_element_type=F32) + bt_ref[:, g:g + 1]
            u = jax.nn.gelu(uv_ref[r, cs].astype(F32))
            o_ref[r, cs] = (u * mixed).astype(o_ref.dtype)
        return carry

    lax.fori_loop(0, _exact_div(ts, SGU_CHUNK), chunk, 0)


def _sgu(proj, ln_g, ln_b, w, bt, layer, *, ts):
    n = proj.shape[0]
    width = SGU_GROUPS * LANES
    assert COL_VB == COL_UB + width
    return pl.pallas_call(
        functools.partial(_sgu_kernel, ts=ts),
        grid=(_exact_div(n, ts),),
        in_specs=[pl.BlockSpec((ts, 2 * width), lambda i: (i, _exact_div(COL_UB, 2 * width))),
                  pl.BlockSpec((None, 1, width), lambda i: (layer, 0, 0)),
                  pl.BlockSpec((None, 1, width), lambda i: (layer, 0, 0)),
                  pl.BlockSpec((None, SGU_GROUPS, SGU_CHUNK, SGU_CHUNK), lambda i: (layer, 0, 0, 0)),
                  pl.BlockSpec((None, SGU_CHUNK, SGU_GROUPS), lambda i: (layer, 0, 0))],
        out_specs=pl.BlockSpec((ts, width), lambda i: (i, 0)),
        out_shape=jax.ShapeDtypeStruct((n, width), BF16),
        compiler_params=_params(("parallel",)),
        name="sgu",
    )(proj, ln_g, ln_b, w, bt)


def _hgrn_kernel(x_ref, lb_ref, ng_ref, o_ref, state_ref, kqb_ref, a_ref, *, tt, hb):
    C, SUB = HGRN_CHUNK, HGRN_SUB
    width = hb * LANES
    col_f, col_i, col_q, col_g = (j * width for j in range(4))

    @pl.when(pl.program_id(2) == 0)
    def _():
        state_ref[...] = jnp.zeros_like(state_ref)

    ng = ng_ref[...]
    r64 = lax.broadcasted_iota(jnp.int32, (C, C), 0)
    c64 = lax.broadcasted_iota(jnp.int32, (C, C), 1)
    tril = jnp.where(r64 >= c64, 1.0, 0.0).astype(BF16)
    t_i = lax.broadcasted_iota(jnp.int32, (C, LANES), 0)
    l_i = lax.broadcasted_iota(jnp.int32, (C, LANES), 1)
    own_half = ((l_i // C) == ((t_i // SUB) % 2)) & ((l_i % C) <= t_i)
    valid0 = own_half & (t_i < 2 * SUB)
    valid1 = own_half & (t_i >= 2 * SUB)

    def gates(r, h):
        cs = slice(h * LANES, (h + 1) * LANES)
        lb = lb_ref[:, cs]
        half_span = 0.5 * (1.0 - lb)
        f = (lb + half_span) + half_span * jnp.tanh(0.5 * x_ref[r, col_f + h * LANES:col_f + (h + 1) * LANES].astype(F32))
        lf = jnp.log2(f)
        k = 1.0 - f
        x_q = x_ref[r, col_q + h * LANES:col_q + (h + 1) * LANES].astype(F32)
        qf = x_q * ((0.5 * HGRN_HEAD_DIM ** -0.5) + (0.5 * HGRN_HEAD_DIM ** -0.5) * jnp.tanh(0.5 * x_q))
        hi = lf.astype(BF16)
        lo = (lf - hi.astype(F32)).astype(BF16)
        b2 = jnp.dot(tril, jnp.concatenate([hi, lo], axis=1), preferred_element_type=F32)
        b = b2[:, :LANES] + b2[:, LANES:]
        return k, qf, b

    def scores(k, qf, b):
        refs = [jnp.zeros((1, LANES), F32)] + [b[SUB * i - 1:SUB * i, :] for i in range(1, _exact_div(C, SUB))]
        q_rel = jnp.concatenate(
            [(qf[SUB * i:SUB * (i + 1)] * jnp.exp2(b[SUB * i:SUB * (i + 1)] - rf)).astype(BF16)
             for i, rf in enumerate(refs)], axis=0)
        blocks = []
        for i, rf in enumerate(refs):
            rows = SUB * (i + 1)
            blocks.append((k[:rows] * jnp.exp2(rf - b[:rows])).astype(BF16))
            if rows < C:
                blocks.append(jnp.zeros((C - rows, LANES), BF16))
        kcat = jnp.concatenate(blocks, axis=0)
        sc = lax.dot_general(q_rel, kcat, (((1,), (1,)), ((), ())), preferred_element_type=F32)
        return jnp.where(valid0, sc[:, :LANES], jnp.where(valid1, sc[:, LANES:], 0.0))

    def outputs(r, h, k, qf, b, a):
        cs = slice(h * LANES, (h + 1) * LANES)
        v = x_ref[r, col_i + h * LANES:col_i + (h + 1) * LANES]
        b_last = b[C - 1:C, :]
        q_abs = (qf * jnp.exp2(b)).astype(BF16)
        vv = jnp.concatenate([v, v], axis=0)
        st = state_ref[h]
        o = (lax.dot_general(q_abs, st.astype(BF16), (((1,), (1,)), ((), ())), preferred_element_type=F32)
             + jnp.dot(a.astype(BF16), vv, preferred_element_type=F32))
        k_end = (k * jnp.exp2(b_last - b)).astype(BF16)
        state_ref[h] = st * jnp.exp2(b_last) + lax.dot_general(
            v, k_end, (((0,), (0,)), ((), ())), preferred_element_type=F32)
        on = o * lax.rsqrt(jnp.mean(o * o, axis=-1, keepdims=True) + EPS) * ng
        x_g = x_ref[r, col_g + h * LANES:col_g + (h + 1) * LANES].astype(F32)
        o_ref[r, cs] = (on * (x_g * (0.5 + 0.5 * jnp.tanh(0.5 * x_g)))).astype(o_ref.dtype)

    def rows(c):
        return pl.ds(pl.multiple_of(c * C, C), C)

    def stage_scores(r):
        kqb = [gates(r, h) for h in range(hb)]
        a = [scores(*t) for t in kqb]
        for h in range(hb):
            for j, val in enumerate(kqb[h]):
                kqb_ref[j, h] = val
            a_ref[h] = a[h].astype(BF16)

    n_chunks = _exact_div(tt, C)
    stage_scores(rows(0))

    def chunk(c, carry):
        staged = [(kqb_ref[0, h], kqb_ref[1, h], kqb_ref[2, h], a_ref[h]) for h in range(hb)]
        for h in range(hb):
            outputs(rows(c), h, *staged[h])
        stage_scores(rows(jnp.minimum(c + 1, n_chunks - 1)))
        return carry

    lax.fori_loop(0, n_chunks, chunk, 0)


def _hgrn(proj, lb_l, norm_g, layer, *, batch, seq, tt, hb):
    n = proj.shape[0]
    nt = _exact_div(seq, tt)
    width = HGRN_HEADS * HGRN_HEAD_DIM
    bw = hb * LANES

    assert hb == HGRN_HEADS and (COL_IC, COL_QC, COL_GC) == (COL_FC + bw, COL_FC + 2 * bw, COL_FC + 3 * bw)
    return pl.pallas_call(
        functools.partial(_hgrn_kernel, tt=tt, hb=hb),
        grid=(batch, _exact_div(HGRN_HEADS, hb), nt),
        in_specs=[pl.BlockSpec((tt, 4 * bw), lambda b, h, t: (b * nt + t, _exact_div(COL_FC, 4 * bw))),
                  pl.BlockSpec((1, bw), lambda b, h, t: (0, h)),
                  pl.BlockSpec((None, 1, LANES), lambda b, h, t: (layer, 0, 0))],
        out_specs=pl.BlockSpec((tt, bw), lambda b, h, t: (b * nt + t, h)),
        out_shape=jax.ShapeDtypeStruct((n, width), BF16),
        scratch_shapes=[pltpu.VMEM((hb, HGRN_HEAD_DIM, HGRN_HEAD_DIM), F32),
                        pltpu.VMEM((3, hb, HGRN_CHUNK, LANES), F32),
                        pltpu.VMEM((hb, HGRN_CHUNK, LANES), BF16)],
        compiler_params=_params(("parallel", "parallel", "arbitrary")),
        name="hgrn2",
    )(proj, lb_l, norm_g)


def _merge_kernel(x_ref, ya_ref, yb_ref, yc_ref, ga_ref, gb_ref, gc_ref,
                  wa_ref, wb_ref, wc_ref, wo_ref, o_ref, *, tk):
    d = o_ref.shape[1]
    ya, yb, yc = ya_ref[...], yb_ref[...], yc_ref[...]

    def merged(c):
        cs = slice(c * tk, (c + 1) * tk)
        m = (ga_ref[:, cs].astype(F32) * jnp.dot(ya, wa_ref[:, cs], preferred_element_type=F32)
             + gb_ref[:, cs].astype(F32) * jnp.dot(yb, wb_ref[:, cs], preferred_element_type=F32)
             + gc_ref[:, cs].astype(F32) * jnp.dot(yc, wc_ref[:, cs], preferred_element_type=F32))
        return m.astype(BF16)

    n_groups = _exact_div(d, tk)
    acc = x_ref[...]
    m_next = merged(0)
    for c in range(n_groups):
        m_cur = m_next
        if c + 1 < n_groups:
            m_next = merged(c + 1)
        acc = acc + jnp.dot(m_cur, wo_ref[c * tk:(c + 1) * tk, :], preferred_element_type=F32)
    o_ref[...] = acc


def _merge(x, ya, yb, yc, proj, wa, wb, wc, wo, layer, *, tm, tk):
    n, d = x.shape
    wbr = ya.shape[1]
    g0 = _exact_div(COL_GATE, d)

    def gate(br):
        return pl.BlockSpec((tm, d), lambda i: (i, g0 + br))

    def resident(rows):
        return pl.BlockSpec((None, rows, d), lambda i: (layer, 0, 0), pipeline_mode=pl.Buffered(1))

    y_spec = pl.BlockSpec((tm, wbr), lambda i: (i, 0))
    return pl.pallas_call(
        functools.partial(_merge_kernel, tk=tk),
        grid=(_exact_div(n, tm),),
        in_specs=[pl.BlockSpec((tm, d), lambda i: (i, 0)),
                  y_spec, y_spec, y_spec, gate(0), gate(1), gate(2),
                  resident(wbr), resident(wbr), resident(wbr), resident(d)],
        out_specs=pl.BlockSpec((tm, d), lambda i: (i, 0)),
        out_shape=jax.ShapeDtypeStruct((n, d), F32),
        compiler_params=_params(("parallel",), vmem_limit=V7X_VMEM_LIMIT_MERGE),
        name="merge",
    )(x, ya, yb, yc, proj, proj, proj, wa, wb, wc, wo)


def _xattn_kernel(x_ref, wq_ref, kv_ref, wo_ref, o_ref, h_ref, *, tm):
    _rms_rows(x_ref, None, h_ref, tm)
    q = jnp.dot(h_ref[...], wq_ref[...], preferred_element_type=F32).astype(BF16)
    xw = X_HEADS * X_HEAD_DIM
    scale = X_HEAD_DIM ** -0.5 * LOG2E

    def score(hd):
        cs = slice(hd * X_HEAD_DIM, (hd + 1) * X_HEAD_DIM)
        return lax.dot_general(q[:, cs], kv_ref[:, cs], (((1,), (1,)), ((), ())),
                               preferred_element_type=F32) * scale

    def attend(hd, s):
        vh = kv_ref[:, xw + hd * X_HEAD_DIM:xw + (hd + 1) * X_HEAD_DIM]
        mx = jnp.max(s, axis=-1, keepdims=True)
        pe = jnp.exp2(s - mx)
        inv = 1.0 / jnp.sum(pe, axis=-1, keepdims=True)
        return (jnp.dot(pe.astype(BF16), vh, preferred_element_type=F32) * inv).astype(BF16)

    outs = []
    s_next = score(0)
    for hd in range(X_HEADS):
        s_cur = s_next
        if hd + 1 < X_HEADS:
            s_next = score(hd + 1)
        outs.append(attend(hd, s_cur))
    oc = jnp.concatenate(outs, axis=1)
    o_ref[...] = x_ref[...] + jnp.dot(oc, wo_ref[...], preferred_element_type=F32)


def _xattn(x, wq, kv, wo, layer, *, seq, mem_len, tm):
    n, d = x.shape
    xw = X_HEADS * X_HEAD_DIM
    tiles_per_seq = _exact_div(seq, tm)
    return pl.pallas_call(
        functools.partial(_xattn_kernel, tm=tm),
        grid=(_exact_div(n, tm),),
        in_specs=[pl.BlockSpec((tm, d), lambda i: (i, 0)),
                  pl.BlockSpec((None, d, xw), lambda i: (layer, 0, 0)),
                  pl.BlockSpec((mem_len, 2 * xw), lambda i: (i // tiles_per_seq, layer)),
                  pl.BlockSpec((None, xw, d), lambda i: (layer, 0, 0))],
        out_specs=pl.BlockSpec((tm, d), lambda i: (i, 0)),
        out_shape=jax.ShapeDtypeStruct((n, d), F32),
        scratch_shapes=[pltpu.VMEM((tm, d), BF16)],
        compiler_params=_params(("parallel",)),
        name="xattn",
    )(x, wq, kv, wo)


def _ffn_kernel(x_ref, fg_ref, wg_ref, wu_ref, wo_ref, o_ref, h_ref, *, tm, final_norm):
    def step(first):
        h = h_ref[...]
        gt = jnp.dot(h, wg_ref[...], preferred_element_type=F32)
        up = jnp.dot(h, wu_ref[...], preferred_element_type=F32)
        a = (jax.nn.silu(gt) * up).astype(BF16)
        down = jnp.dot(a, wo_ref[...], preferred_element_type=F32)
        o_ref[...] = (x_ref[...] if first else o_ref[...]) + down

    @pl.when(pl.program_id(1) == 0)
    def _():
        _rms_rows(x_ref, None, h_ref, tm)
        step(True)

    pl.when(pl.program_id(1) != 0)(functools.partial(step, False))

    if final_norm:
        @pl.when(pl.program_id(1) == pl.num_programs(1) - 1)
        def _():
            _rms_rows(o_ref, fg_ref, o_ref, tm)


def _ffn(x, final_g, w_in, w_out, layer, *, tm, tf, final_norm):
    n, d = x.shape
    nf = _exact_div(w_out.shape[1], tf)
    return pl.pallas_call(
        functools.partial(_ffn_kernel, tm=tm, final_norm=final_norm),
        grid=(_exact_div(n, tm), nf),
        in_specs=[pl.BlockSpec((tm, d), lambda i, f: (i, 0)),
                  pl.BlockSpec((1, d), lambda i, f: (0, 0)),
                  pl.BlockSpec((None, d, tf), lambda i, f: (layer, 0, f)),
                  pl.BlockSpec((None, d, tf), lambda i, f: (layer, 0, nf + f)),
                  pl.BlockSpec((None, tf, d), lambda i, f: (layer, f, 0))],
        out_specs=pl.BlockSpec((tm, d), lambda i, f: (i, 0)),
        out_shape=jax.ShapeDtypeStruct((n, d), F32),
        scratch_shapes=[pltpu.VMEM((tm, d), BF16)],
        compiler_params=_params(("parallel", "arbitrary")),
        name="ffn",
    )(x, final_g, w_in, w_in, w_out)


def kernel(x, mem, norm_mix, w_in, w_gate, sinks, sgu_ln_g, sgu_ln_b, sgu_w, sgu_b, hgrn_lb, hgrn_norm,
           w_br_a, w_br_b, w_br_c, w_out, norm_x, mem_norm, w_xq, w_xkv, w_xo, norm_ffn, w_ffn_in,
           w_ffn_out, final_norm):
    batch, seq, d = x.shape
    depth = w_in.shape[0]
    mem_len = mem.shape[1]
    n = batch * seq

    kv_end = (ATTN_Q_HEADS + 2 * ATTN_KV_HEADS) * ATTN_HEAD_DIM
    g_mix = norm_mix[:, :, None]
    w_all = jnp.concatenate(
        [(w * g_mix).astype(BF16)
         for w in (w_gate, w_in[..., kv_end:], w_in[..., :kv_end])], axis=-1)
    assert w_all.shape[-1] == COL_END
    wa, wb, wc, wo = (w.astype(BF16) for w in (w_br_a, w_br_b, w_br_c, w_out))
    wxq, wxo = (w_xq * norm_x[:, :, None]).astype(BF16), w_xo.astype(BF16)
    wxkv = jnp.concatenate([w_xkv[l] * mem_norm[:, None] for l in range(depth)], axis=-1).astype(BF16)
    wfi, wfo = (w_ffn_in * norm_ffn[:, :, None]).astype(BF16), w_ffn_out.astype(BF16)
    sgu_bt = jnp.swapaxes(sgu_b, 1, 2)
    sm = jax.nn.softmax(hgrn_lb.astype(F32), axis=0)
    lb_all = jnp.cumsum(sm, axis=0) - sm[0:1]
    sgu_ln_g, sgu_ln_b, hgrn_norm = (p[:, None, :] for p in (sgu_ln_g, sgu_ln_b, hgrn_norm))

    xf = x.reshape(n, d)
    kv_all = _norm_matmul(mem.reshape(batch * mem_len, d), wxkv, 0, tm=1024, tn=512)
    for l in range(depth):
        proj = _norm_matmul(xf, w_all, l, tm=1024, tn=1536, activate=True)
        ya = _swa(proj, sinks[l], batch=batch, seq=seq, tq=512)
        yb = _sgu(proj, sgu_ln_g, sgu_ln_b, sgu_w, sgu_bt, l, ts=1024)
        yc = _hgrn(proj, lb_all[l:l + 1], hgrn_norm, l, batch=batch, seq=seq, tt=seq, hb=8)
        xf = _merge(xf, ya, yb, yc, proj, wa, wb, wc, wo, l, tm=512, tk=256)
        xf = _xattn(xf, wxq, kv_all, wxo, l, seq=seq, mem_len=mem_len, tm=1024)
        xf = _ffn(xf, final_norm.reshape(1, d), wfi, wfo, l, tm=1024, tf=512,
                  final_norm=(l == depth - 1))
    return xf.reshape(batch, seq, d)
```

```python
import functools

import jax
import jax.numpy as jnp
from jax import lax
from jax.experimental import pallas as pl
from jax.experimental.pallas import tpu as pltpu

F32 = jnp.float32
BF16 = jnp.bfloat16
EPS = 1e-6

LANES = 128
V7X_VMEM_LIMIT = 56 * 1024 * 1024
V7X_VMEM_LIMIT_MERGE = 62 * 1024 * 1024

ATTN_HEAD_DIM = 64
ATTN_Q_HEADS = 16
ATTN_KV_HEADS = 4
WINDOW = 128
SGU_CHUNK = 128
SGU_GROUPS = 8
HGRN_HEADS = 8
HGRN_HEAD_DIM = 128
HGRN_CHUNK = 64
HGRN_SUB = 16
X_HEADS = 4
X_HEAD_DIM = 128

COL_GATE, COL_UB, COL_VB, COL_FC, COL_IC, COL_QC, COL_GC, COL_QA, COL_KV, COL_END = (
    0, 6144, 7168, 8192, 9216, 10240, 11264, 12288, 13312, 13824)
ACT_SUB = 512
LOG2E = 1.4426950408889634


def _exact_div(a, b):
    assert a % b == 0, (a, b)
    return a // b


def _params(sem, vmem_limit=V7X_VMEM_LIMIT):
    return pltpu.CompilerParams(dimension_semantics=sem, vmem_limit_bytes=vmem_limit)


def _rms_rows(x_ref, g_ref, h_ref, rows, copy_ref=None):
    slab = 64
    g = None if g_ref is None else g_ref[...]

    def body(i, c):
        r = pl.ds(pl.multiple_of(i * slab, slab), slab)
        x = x_ref[r, :]
        if copy_ref is not None:
            copy_ref[r, :] = x
        ms = jnp.mean(x * x, axis=-1, keepdims=True)
        y = x * lax.rsqrt(ms + EPS)
        h_ref[r, :] = (y if g is None else y * g).astype(h_ref.dtype)
        return c

    lax.fori_loop(0, _exact_div(rows, slab), body, 0, unroll=4)


def _in_range(c0, lo_hi):
    _exact_div(lo_hi[0], ACT_SUB), _exact_div(lo_hi[1], ACT_SUB)
    return (c0 >= lo_hi[0]) & (c0 < lo_hi[1])


def _projection_activation(x, c0, gate_tile):
    if gate_tile:
        return 0.5 + 0.5 * jnp.tanh(0.5 * x)
    return x * jnp.where(_in_range(c0, (COL_QA, COL_KV)), ATTN_HEAD_DIM ** -0.5 * LOG2E, 1.0)


def _norm_matmul_kernel(x_ref, w_ref, o_ref, h_ref, *, activate, tm, tn):
    j = pl.program_id(1)

    @pl.when(j == 0)
    def _():
        _rms_rows(x_ref, None, h_ref, tm)

    if not activate:
        o_ref[...] = jnp.dot(h_ref[...], w_ref[...], preferred_element_type=F32).astype(o_ref.dtype)
    else:
        assert COL_GATE == 0
        n_gate_tiles = _exact_div(COL_UB, tn)

        def step(gate_tile):
            h = h_ref[...]
            for s in range(_exact_div(tn, ACT_SUB)):
                cs = slice(s * ACT_SUB, (s + 1) * ACT_SUB)
                acc = jnp.dot(h, w_ref[:, cs], preferred_element_type=F32)
                o_ref[:, cs] = _projection_activation(acc, j * tn + s * ACT_SUB, gate_tile).astype(o_ref.dtype)

        pl.when(j < n_gate_tiles)(functools.partial(step, True))
        pl.when(j >= n_gate_tiles)(functools.partial(step, False))


def _norm_matmul(x, w, layer, *, tm, tn, activate=False):
    n, d = x.shape
    nout = w.shape[-1]
    kern = functools.partial(_norm_matmul_kernel, activate=activate, tm=tm, tn=tn)
    if w.ndim == 3:
        w_spec = pl.BlockSpec((None, d, tn), lambda i, j: (layer, 0, j))
    else:
        w_spec = pl.BlockSpec((d, tn), lambda i, j: (0, j))
    return pl.pallas_call(
        kern,
        grid=(_exact_div(n, tm), _exact_div(nout, tn)),
        in_specs=[pl.BlockSpec((tm, d), lambda i, j: (i, 0)),
                  w_spec],
        out_specs=pl.BlockSpec((tm, tn), lambda i, j: (i, j)),
        out_shape=jax.ShapeDtypeStruct((n, nout), BF16),
        scratch_shapes=[pltpu.VMEM((tm, d), BF16)],
        compiler_params=_params(("parallel", "arbitrary")),
        name="norm_matmul",
    )(x, w)


def _swap_halves(t):
    return jnp.concatenate([t[:, 64:], t[:, :64]], axis=1)


def _swa_kernel(sink_ref, q_ref, kvc_ref, kvp_ref, o_ref, kv_buf, *, tq):
    t_idx = pl.program_id(1)
    kv_buf[0:WINDOW, :] = kvp_ref[...]
    kv_buf[WINDOW:, :] = kvc_ref[...]

    row = lax.broadcasted_iota(jnp.int32, (WINDOW, 2 * WINDOW), 0)
    col = lax.broadcasted_iota(jnp.int32, (WINDOW, 2 * WINDOW), 1)
    band = (col > row) & (col <= row + WINDOW)
    lane = lax.broadcasted_iota(jnp.int32, (2 * WINDOW, LANES), 1)
    lo = lane < 64
    out_lane = lax.broadcasted_iota(jnp.int32, (WINDOW, LANES), 1)

    def blk(i, c):
        r0 = pl.multiple_of(i * WINDOW, WINDOW)
        cmin = jnp.where((t_idx == 0) & (i == 0), WINDOW, 0)
        valid = band & (col >= cmin)
        kcats, vcats = [], []
        for m in range(2):
            kt = kv_buf[pl.ds(r0, 2 * WINDOW), m * LANES:(m + 1) * LANES]
            vt = kv_buf[pl.ds(r0, 2 * WINDOW), 256 + m * LANES:256 + (m + 1) * LANES]
            ks, vs = _swap_halves(kt), _swap_halves(vt)
            zero = jnp.zeros_like(kt)
            for e in range(2):
                k_lo = jnp.where(lo, kt if e == 0 else ks, zero)
                k_hi = jnp.where(lo, zero, ks if e == 0 else kt)
                v_lo = jnp.where(lo, vt if e == 0 else vs, zero)
                v_hi = jnp.where(lo, zero, vs if e == 0 else vt)
                kcats.append(jnp.concatenate([k_lo, k_hi], axis=0))
                vcats.append(jnp.concatenate([v_lo, v_hi], axis=0))

        def score(j):
            qt = jnp.concatenate([q_ref[pl.ds(r0, WINDOW), p * LANES:(p + 1) * LANES]
                                  for p in (2 * j, 2 * j + 1)], axis=0)
            return lax.dot_general(qt, kcats[j], (((1,), (1,)), ((), ())),
                                   preferred_element_type=F32)

        def attend(j, s):
            pcats, scales = [], []
            for t in range(2):
                p = 2 * j + t
                probs, invs = [], []
                for hh in range(2):
                    sink = sink_ref[2 * p + hh] * LOG2E
                    sh = jnp.where(valid, s[t * WINDOW:(t + 1) * WINDOW, hh * 256:(hh + 1) * 256], -jnp.inf)
                    mx = jnp.maximum(jnp.max(sh, axis=-1, keepdims=True), sink)
                    pe = jnp.exp2(sh - mx)
                    den = jnp.sum(pe, axis=-1, keepdims=True) + jnp.exp2(sink - mx)
                    probs.append(pe.astype(BF16))
                    invs.append(1.0 / den)
                pcats.append(jnp.concatenate(probs, axis=1))
                scales.append(jnp.where(out_lane < 64, invs[0], invs[1]))
            o = jnp.dot(jnp.concatenate(pcats, axis=0), vcats[j], preferred_element_type=F32)
            for t in range(2):
                p = 2 * j + t
                o_ref[pl.ds(r0, WINDOW), p * LANES:(p + 1) * LANES] = (
                    o[t * WINDOW:(t + 1) * WINDOW] * scales[t]).astype(o_ref.dtype)

        s_next = score(0)
        for j in range(ATTN_KV_HEADS):
            s_cur = s_next
            if j + 1 < ATTN_KV_HEADS:
                s_next = score(j + 1)
            attend(j, s_cur)
        return c

    lax.fori_loop(0, _exact_div(tq, WINDOW), blk, 0)


def _swa(proj, sinks_l, *, batch, seq, tq):
    n = proj.shape[0]
    nt = _exact_div(seq, tq)
    qw = ATTN_Q_HEADS * ATTN_HEAD_DIM
    kvw = 2 * ATTN_KV_HEADS * ATTN_HEAD_DIM
    kv_blk = _exact_div(COL_KV, kvw)
    bpt = _exact_div(tq, WINDOW)

    def prev_map(b, t):
        return (jnp.maximum(b * _exact_div(seq, WINDOW) + t * bpt - 1, 0), kv_blk)

    return pl.pallas_call(
        functools.partial(_swa_kernel, tq=tq),
        grid=(batch, nt),
        in_specs=[pl.BlockSpec(memory_space=pltpu.SMEM),
                  pl.BlockSpec((tq, qw), lambda b, t: (b * nt + t, _exact_div(COL_QA, qw))),
                  pl.BlockSpec((tq, kvw), lambda b, t: (b * nt + t, kv_blk)),
                  pl.BlockSpec((WINDOW, kvw), prev_map)],
        out_specs=pl.BlockSpec((tq, qw), lambda b, t: (b * nt + t, 0)),
        out_shape=jax.ShapeDtypeStruct((n, qw), BF16),
        scratch_shapes=[pltpu.VMEM((tq + WINDOW, kvw), BF16)],
        compiler_params=_params(("parallel", "parallel")),
        name="swa",
    )(sinks_l, proj, proj, proj)


GELU_C = 0.7978845608028654
GELU_A = 0.044715


def _gelu_tanh(x):
    return x * (0.5 + 0.5 * jnp.tanh(x * (GELU_C + (GELU_C * GELU_A) * (x * x))))


def _sgu_kernel(uv_ref, lng_ref, lnb_ref, w_ref, bt_ref, o_ref, *, ts):
    width = SGU_GROUPS * LANES
    row = lax.broadcasted_iota(jnp.int32, (SGU_CHUNK, SGU_CHUNK), 0)
    col = lax.broadcasted_iota(jnp.int32, (SGU_CHUNK, SGU_CHUNK), 1)
    tril = row >= col
    lng = lng_ref[...]
    lnb = lnb_ref[...]
    w_tril = [jnp.where(tril, w_ref[g], 0.0).astype(BF16) for g in range(SGU_GROUPS)]

    def chunk(c, carry):
        r = pl.ds(pl.multiple_of(c * SGU_CHUNK, SGU_CHUNK), SGU_CHUNK)
        v = _gelu_tanh(uv_ref[r, width:].astype(F32))
        mu = jnp.mean(v, axis=-1, keepdims=True)
        vc = v - mu
        var = jnp.mean(vc * vc, axis=-1, keepdims=True)
        vn = (vc * lax.rsqrt(var + EPS) * lng + lnb).astype(BF16)
        for g in range(SGU_GROUPS):
            cs = slice(g * LANES, (g + 1) * LANES)
            mixed = jnp.dot(w_tril[g], vn[:, cs], preferred_element_type=F32) + bt_ref[:, g:g + 1]
            u = _gelu_tanh(uv_ref[r, cs].astype(F32))
            o_ref[r, cs] = (u * mixed).astype(o_ref.dtype)
        return carry

    lax.fori_loop(0, _exact_div(ts, SGU_CHUNK), chunk, 0)


def _sgu(proj, ln_g, ln_b, w, bt, layer, *, ts):
    n = proj.shape[0]
    width = SGU_GROUPS * LANES
    assert COL_VB == COL_UB + width
    return pl.pallas_call(
        functools.partial(_sgu_kernel, ts=ts),
        grid=(_exact_div(n, ts),),
        in_specs=[pl.BlockSpec((ts, 2 * width), lambda i: (i, _exact_div(COL_UB, 2 * width))),
                  pl.BlockSpec((None, 1, width), lambda i: (layer, 0, 0)),
                  pl.BlockSpec((None, 1, width), lambda i: (layer, 0, 0)),
                  pl.BlockSpec((None, SGU_GROUPS, SGU_CHUNK, SGU_CHUNK), lambda i: (layer, 0, 0, 0)),
                  pl.BlockSpec((None, SGU_CHUNK, SGU_GROUPS), lambda i: (layer, 0, 0))],
        out_specs=pl.BlockSpec((ts, width), lambda i: (i, 0)),
        out_shape=jax.ShapeDtypeStruct((n, width), BF16),
        compiler_params=_params(("parallel",)),
        name="sgu",
    )(proj, ln_g, ln_b, w, bt)


def _hgrn_kernel(x_ref, lb_ref, ng_ref, o_ref, state_ref, kb_ref, aq_ref, *, tt, hb):
    C, SUB = HGRN_CHUNK, HGRN_SUB
    width = hb * LANES
    col_f, col_i, col_q, col_g = (j * width for j in range(4))

    @pl.when(pl.program_id(2) == 0)
    def _():
        state_ref[...] = jnp.zeros_like(state_ref)

    ng = ng_ref[...]
    r64 = lax.broadcasted_iota(jnp.int32, (C, C), 0)
    c64 = lax.broadcasted_iota(jnp.int32, (C, C), 1)
    tril = jnp.where(r64 >= c64, 1.0, 0.0).astype(BF16)
    t_i = lax.broadcasted_iota(jnp.int32, (C, LANES), 0)
    l_i = lax.broadcasted_iota(jnp.int32, (C, LANES), 1)
    own_half = ((l_i // C) == ((t_i // SUB) % 2)) & ((l_i % C) <= t_i)
    valid0 = own_half & (t_i < 2 * SUB)
    valid1 = own_half & (t_i >= 2 * SUB)

    def gates(r, h):
        cs = slice(h * LANES, (h + 1) * LANES)
        lb = lb_ref[:, cs]
        half_span = 0.5 * (1.0 - lb)
        f = (lb + half_span) + half_span * jnp.tanh(0.5 * x_ref[r, col_f + h * LANES:col_f + (h + 1) * LANES].astype(F32))
        lf = jnp.log2(f)
        k = 1.0 - f
        x_q = x_ref[r, col_q + h * LANES:col_q + (h + 1) * LANES].astype(F32)
        qf = x_q * ((0.5 * HGRN_HEAD_DIM ** -0.5) + (0.5 * HGRN_HEAD_DIM ** -0.5) * jnp.tanh(0.5 * x_q))
        hi = lf.astype(BF16)
        lo = (lf - hi.astype(F32)).astype(BF16)
        b2 = jnp.dot(tril, jnp.concatenate([hi, lo], axis=1), preferred_element_type=F32)
        b = b2[:, :LANES] + b2[:, LANES:]
        return k, qf, b

    def scores(k, qf, b):
        refs = [jnp.zeros((1, LANES), F32)] + [b[SUB * i - 1:SUB * i, :] for i in range(1, _exact_div(C, SUB))]
        rel = [qf[SUB * i:SUB * (i + 1)] * jnp.exp2(b[SUB * i:SUB * (i + 1)] - rf) for i, rf in enumerate(refs)]
        q_rel = jnp.concatenate([t.astype(BF16) for t in rel], axis=0)
        q_abs = jnp.concatenate(
            [(t if i == 0 else t * jnp.exp2(refs[i])).astype(BF16) for i, t in enumerate(rel)], axis=0)
        blocks = []
        for i, rf in enumerate(refs):
            rows = SUB * (i + 1)
            blocks.append((k[:rows] * jnp.exp2(rf - b[:rows])).astype(BF16))
            if rows < C:
                blocks.append(jnp.zeros((C - rows, LANES), BF16))
        kcat = jnp.concatenate(blocks, axis=0)
        sc = lax.dot_general(q_rel, kcat, (((1,), (1,)), ((), ())), preferred_element_type=F32)
        a = jnp.where(valid0, sc[:, :LANES], jnp.where(valid1, sc[:, LANES:], 0.0))
        return a.astype(BF16), q_abs

    def outputs(r, h, k, b, a, q_abs):
        cs = slice(h * LANES, (h + 1) * LANES)
        v = x_ref[r, col_i + h * LANES:col_i + (h + 1) * LANES]
        b_last = b[C - 1:C, :]
        vv = jnp.concatenate([v, v], axis=0)
        st = state_ref[h]
        o = (lax.dot_general(q_abs, st.astype(BF16), (((1,), (1,)), ((), ())), preferred_element_type=F32)
             + jnp.dot(a, vv, preferred_element_type=F32))
        k_end = (k * jnp.exp2(b_last - b)).astype(BF16)
        state_ref[h] = st * jnp.exp2(b_last) + lax.dot_general(
            v, k_end, (((0,), (0,)), ((), ())), preferred_element_type=F32)
        on = o * lax.rsqrt(jnp.mean(o * o, axis=-1, keepdims=True) + EPS) * ng
        x_g = x_ref[r, col_g + h * LANES:col_g + (h + 1) * LANES].astype(F32)
        o_ref[r, cs] = (on * (x_g * (0.5 + 0.5 * jnp.tanh(0.5 * x_g)))).astype(o_ref.dtype)

    def rows(c):
        return pl.ds(pl.multiple_of(c * C, C), C)

    def stage_scores(r):
        kqb = [gates(r, h) for h in range(hb)]
        aq = [scores(*t) for t in kqb]
        for h in range(hb):
            kb_ref[0, h] = kqb[h][0]
            kb_ref[1, h] = kqb[h][2]
            aq_ref[0, h], aq_ref[1, h] = aq[h]

    n_chunks = _exact_div(tt, C)
    stage_scores(rows(0))

    def chunk(c, carry):
        staged = [(kb_ref[0, h], kb_ref[1, h], aq_ref[0, h], aq_ref[1, h]) for h in range(hb)]
        for h in range(hb):
            outputs(rows(c), h, *staged[h])
        stage_scores(rows(jnp.minimum(c + 1, n_chunks - 1)))
        return carry

    lax.fori_loop(0, n_chunks, chunk, 0)


def _hgrn(proj, lb_l, norm_g, layer, *, batch, seq, tt, hb):
    n = proj.shape[0]
    nt = _exact_div(seq, tt)
    width = HGRN_HEADS * HGRN_HEAD_DIM
    bw = hb * LANES

    assert hb == HGRN_HEADS and (COL_IC, COL_QC, COL_GC) == (COL_FC + bw, COL_FC + 2 * bw, COL_FC + 3 * bw)
    return pl.pallas_call(
        functools.partial(_hgrn_kernel, tt=tt, hb=hb),
        grid=(batch, _exact_div(HGRN_HEADS, hb), nt),
        in_specs=[pl.BlockSpec((tt, 4 * bw), lambda b, h, t: (b * nt + t, _exact_div(COL_FC, 4 * bw))),
                  pl.BlockSpec((1, bw), lambda b, h, t: (0, h)),
                  pl.BlockSpec((None, 1, LANES), lambda b, h, t: (layer, 0, 0))],
        out_specs=pl.BlockSpec((tt, bw), lambda b, h, t: (b * nt + t, h)),
        out_shape=jax.ShapeDtypeStruct((n, width), BF16),
        scratch_shapes=[pltpu.VMEM((hb, HGRN_HEAD_DIM, HGRN_HEAD_DIM), F32),
                        pltpu.VMEM((2, hb, HGRN_CHUNK, LANES), F32),
                        pltpu.VMEM((2, hb, HGRN_CHUNK, LANES), BF16)],
        compiler_params=_params(("parallel", "parallel", "arbitrary")),
        name="hgrn2",
    )(proj, lb_l, norm_g)


def _merge_kernel(x_ref, ya_ref, yb_ref, yc_ref, ga_ref, gb_ref, gc_ref,
                  wa_ref, wb_ref, wc_ref, wo_ref, o_ref, *, tk):
    d = o_ref.shape[1]
    ya, yb, yc = ya_ref[...], yb_ref[...], yc_ref[...]

    def merged(c):
        cs = slice(c * tk, (c + 1) * tk)
        m = (ga_ref[:, cs].astype(F32) * jnp.dot(ya, wa_ref[:, cs], preferred_element_type=F32)
             + gb_ref[:, cs].astype(F32) * jnp.dot(yb, wb_ref[:, cs], preferred_element_type=F32)
             + gc_ref[:, cs].astype(F32) * jnp.dot(yc, wc_ref[:, cs], preferred_element_type=F32))
        return m.astype(BF16)

    n_groups = _exact_div(d, tk)
    acc = x_ref[...]
    m_next = merged(0)
    for c in range(n_groups):
        m_cur = m_next
        if c + 1 < n_groups:
            m_next = merged(c + 1)
        acc = acc + jnp.dot(m_cur, wo_ref[c * tk:(c + 1) * tk, :], preferred_element_type=F32)
    o_ref[...] = acc


def _merge(x, ya, yb, yc, proj, wa, wb, wc, wo, layer, *, tm, tk):
    n, d = x.shape
    wbr = ya.shape[1]
    g0 = _exact_div(COL_GATE, d)

    def gate(br):
        return pl.BlockSpec((tm, d), lambda i: (i, g0 + br))

    def resident(rows):
        return pl.BlockSpec((None, rows, d), lambda i: (layer, 0, 0), pipeline_mode=pl.Buffered(1))

    y_spec = pl.BlockSpec((tm, wbr), lambda i: (i, 0))
    return pl.pallas_call(
        functools.partial(_merge_kernel, tk=tk),
        grid=(_exact_div(n, tm),),
        in_specs=[pl.BlockSpec((tm, d), lambda i: (i, 0)),
                  y_spec, y_spec, y_spec, gate(0), gate(1), gate(2),
                  resident(wbr), resident(wbr), resident(wbr), resident(d)],
        out_specs=pl.BlockSpec((tm, d), lambda i: (i, 0)),
        out_shape=jax.ShapeDtypeStruct((n, d), F32),
        compiler_params=_params(("parallel",), vmem_limit=V7X_VMEM_LIMIT_MERGE),
        name="merge",
    )(x, ya, yb, yc, proj, proj, proj, wa, wb, wc, wo)


def _xattn_kernel(x_ref, wq_ref, kv_ref, wo_ref, o_ref, h_ref, *, tm):
    _rms_rows(x_ref, None, h_ref, tm)
    q = jnp.dot(h_ref[...], wq_ref[...], preferred_element_type=F32).astype(BF16)
    xw = X_HEADS * X_HEAD_DIM
    scale = X_HEAD_DIM ** -0.5 * LOG2E

    def score(hd):
        cs = slice(hd * X_HEAD_DIM, (hd + 1) * X_HEAD_DIM)
        return lax.dot_general(q[:, cs], kv_ref[:, cs], (((1,), (1,)), ((), ())),
                               preferred_element_type=F32) * scale

    def attend(hd, s):
        vh = kv_ref[:, xw + hd * X_HEAD_DIM:xw + (hd + 1) * X_HEAD_DIM]
        mx = jnp.max(s, axis=-1, keepdims=True)
        pe = jnp.exp2(s - mx)
        inv = 1.0 / jnp.sum(pe, axis=-1, keepdims=True)
        return (jnp.dot(pe.astype(BF16), vh, preferred_element_type=F32) * inv).astype(BF16)

    outs = []
    s_next = score(0)
    for hd in range(X_HEADS):
        s_cur = s_next
        if hd + 1 < X_HEADS:
            s_next = score(hd + 1)
        outs.append(attend(hd, s_cur))
    oc = jnp.concatenate(outs, axis=1)
    o_ref[...] = x_ref[...] + jnp.dot(oc, wo_ref[...], preferred_element_type=F32)


def _xattn(x, wq, kv, wo, layer, *, seq, mem_len, tm):
    n, d = x.shape
    xw = X_HEADS * X_HEAD_DIM
    tiles_per_seq = _exact_div(seq, tm)
    return pl.pallas_call(
        functools.partial(_xattn_kernel, tm=tm),
        grid=(_exact_div(n, tm),),
        in_specs=[pl.BlockSpec((tm, d), lambda i: (i, 0)),
                  pl.BlockSpec((None, d, xw), lambda i: (layer, 0, 0)),
                  pl.BlockSpec((mem_len, 2 * xw), lambda i: (i // tiles_per_seq, layer)),
                  pl.BlockSpec((None, xw, d), lambda i: (layer, 0, 0))],
        out_specs=pl.BlockSpec((tm, d), lambda i: (i, 0)),
        out_shape=jax.ShapeDtypeStruct((n, d), F32),
        scratch_shapes=[pltpu.VMEM((tm, d), BF16)],
        compiler_params=_params(("parallel",)),
        name="xattn",
    )(x, wq, kv, wo)


def _ffn_kernel(x_ref, fg_ref, wg_ref, wu_ref, wo_ref, o_ref, h_ref, *, tm, final_norm):
    def step(first):
        h = h_ref[...]
        gt = jnp.dot(h, wg_ref[...], preferred_element_type=F32)
        up = jnp.dot(h, wu_ref[...], preferred_element_type=F32)
        a = (jax.nn.silu(gt) * up).astype(BF16)
        down = jnp.dot(a, wo_ref[...], preferred_element_type=F32)
        o_ref[...] = (x_ref[...] if first else o_ref[...]) + down

    @pl.when(pl.program_id(1) == 0)
    def _():
        _rms_rows(x_ref, None, h_ref, tm)
        step(True)

    pl.when(pl.program_id(1) != 0)(functools.partial(step, False))

    if final_norm:
        @pl.when(pl.program_id(1) == pl.num_programs(1) - 1)
        def _():
            _rms_rows(o_ref, fg_ref, o_ref, tm)


def _ffn(x, final_g, w_in, w_out, layer, *, tm, tf, final_norm):
    n, d = x.shape
    nf = _exact_div(w_out.shape[1], tf)
    return pl.pallas_call(
        functools.partial(_ffn_kernel, tm=tm, final_norm=final_norm),
        grid=(_exact_div(n, tm), nf),
        in_specs=[pl.BlockSpec((tm, d), lambda i, f: (i, 0)),
                  pl.BlockSpec((1, d), lambda i, f: (0, 0)),
                  pl.BlockSpec((None, d, tf), lambda i, f: (layer, 0, f)),
                  pl.BlockSpec((None, d, tf), lambda i, f: (layer, 0, nf + f)),
                  pl.BlockSpec((None, tf, d), lambda i, f: (layer, f, 0))],
        out_specs=pl.BlockSpec((tm, d), lambda i, f: (i, 0)),
        out_shape=jax.ShapeDtypeStruct((n, d), F32),
        scratch_shapes=[pltpu.VMEM((tm, d), BF16)],
        compiler_params=_params(("parallel", "arbitrary")),
        name="ffn",
    )(x, final_g, w_in, w_in, w_out)


def kernel(x, mem, norm_mix, w_in, w_gate, sinks, sgu_ln_g, sgu_ln_b, sgu_w, sgu_b, hgrn_lb, hgrn_norm,
           w_br_a, w_br_b, w_br_c, w_out, norm_x, mem_norm, w_xq, w_xkv, w_xo, norm_ffn, w_ffn_in,
           w_ffn_out, final_norm):
    batch, seq, d = x.shape
    depth = w_in.shape[0]
    mem_len = mem.shape[1]
    n = batch * seq

    kv_end = (ATTN_Q_HEADS + 2 * ATTN_KV_HEADS) * ATTN_HEAD_DIM
    g_mix = norm_mix[:, :, None]
    w_all = jnp.concatenate(
        [(w * g_mix).astype(BF16)
         for w in (w_gate, w_in[..., kv_end:], w_in[..., :kv_end])], axis=-1)
    assert w_all.shape[-1] == COL_END
    wa, wb, wc, wo = (w.astype(BF16) for w in (w_br_a, w_br_b, w_br_c, w_out))
    wxq, wxo = (w_xq * norm_x[:, :, None]).astype(BF16), w_xo.astype(BF16)
    wxkv = jnp.concatenate([w_xkv[l] * mem_norm[:, None] for l in range(depth)], axis=-1).astype(BF16)
    wfi, wfo = (w_ffn_in * norm_ffn[:, :, None]).astype(BF16), w_ffn_out.astype(BF16)
    sgu_bt = jnp.swapaxes(sgu_b, 1, 2)
    sm = jax.nn.softmax(hgrn_lb.astype(F32), axis=0)
    lb_all = jnp.cumsum(sm, axis=0) - sm[0:1]
    sgu_ln_g, sgu_ln_b, hgrn_norm = (p[:, None, :] for p in (sgu_ln_g, sgu_ln_b, hgrn_norm))

    xf = x.reshape(n, d)
    kv_all = _norm_matmul(mem.reshape(batch * mem_len, d), wxkv, 0, tm=1024, tn=512)
    for l in range(depth):
        proj = _norm_matmul(xf, w_all, l, tm=1024, tn=1536, activate=True)
        ya = _swa(proj, sinks[l], batch=batch, seq=seq, tq=512)
        yb = _sgu(proj, sgu_ln_g, sgu_ln_b, sgu_w, sgu_bt, l, ts=1024)
        yc = _hgrn(proj, lb_all[l:l + 1], hgrn_norm, l, batch=batch, seq=seq, tt=seq, hb=8)
        xf = _merge(xf, ya, yb, yc, proj, wa, wb, wc, wo, l, tm=512, tk=256)
        xf = _xattn(xf, wxq, kv_all, wxo, l, seq=seq, mem_len=mem_len, tm=1024)
        xf = _ffn(xf, final_norm.reshape(1, d), wfi, wfo, l, tm=1024, tf=512,
                  final_norm=(l == depth - 1))
    return xf.reshape(batch, seq, d)
```

```python
import functools

import jax
import jax.numpy as jnp
from jax import lax
from jax.experimental import pallas as pl
from jax.experimental.pallas import tpu as pltpu

F32 = jnp.float32
BF16 = jnp.bfloat16
EPS = 1e-6

LANES = 128
V7X_VMEM_LIMIT = 56 * 1024 * 1024
V7X_VMEM_LIMIT_MERGE = 62 * 1024 * 1024

ATTN_HEAD_DIM = 64
ATTN_Q_HEADS = 16
ATTN_KV_HEADS = 4
WINDOW = 128
SGU_CHUNK = 128
SGU_GROUPS = 8
HGRN_HEADS = 8
HGRN_HEAD_DIM = 128
HGRN_CHUNK = 64
HGRN_SUB = 16
X_HEADS = 4
X_HEAD_DIM = 128

COL_GATE, COL_UB, COL_VB, COL_FC, COL_IC, COL_QC, COL_GC, COL_QA, COL_KV, COL_END = (
    0, 6144, 7168, 8192, 9216, 10240, 11264, 12288, 13312, 13824)
ACT_SUB = 512
FFN_TF = 512
LOG2E = 1.4426950408889634


def _exact_div(a, b):
    assert a % b == 0, (a, b)
    return a // b


def _params(sem, vmem_limit=V7X_VMEM_LIMIT):
    return pltpu.CompilerParams(dimension_semantics=sem, vmem_limit_bytes=vmem_limit)


def _rms_rows(x_ref, g_ref, h_ref, rows, copy_ref=None):
    slab = 64
    g = None if g_ref is None else g_ref[...]

    def body(i, c):
        r = pl.ds(pl.multiple_of(i * slab, slab), slab)
        x = x_ref[r, :]
        if copy_ref is not None:
            copy_ref[r, :] = x
        ms = jnp.mean(x * x, axis=-1, keepdims=True)
        y = x * lax.rsqrt(ms + EPS)
        h_ref[r, :] = (y if g is None else y * g).astype(h_ref.dtype)
        return c

    lax.fori_loop(0, _exact_div(rows, slab), body, 0, unroll=4)


def _in_range(c0, lo_hi):
    _exact_div(lo_hi[0], ACT_SUB), _exact_div(lo_hi[1], ACT_SUB)
    return (c0 >= lo_hi[0]) & (c0 < lo_hi[1])


def _projection_activation(x, c0, gate_tile):
    if gate_tile:
        return 0.5 + 0.5 * jnp.tanh(0.5 * x)
    return x * jnp.where(_in_range(c0, (COL_QA, COL_KV)), ATTN_HEAD_DIM ** -0.5 * LOG2E, 1.0)


def _norm_matmul_kernel(x_ref, w_ref, o_ref, h_ref, *, activate, tm, tn):
    j = pl.program_id(1)

    @pl.when(j == 0)
    def _():
        _rms_rows(x_ref, None, h_ref, tm)

    if not activate:
        o_ref[...] = jnp.dot(h_ref[...], w_ref[...], preferred_element_type=F32).astype(o_ref.dtype)
    else:
        assert COL_GATE == 0
        n_gate_tiles = _exact_div(COL_UB, tn)

        def step(gate_tile):
            h = h_ref[...]
            for s in range(_exact_div(tn, ACT_SUB)):
                cs = slice(s * ACT_SUB, (s + 1) * ACT_SUB)
                acc = jnp.dot(h, w_ref[:, cs], preferred_element_type=F32)
                o_ref[:, cs] = _projection_activation(acc, j * tn + s * ACT_SUB, gate_tile).astype(o_ref.dtype)

        pl.when(j < n_gate_tiles)(functools.partial(step, True))
        pl.when(j >= n_gate_tiles)(functools.partial(step, False))


def _norm_matmul(x, w, layer, *, tm, tn, activate=False):
    n, d = x.shape
    nout = w.shape[-1]
    kern = functools.partial(_norm_matmul_kernel, activate=activate, tm=tm, tn=tn)
    if w.ndim == 3:
        w_spec = pl.BlockSpec((None, d, tn), lambda i, j: (layer, 0, j))
    else:
        w_spec = pl.BlockSpec((d, tn), lambda i, j: (0, j))
    return pl.pallas_call(
        kern,
        grid=(_exact_div(n, tm), _exact_div(nout, tn)),
        in_specs=[pl.BlockSpec((tm, d), lambda i, j: (i, 0)),
                  w_spec],
        out_specs=pl.BlockSpec((tm, tn), lambda i, j: (i, j)),
        out_shape=jax.ShapeDtypeStruct((n, nout), BF16),
        scratch_shapes=[pltpu.VMEM((tm, d), BF16)],
        compiler_params=_params(("parallel", "arbitrary")),
        name="norm_matmul",
    )(x, w)


def _swap_halves(t):
    return jnp.concatenate([t[:, 64:], t[:, :64]], axis=1)


def _swa_kernel(sink_ref, q_ref, kvc_ref, kvp_ref, o_ref, kv_buf, *, tq):
    t_idx = pl.program_id(1)
    kv_buf[0:WINDOW, :] = kvp_ref[...]
    kv_buf[WINDOW:, :] = kvc_ref[...]

    row = lax.broadcasted_iota(jnp.int32, (WINDOW, 2 * WINDOW), 0)
    col = lax.broadcasted_iota(jnp.int32, (WINDOW, 2 * WINDOW), 1)
    band = (col > row) & (col <= row + WINDOW)
    lane = lax.broadcasted_iota(jnp.int32, (2 * WINDOW, LANES), 1)
    lo = lane < 64
    out_lane = lax.broadcasted_iota(jnp.int32, (WINDOW, LANES), 1)

    def blk(i, c):
        r0 = pl.multiple_of(i * WINDOW, WINDOW)
        cmin = jnp.where((t_idx == 0) & (i == 0), WINDOW, 0)
        valid = band & (col >= cmin)
        kcats, vcats = [], []
        for m in range(2):
            kt = kv_buf[pl.ds(r0, 2 * WINDOW), m * LANES:(m + 1) * LANES]
            vt = kv_buf[pl.ds(r0, 2 * WINDOW), 256 + m * LANES:256 + (m + 1) * LANES]
            ks, vs = _swap_halves(kt), _swap_halves(vt)
            zero = jnp.zeros_like(kt)
            for e in range(2):
                k_lo = jnp.where(lo, kt if e == 0 else ks, zero)
                k_hi = jnp.where(lo, zero, ks if e == 0 else kt)
                v_lo = jnp.where(lo, vt if e == 0 else vs, zero)
                v_hi = jnp.where(lo, zero, vs if e == 0 else vt)
                kcats.append(jnp.concatenate([k_lo, k_hi], axis=0))
                vcats.append(jnp.concatenate([v_lo, v_hi], axis=0))

        def score(j):
            qt = jnp.concatenate([q_ref[pl.ds(r0, WINDOW), p * LANES:(p + 1) * LANES]
                                  for p in (2 * j, 2 * j + 1)], axis=0)
            return lax.dot_general(qt, kcats[j], (((1,), (1,)), ((), ())),
                                   preferred_element_type=F32)

        def attend(j, s):
            pcats, scales = [], []
            for t in range(2):
                p = 2 * j + t
                probs, invs = [], []
                for hh in range(2):
                    sink = sink_ref[2 * p + hh] * LOG2E
                    sh = jnp.where(valid, s[t * WINDOW:(t + 1) * WINDOW, hh * 256:(hh + 1) * 256], -jnp.inf)
                    mx = jnp.maximum(jnp.max(sh, axis=-1, keepdims=True), sink)
                    pe = jnp.exp2(sh - mx)
                    den = jnp.sum(pe, axis=-1, keepdims=True) + jnp.exp2(sink - mx)
                    probs.append(pe.astype(BF16))
                    invs.append(1.0 / den)
                pcats.append(jnp.concatenate(probs, axis=1))
                scales.append(jnp.where(out_lane < 64, invs[0], invs[1]))
            o = jnp.dot(jnp.concatenate(pcats, axis=0), vcats[j], preferred_element_type=F32)
            for t in range(2):
                p = 2 * j + t
                o_ref[pl.ds(r0, WINDOW), p * LANES:(p + 1) * LANES] = (
                    o[t * WINDOW:(t + 1) * WINDOW] * scales[t]).astype(o_ref.dtype)

        s_next = score(0)
        for j in range(ATTN_KV_HEADS):
            s_cur = s_next
            if j + 1 < ATTN_KV_HEADS:
                s_next = score(j + 1)
            attend(j, s_cur)
        return c

    lax.fori_loop(0, _exact_div(tq, WINDOW), blk, 0)


def _swa(proj, sinks_l, *, batch, seq, tq):
    n = proj.shape[0]
    nt = _exact_div(seq, tq)
    qw = ATTN_Q_HEADS * ATTN_HEAD_DIM
    kvw = 2 * ATTN_KV_HEADS * ATTN_HEAD_DIM
    kv_blk = _exact_div(COL_KV, kvw)
    bpt = _exact_div(tq, WINDOW)

    def prev_map(b, t):
        return (jnp.maximum(b * _exact_div(seq, WINDOW) + t * bpt - 1, 0), kv_blk)

    return pl.pallas_call(
        functools.partial(_swa_kernel, tq=tq),
        grid=(batch, nt),
        in_specs=[pl.BlockSpec(memory_space=pltpu.SMEM),
                  pl.BlockSpec((tq, qw), lambda b, t: (b * nt + t, _exact_div(COL_QA, qw))),
                  pl.BlockSpec((tq, kvw), lambda b, t: (b * nt + t, kv_blk)),
                  pl.BlockSpec((WINDOW, kvw), prev_map)],
        out_specs=pl.BlockSpec((tq, qw), lambda b, t: (b * nt + t, 0)),
        out_shape=jax.ShapeDtypeStruct((n, qw), BF16),
        scratch_shapes=[pltpu.VMEM((tq + WINDOW, kvw), BF16)],
        compiler_params=_params(("parallel", "parallel")),
        name="swa",
    )(sinks_l, proj, proj, proj)


GELU_C = 0.7978845608028654
GELU_A = 0.044715


def _gelu_tanh(x):
    return x * (0.5 + 0.5 * jnp.tanh(x * (GELU_C + (GELU_C * GELU_A) * (x * x))))


def _sgu_kernel(uv_ref, lng_ref, lnb_ref, w_ref, bt_ref, o_ref, *, ts):
    width = SGU_GROUPS * LANES
    row = lax.broadcasted_iota(jnp.int32, (SGU_CHUNK, SGU_CHUNK), 0)
    col = lax.broadcasted_iota(jnp.int32, (SGU_CHUNK, SGU_CHUNK), 1)
    tril = row >= col
    lng = lng_ref[...]
    lnb = lnb_ref[...]
    w_tril = [jnp.where(tril, w_ref[g], 0.0).astype(BF16) for g in range(SGU_GROUPS)]

    def chunk(c, carry):
        r = pl.ds(pl.multiple_of(c * SGU_CHUNK, SGU_CHUNK), SGU_CHUNK)
        v = _gelu_tanh(uv_ref[r, width:].astype(F32))
        mu = jnp.mean(v, axis=-1, keepdims=True)
        vc = v - mu
        var = jnp.mean(vc * vc, axis=-1, keepdims=True)
        vn = (vc * lax.rsqrt(var + EPS) * lng + lnb).astype(BF16)
        for g in range(SGU_GROUPS):
            cs = slice(g * LANES, (g + 1) * LANES)
            mixed = jnp.dot(w_tril[g], vn[:, cs], preferred_element_type=F32) + bt_ref[:, g:g + 1]
            u = _gelu_tanh(uv_ref[r, cs].astype(F32))
            o_ref[r, cs] = (u * mixed).astype(o_ref.dtype)
        return carry

    lax.fori_loop(0, _exact_div(ts, SGU_CHUNK), chunk, 0)


def _sgu(proj, ln_g, ln_b, w, bt, layer, *, ts):
    n = proj.shape[0]
    width = SGU_GROUPS * LANES
    assert COL_VB == COL_UB + width
    return pl.pallas_call(
        functools.partial(_sgu_kernel, ts=ts),
        grid=(_exact_div(n, ts),),
        in_specs=[pl.BlockSpec((ts, 2 * width), lambda i: (i, _exact_div(COL_UB, 2 * width))),
                  pl.BlockSpec((None, 1, width), lambda i: (layer, 0, 0)),
                  pl.BlockSpec((None, 1, width), lambda i: (layer, 0, 0)),
                  pl.BlockSpec((None, SGU_GROUPS, SGU_CHUNK, SGU_CHUNK), lambda i: (layer, 0, 0, 0)),
                  pl.BlockSpec((None, SGU_CHUNK, SGU_GROUPS), lambda i: (layer, 0, 0))],
        out_specs=pl.BlockSpec((ts, width), lambda i: (i, 0)),
        out_shape=jax.ShapeDtypeStruct((n, width), BF16),
        compiler_params=_params(("parallel",)),
        name="sgu",
    )(proj, ln_g, ln_b, w, bt)


def _hgrn_kernel(x_ref, lb_ref, ng_ref, o_ref, state_ref, kb_ref, aq_ref, *, tt, hb):
    C, SUB = HGRN_CHUNK, HGRN_SUB
    width = hb * LANES
    col_f, col_i, col_q, col_g = (j * width for j in range(4))

    @pl.when(pl.program_id(2) == 0)
    def _():
        state_ref[...] = jnp.zeros_like(state_ref)

    ng = ng_ref[...]
    r64 = lax.broadcasted_iota(jnp.int32, (C, C), 0)
    c64 = lax.broadcasted_iota(jnp.int32, (C, C), 1)
    tril = jnp.where(r64 >= c64, 1.0, 0.0).astype(BF16)
    t_i = lax.broadcasted_iota(jnp.int32, (C, LANES), 0)
    l_i = lax.broadcasted_iota(jnp.int32, (C, LANES), 1)
    own_half = ((l_i // C) == ((t_i // SUB) % 2)) & ((l_i % C) <= t_i)
    valid0 = own_half & (t_i < 2 * SUB)
    valid1 = own_half & (t_i >= 2 * SUB)

    def gates(r, h):
        cs = slice(h * LANES, (h + 1) * LANES)
        lb = lb_ref[:, cs]
        half_span = 0.5 * (1.0 - lb)
        f = (lb + half_span) + half_span * jnp.tanh(0.5 * x_ref[r, col_f + h * LANES:col_f + (h + 1) * LANES].astype(F32))
        lf = jnp.log2(f)
        k = 1.0 - f
        x_q = x_ref[r, col_q + h * LANES:col_q + (h + 1) * LANES].astype(F32)
        qf = x_q * ((0.5 * HGRN_HEAD_DIM ** -0.5) + (0.5 * HGRN_HEAD_DIM ** -0.5) * jnp.tanh(0.5 * x_q))
        hi = lf.astype(BF16)
        lo = (lf - hi.astype(F32)).astype(BF16)
        b2 = jnp.dot(tril, jnp.concatenate([hi, lo], axis=1), preferred_element_type=F32)
        b = b2[:, :LANES] + b2[:, LANES:]
        return k, qf, b

    def scores(k, qf, b):
        refs = [jnp.zeros((1, LANES), F32)] + [b[SUB * i - 1:SUB * i, :] for i in range(1, _exact_div(C, SUB))]
        rel = [qf[SUB * i:SUB * (i + 1)] * jnp.exp2(b[SUB * i:SUB * (i + 1)] - rf) for i, rf in enumerate(refs)]
        q_rel = jnp.concatenate([t.astype(BF16) for t in rel], axis=0)
        q_abs = jnp.concatenate(
            [(t if i == 0 else t * jnp.exp2(refs[i])).astype(BF16) for i, t in enumerate(rel)], axis=0)
        blocks = []
        for i, rf in enumerate(refs):
            rows = SUB * (i + 1)
            blocks.append((k[:rows] * jnp.exp2(rf - b[:rows])).astype(BF16))
            if rows < C:
                blocks.append(jnp.zeros((C - rows, LANES), BF16))
        kcat = jnp.concatenate(blocks, axis=0)
        sc = lax.dot_general(q_rel, kcat, (((1,), (1,)), ((), ())), preferred_element_type=F32)
        a = jnp.where(valid0, sc[:, :LANES], jnp.where(valid1, sc[:, LANES:], 0.0))
        return a.astype(BF16), q_abs

    def outputs(r, h, k, b, a, q_abs):
        cs = slice(h * LANES, (h + 1) * LANES)
        v = x_ref[r, col_i + h * LANES:col_i + (h + 1) * LANES]
        b_last = b[C - 1:C, :]
        vv = jnp.concatenate([v, v], axis=0)
        st = state_ref[h]
        o = (lax.dot_general(q_abs, st.astype(BF16), (((1,), (1,)), ((), ())), preferred_element_type=F32)
             + jnp.dot(a, vv, preferred_element_type=F32))
        k_end = (k * jnp.exp2(b_last - b)).astype(BF16)
        state_ref[h] = st * jnp.exp2(b_last) + lax.dot_general(
            v, k_end, (((0,), (0,)), ((), ())), preferred_element_type=F32)
        on = o * lax.rsqrt(jnp.mean(o * o, axis=-1, keepdims=True) + EPS) * ng
        x_g = x_ref[r, col_g + h * LANES:col_g + (h + 1) * LANES].astype(F32)
        o_ref[r, cs] = (on * (x_g * (0.5 + 0.5 * jnp.tanh(0.5 * x_g)))).astype(o_ref.dtype)

    def rows(c):
        return pl.ds(pl.multiple_of(c * C, C), C)

    def stage_scores(r):
        kqb = [gates(r, h) for h in range(hb)]
        aq = [scores(*t) for t in kqb]
        for h in range(hb):
            kb_ref[0, h] = kqb[h][0]
            kb_ref[1, h] = kqb[h][2]
            aq_ref[0, h], aq_ref[1, h] = aq[h]

    n_chunks = _exact_div(tt, C)
    stage_scores(rows(0))

    def chunk(c, carry):
        staged = [(kb_ref[0, h], kb_ref[1, h], aq_ref[0, h], aq_ref[1, h]) for h in range(hb)]
        for h in range(hb):
            outputs(rows(c), h, *staged[h])
        stage_scores(rows(jnp.minimum(c + 1, n_chunks - 1)))
        return carry

    lax.fori_loop(0, n_chunks, chunk, 0)


def _hgrn(proj, lb_l, norm_g, layer, *, batch, seq, tt, hb):
    n = proj.shape[0]
    nt = _exact_div(seq, tt)
    width = HGRN_HEADS * HGRN_HEAD_DIM
    bw = hb * LANES

    assert hb == HGRN_HEADS and (COL_IC, COL_QC, COL_GC) == (COL_FC + bw, COL_FC + 2 * bw, COL_FC + 3 * bw)
    return pl.pallas_call(
        functools.partial(_hgrn_kernel, tt=tt, hb=hb),
        grid=(batch, _exact_div(HGRN_HEADS, hb), nt),
        in_specs=[pl.BlockSpec((tt, 4 * bw), lambda b, h, t: (b * nt + t, _exact_div(COL_FC, 4 * bw))),
                  pl.BlockSpec((1, bw), lambda b, h, t: (0, h)),
                  pl.BlockSpec((None, 1, LANES), lambda b, h, t: (layer, 0, 0))],
        out_specs=pl.BlockSpec((tt, bw), lambda b, h, t: (b * nt + t, h)),
        out_shape=jax.ShapeDtypeStruct((n, width), BF16),
        scratch_shapes=[pltpu.VMEM((hb, HGRN_HEAD_DIM, HGRN_HEAD_DIM), F32),
                        pltpu.VMEM((2, hb, HGRN_CHUNK, LANES), F32),
                        pltpu.VMEM((2, hb, HGRN_CHUNK, LANES), BF16)],
        compiler_params=_params(("parallel", "parallel", "arbitrary")),
        name="hgrn2",
    )(proj, lb_l, norm_g)


def _merge_kernel(x_ref, ya_ref, yb_ref, yc_ref, ga_ref, gb_ref, gc_ref,
                  wa_ref, wb_ref, wc_ref, wo_ref, o_ref, *, tk):
    d = o_ref.shape[1]
    ya, yb, yc = ya_ref[...], yb_ref[...], yc_ref[...]

    def merged(c):
        cs = slice(c * tk, (c + 1) * tk)
        m = (ga_ref[:, cs].astype(F32) * jnp.dot(ya, wa_ref[:, cs], preferred_element_type=F32)
             + gb_ref[:, cs].astype(F32) * jnp.dot(yb, wb_ref[:, cs], preferred_element_type=F32)
             + gc_ref[:, cs].astype(F32) * jnp.dot(yc, wc_ref[:, cs], preferred_element_type=F32))
        return m.astype(BF16)

    n_groups = _exact_div(d, tk)
    acc = x_ref[...]
    m_next = merged(0)
    for c in range(n_groups):
        m_cur = m_next
        if c + 1 < n_groups:
            m_next = merged(c + 1)
        acc = acc + jnp.dot(m_cur, wo_ref[c * tk:(c + 1) * tk, :], preferred_element_type=F32)
    o_ref[...] = acc


def _merge(x, ya, yb, yc, proj, wa, wb, wc, wo, layer, *, tm, tk):
    n, d = x.shape
    wbr = ya.shape[1]
    g0 = _exact_div(COL_GATE, d)

    def gate(br):
        return pl.BlockSpec((tm, d), lambda i: (i, g0 + br))

    def resident(rows):
        return pl.BlockSpec((None, rows, d), lambda i: (layer, 0, 0), pipeline_mode=pl.Buffered(1))

    y_spec = pl.BlockSpec((tm, wbr), lambda i: (i, 0))
    return pl.pallas_call(
        functools.partial(_merge_kernel, tk=tk),
        grid=(_exact_div(n, tm),),
        in_specs=[pl.BlockSpec((tm, d), lambda i: (i, 0)),
                  y_spec, y_spec, y_spec, gate(0), gate(1), gate(2),
                  resident(wbr), resident(wbr), resident(wbr), resident(d)],
        out_specs=pl.BlockSpec((tm, d), lambda i: (i, 0)),
        out_shape=jax.ShapeDtypeStruct((n, d), F32),
        compiler_params=_params(("parallel",), vmem_limit=V7X_VMEM_LIMIT_MERGE),
        name="merge",
    )(x, ya, yb, yc, proj, proj, proj, wa, wb, wc, wo)


def _xattn_kernel(x_ref, wq_ref, kv_ref, wo_ref, o_ref, h_ref, *, tm):
    _rms_rows(x_ref, None, h_ref, tm)
    q = jnp.dot(h_ref[...], wq_ref[...], preferred_element_type=F32).astype(BF16)
    xw = X_HEADS * X_HEAD_DIM
    scale = X_HEAD_DIM ** -0.5 * LOG2E

    def score(hd):
        cs = slice(hd * X_HEAD_DIM, (hd + 1) * X_HEAD_DIM)
        return lax.dot_general(q[:, cs], kv_ref[:, cs], (((1,), (1,)), ((), ())),
                               preferred_element_type=F32) * scale

    def attend(hd, s):
        vh = kv_ref[:, xw + hd * X_HEAD_DIM:xw + (hd + 1) * X_HEAD_DIM]
        mx = jnp.max(s, axis=-1, keepdims=True)
        pe = jnp.exp2(s - mx)
        inv = 1.0 / jnp.sum(pe, axis=-1, keepdims=True)
        return (jnp.dot(pe.astype(BF16), vh, preferred_element_type=F32) * inv).astype(BF16)

    outs = []
    s_next = score(0)
    for hd in range(X_HEADS):
        s_cur = s_next
        if hd + 1 < X_HEADS:
            s_next = score(hd + 1)
        outs.append(attend(hd, s_cur))
    oc = jnp.concatenate(outs, axis=1)
    o_ref[...] = x_ref[...] + jnp.dot(oc, wo_ref[...], preferred_element_type=F32)


def _xattn(x, wq, kv, wo, layer, *, seq, mem_len, tm):
    n, d = x.shape
    xw = X_HEADS * X_HEAD_DIM
    tiles_per_seq = _exact_div(seq, tm)
    return pl.pallas_call(
        functools.partial(_xattn_kernel, tm=tm),
        grid=(_exact_div(n, tm),),
        in_specs=[pl.BlockSpec((tm, d), lambda i: (i, 0)),
                  pl.BlockSpec((None, d, xw), lambda i: (layer, 0, 0)),
                  pl.BlockSpec((mem_len, 2 * xw), lambda i: (i // tiles_per_seq, layer)),
                  pl.BlockSpec((None, xw, d), lambda i: (layer, 0, 0))],
        out_specs=pl.BlockSpec((tm, d), lambda i: (i, 0)),
        out_shape=jax.ShapeDtypeStruct((n, d), F32),
        scratch_shapes=[pltpu.VMEM((tm, d), BF16)],
        compiler_params=_params(("parallel",)),
        name="xattn",
    )(x, wq, kv, wo)


def _ffn_kernel(x_ref, fg_ref, wgu_ref, wo_ref, o_ref, h_ref, *, tm, tf, final_norm):
    def step(first):
        h = h_ref[...]
        gu = jnp.dot(h, wgu_ref[...], preferred_element_type=F32)
        a = (jax.nn.silu(gu[:, :tf]) * gu[:, tf:]).astype(BF16)
        down = jnp.dot(a, wo_ref[...], preferred_element_type=F32)
        o_ref[...] = (x_ref[...] if first else o_ref[...]) + down

    @pl.when(pl.program_id(1) == 0)
    def _():
        _rms_rows(x_ref, None, h_ref, tm)
        step(True)

    pl.when(pl.program_id(1) != 0)(functools.partial(step, False))

    if final_norm:
        @pl.when(pl.program_id(1) == pl.num_programs(1) - 1)
        def _():
            _rms_rows(o_ref, fg_ref, o_ref, tm)


def _pair_gate_up(w_in, tf):
    depth, d, two_dff = w_in.shape
    nf = _exact_div(two_dff, 2 * tf)
    return jnp.swapaxes(w_in.reshape(depth, d, 2, nf, tf), 2, 3).reshape(depth, d, two_dff)


def _ffn(x, final_g, w_in_paired, w_out, layer, *, tm, tf, final_norm):
    n, d = x.shape
    nf = _exact_div(w_out.shape[1], tf)
    return pl.pallas_call(
        functools.partial(_ffn_kernel, tm=tm, tf=tf, final_norm=final_norm),
        grid=(_exact_div(n, tm), nf),
        in_specs=[pl.BlockSpec((tm, d), lambda i, f: (i, 0)),
                  pl.BlockSpec((1, d), lambda i, f: (0, 0)),
                  pl.BlockSpec((None, d, 2 * tf), lambda i, f: (layer, 0, f)),
                  pl.BlockSpec((None, tf, d), lambda i, f: (layer, f, 0))],
        out_specs=pl.BlockSpec((tm, d), lambda i, f: (i, 0)),
        out_shape=jax.ShapeDtypeStruct((n, d), F32),
        scratch_shapes=[pltpu.VMEM((tm, d), BF16)],
        compiler_params=_params(("parallel", "arbitrary")),
        name="ffn",
    )(x, final_g, w_in_paired, w_out)


def kernel(x, mem, norm_mix, w_in, w_gate, sinks, sgu_ln_g, sgu_ln_b, sgu_w, sgu_b, hgrn_lb, hgrn_norm,
           w_br_a, w_br_b, w_br_c, w_out, norm_x, mem_norm, w_xq, w_xkv, w_xo, norm_ffn, w_ffn_in,
           w_ffn_out, final_norm):
    batch, seq, d = x.shape
    depth = w_in.shape[0]
    mem_len = mem.shape[1]
    n = batch * seq

    kv_end = (ATTN_Q_HEADS + 2 * ATTN_KV_HEADS) * ATTN_HEAD_DIM
    g_mix = norm_mix[:, :, None]
    w_all = jnp.concatenate(
        [(w * g_mix).astype(BF16)
         for w in (w_gate, w_in[..., kv_end:], w_in[..., :kv_end])], axis=-1)
    assert w_all.shape[-1] == COL_END
    wa, wb, wc, wo = (w.astype(BF16) for w in (w_br_a, w_br_b, w_br_c, w_out))
    wxq, wxo = (w_xq * norm_x[:, :, None]).astype(BF16), w_xo.astype(BF16)
    wxkv = jnp.concatenate([w_xkv[l] * mem_norm[:, None] for l in range(depth)], axis=-1).astype(BF16)
    wfi = _pair_gate_up((w_ffn_in * norm_ffn[:, :, None]).astype(BF16), FFN_TF)
    wfo = w_ffn_out.astype(BF16)
    sgu_bt = jnp.swapaxes(sgu_b, 1, 2)
    sm = jax.nn.softmax(hgrn_lb.astype(F32), axis=0)
    lb_all = jnp.cumsum(sm, axis=0) - sm[0:1]
    sgu_ln_g, sgu_ln_b, hgrn_norm = (p[:, None, :] for p in (sgu_ln_g, sgu_ln_b, hgrn_norm))

    xf = x.reshape(n, d)
    kv_all = _norm_matmul(mem.reshape(batch * mem_len, d), wxkv, 0, tm=1024, tn=512)
    for l in range(depth):
        proj = _norm_matmul(xf, w_all, l, tm=1024, tn=1536, activate=True)
        ya = _swa(proj, sinks[l], batch=batch, seq=seq, tq=512)
        yb = _sgu(proj, sgu_ln_g, sgu_ln_b, sgu_w, sgu_bt, l, ts=1024)
        yc = _hgrn(proj, lb_all[l:l + 1], hgrn_norm, l, batch=batch, seq=seq, tt=seq, hb=8)
        xf = _merge(xf, ya, yb, yc, proj, wa, wb, wc, wo, l, tm=512, tk=256)
        xf = _xattn(xf, wxq, kv_all, wxo, l, seq=seq, mem_len=mem_len, tm=1024)
        xf = _ffn(xf, final_norm.reshape(1, d), wfi, wfo, l, tm=1024, tf=FFN_TF,
                  final_norm=(l == depth - 1))
    return xf.reshape(batch, seq, d)
```

```python
import functools

import jax
import jax.numpy as jnp
from jax import lax
from jax.experimental import pallas as pl
from jax.experimental.pallas import tpu as pltpu

F32 = jnp.float32
BF16 = jnp.bfloat16
EPS = 1e-6

LANES = 128
V7X_VMEM_LIMIT = 56 * 1024 * 1024
V7X_VMEM_LIMIT_MERGE = 62 * 1024 * 1024

ATTN_HEAD_DIM = 64
ATTN_Q_HEADS = 16
ATTN_KV_HEADS = 4
WINDOW = 128
SGU_CHUNK = 128
SGU_GROUPS = 8
HGRN_HEADS = 8
HGRN_HEAD_DIM = 128
HGRN_CHUNK = 64
HGRN_SUB = 16
X_HEADS = 4
X_HEAD_DIM = 128

COL_GATE, COL_UB, COL_VB, COL_FC, COL_IC, COL_QC, COL_GC, COL_QA, COL_KV, COL_END = (
    0, 6144, 7168, 8192, 9216, 10240, 11264, 12288, 13312, 13824)
ACT_SUB = 512
LOG2E = 1.4426950408889634


def _exact_div(a, b):
    assert a % b == 0, (a, b)
    return a // b


def _params(sem, vmem_limit=V7X_VMEM_LIMIT):
    return pltpu.CompilerParams(dimension_semantics=sem, vmem_limit_bytes=vmem_limit)


def _rms_rows(x_ref, g_ref, h_ref, rows, copy_ref=None):
    slab = 64
    g = None if g_ref is None else g_ref[...]

    def body(i, c):
        r = pl.ds(pl.multiple_of(i * slab, slab), slab)
        x = x_ref[r, :]
        if copy_ref is not None:
            copy_ref[r, :] = x
        ms = jnp.mean(x * x, axis=-1, keepdims=True)
        y = x * lax.rsqrt(ms + EPS)
        h_ref[r, :] = (y if g is None else y * g).astype(h_ref.dtype)
        return c

    lax.fori_loop(0, _exact_div(rows, slab), body, 0, unroll=4)


def _in_range(c0, lo_hi):
    _exact_div(lo_hi[0], ACT_SUB), _exact_div(lo_hi[1], ACT_SUB)
    return (c0 >= lo_hi[0]) & (c0 < lo_hi[1])


def _projection_activation(x, c0, gate_tile):
    if gate_tile:
        return 0.5 + 0.5 * jnp.tanh(0.5 * x)
    return x * jnp.where(_in_range(c0, (COL_QA, COL_KV)), ATTN_HEAD_DIM ** -0.5 * LOG2E, 1.0)


def _norm_matmul_kernel(x_ref, w_ref, o_ref, h_ref, *, activate, tm, tn):
    j = pl.program_id(1)

    @pl.when(j == 0)
    def _():
        _rms_rows(x_ref, None, h_ref, tm)

    if not activate:
        o_ref[...] = jnp.dot(h_ref[...], w_ref[...], preferred_element_type=F32).astype(o_ref.dtype)
    else:
        assert COL_GATE == 0
        n_gate_tiles = _exact_div(COL_UB, tn)

        def step(gate_tile):
            h = h_ref[...]
            for s in range(_exact_div(tn, ACT_SUB)):
                cs = slice(s * ACT_SUB, (s + 1) * ACT_SUB)
                acc = jnp.dot(h, w_ref[:, cs], preferred_element_type=F32)
                o_ref[:, cs] = _projection_activation(acc, j * tn + s * ACT_SUB, gate_tile).astype(o_ref.dtype)

        pl.when(j < n_gate_tiles)(functools.partial(step, True))
        pl.when(j >= n_gate_tiles)(functools.partial(step, False))


def _norm_matmul(x, w, layer, *, tm, tn, activate=False):
    n, d = x.shape
    nout = w.shape[-1]
    kern = functools.partial(_norm_matmul_kernel, activate=activate, tm=tm, tn=tn)
    if w.ndim == 3:
        w_spec = pl.BlockSpec((None, d, tn), lambda i, j: (layer, 0, j))
    else:
        w_spec = pl.BlockSpec((d, tn), lambda i, j: (0, j))
    return pl.pallas_call(
        kern,
        grid=(_exact_div(n, tm), _exact_div(nout, tn)),
        in_specs=[pl.BlockSpec((tm, d), lambda i, j: (i, 0)),
                  w_spec],
        out_specs=pl.BlockSpec((tm, tn), lambda i, j: (i, j)),
        out_shape=jax.ShapeDtypeStruct((n, nout), BF16),
        scratch_shapes=[pltpu.VMEM((tm, d), BF16)],
        compiler_params=_params(("parallel", "arbitrary")),
        name="norm_matmul",
    )(x, w)


def _swap_halves(t):
    return jnp.concatenate([t[:, 64:], t[:, :64]], axis=1)


def _swa_kernel(sink_ref, q_ref, kvc_ref, kvp_ref, o_ref, kv_buf, *, tq):
    t_idx = pl.program_id(1)
    kv_buf[0:WINDOW, :] = kvp_ref[...]
    kv_buf[WINDOW:, :] = kvc_ref[...]

    row = lax.broadcasted_iota(jnp.int32, (WINDOW, 2 * WINDOW), 0)
    col = lax.broadcasted_iota(jnp.int32, (WINDOW, 2 * WINDOW), 1)
    band = (col > row) & (col <= row + WINDOW)
    lane = lax.broadcasted_iota(jnp.int32, (2 * WINDOW, LANES), 1)
    lo = lane < 64
    out_lane = lax.broadcasted_iota(jnp.int32, (WINDOW, LANES), 1)

    def blk(i, c):
        r0 = pl.multiple_of(i * WINDOW, WINDOW)
        cmin = jnp.where((t_idx == 0) & (i == 0), WINDOW, 0)
        valid = band & (col >= cmin)
        kcats, vcats = [], []
        for m in range(2):
            kt = kv_buf[pl.ds(r0, 2 * WINDOW), m * LANES:(m + 1) * LANES]
            vt = kv_buf[pl.ds(r0, 2 * WINDOW), 256 + m * LANES:256 + (m + 1) * LANES]
            ks, vs = _swap_halves(kt), _swap_halves(vt)
            zero = jnp.zeros_like(kt)
            for e in range(2):
                k_lo = jnp.where(lo, kt if e == 0 else ks, zero)
                k_hi = jnp.where(lo, zero, ks if e == 0 else kt)
                v_lo = jnp.where(lo, vt if e == 0 else vs, zero)
                v_hi = jnp.where(lo, zero, vs if e == 0 else vt)
                kcats.append(jnp.concatenate([k_lo, k_hi], axis=0))
                vcats.append(jnp.concatenate([v_lo, v_hi], axis=0))

        def score(j):
            qt = jnp.concatenate([q_ref[pl.ds(r0, WINDOW), p * LANES:(p + 1) * LANES]
                                  for p in (2 * j, 2 * j + 1)], axis=0)
            return lax.dot_general(qt, kcats[j], (((1,), (1,)), ((), ())),
                                   preferred_element_type=F32)

        def attend(j, s):
            pcats, scales = [], []
            for t in range(2):
                p = 2 * j + t
                probs, invs = [], []
                for hh in range(2):
                    sink = sink_ref[2 * p + hh] * LOG2E
                    sh = jnp.where(valid, s[t * WINDOW:(t + 1) * WINDOW, hh * 256:(hh + 1) * 256], -jnp.inf)
                    mx = jnp.maximum(jnp.max(sh, axis=-1, keepdims=True), sink)
                    pe = jnp.exp2(sh - mx)
                    den = jnp.sum(pe, axis=-1, keepdims=True) + jnp.exp2(sink - mx)
                    probs.append(pe.astype(BF16))
                    invs.append(1.0 / den)
                pcats.append(jnp.concatenate(probs, axis=1))
                scales.append(jnp.where(out_lane < 64, invs[0], invs[1]))
            o = jnp.dot(jnp.concatenate(pcats, axis=0), vcats[j], preferred_element_type=F32)
            for t in range(2):
                p = 2 * j + t
                o_ref[pl.ds(r0, WINDOW), p * LANES:(p + 1) * LANES] = (
                    o[t * WINDOW:(t + 1) * WINDOW] * scales[t]).astype(o_ref.dtype)

        s_next = score(0)
        for j in range(ATTN_KV_HEADS):
            s_cur = s_next
            if j + 1 < ATTN_KV_HEADS:
                s_next = score(j + 1)
            attend(j, s_cur)
        return c

    lax.fori_loop(0, _exact_div(tq, WINDOW), blk, 0)


def _swa(proj, sinks_l, *, batch, seq, tq):
    n = proj.shape[0]
    nt = _exact_div(seq, tq)
    qw = ATTN_Q_HEADS * ATTN_HEAD_DIM
    kvw = 2 * ATTN_KV_HEADS * ATTN_HEAD_DIM
    kv_blk = _exact_div(COL_KV, kvw)
    bpt = _exact_div(tq, WINDOW)

    def prev_map(b, t):
        return (jnp.maximum(b * _exact_div(seq, WINDOW) + t * bpt - 1, 0), kv_blk)

    return pl.pallas_call(
        functools.partial(_swa_kernel, tq=tq),
        grid=(batch, nt),
        in_specs=[pl.BlockSpec(memory_space=pltpu.SMEM),
                  pl.BlockSpec((tq, qw), lambda b, t: (b * nt + t, _exact_div(COL_QA, qw))),
                  pl.BlockSpec((tq, kvw), lambda b, t: (b * nt + t, kv_blk)),
                  pl.BlockSpec((WINDOW, kvw), prev_map)],
        out_specs=pl.BlockSpec((tq, qw), lambda b, t: (b * nt + t, 0)),
        out_shape=jax.ShapeDtypeStruct((n, qw), BF16),
        scratch_shapes=[pltpu.VMEM((tq + WINDOW, kvw), BF16)],
        compiler_params=_params(("parallel", "parallel")),
        name="swa",
    )(sinks_l, proj, proj, proj)


GELU_C = 0.7978845608028654
GELU_A = 0.044715


def _gelu_tanh(x):
    return x * (0.5 + 0.5 * jnp.tanh(x * (GELU_C + (GELU_C * GELU_A) * (x * x))))


def _sgu_kernel(uv_ref, lng_ref, lnb_ref, w_ref, bt_ref, o_ref, *, ts):
    width = SGU_GROUPS * LANES
    row = lax.broadcasted_iota(jnp.int32, (SGU_CHUNK, SGU_CHUNK), 0)
    col = lax.broadcasted_iota(jnp.int32, (SGU_CHUNK, SGU_CHUNK), 1)
    tril = row >= col
    lng = lng_ref[...]
    lnb = lnb_ref[...]
    w_tril = [jnp.where(tril, w_ref[g], 0.0).astype(BF16) for g in range(SGU_GROUPS)]

    def chunk(c, carry):
        r = pl.ds(pl.multiple_of(c * SGU_CHUNK, SGU_CHUNK), SGU_CHUNK)
        v = _gelu_tanh(uv_ref[r, width:].astype(F32))
        mu = jnp.mean(v, axis=-1, keepdims=True)
        vc = v - mu
        var = jnp.mean(vc * vc, axis=-1, keepdims=True)
        vn = (vc * lax.rsqrt(var + EPS) * lng + lnb).astype(BF16)
        for g in range(SGU_GROUPS):
            cs = slice(g * LANES, (g + 1) * LANES)
            mixed = jnp.dot(w_tril[g], vn[:, cs], preferred_element_type=F32) + bt_ref[:, g:g + 1]
            u = _gelu_tanh(uv_ref[r, cs].astype(F32))
            o_ref[r, cs] = (u * mixed).astype(o_ref.dtype)
        return carry

    lax.fori_loop(0, _exact_div(ts, SGU_CHUNK), chunk, 0, unroll=2)


def _sgu(proj, ln_g, ln_b, w, bt, layer, *, ts):
    n = proj.shape[0]
    width = SGU_GROUPS * LANES
    assert COL_VB == COL_UB + width
    return pl.pallas_call(
        functools.partial(_sgu_kernel, ts=ts),
        grid=(_exact_div(n, ts),),
        in_specs=[pl.BlockSpec((ts, 2 * width), lambda i: (i, _exact_div(COL_UB, 2 * width))),
                  pl.BlockSpec((None, 1, width), lambda i: (layer, 0, 0)),
                  pl.BlockSpec((None, 1, width), lambda i: (layer, 0, 0)),
                  pl.BlockSpec((None, SGU_GROUPS, SGU_CHUNK, SGU_CHUNK), lambda i: (layer, 0, 0, 0)),
                  pl.BlockSpec((None, SGU_CHUNK, SGU_GROUPS), lambda i: (layer, 0, 0))],
        out_specs=pl.BlockSpec((ts, width), lambda i: (i, 0)),
        out_shape=jax.ShapeDtypeStruct((n, width), BF16),
        compiler_params=_params(("parallel",)),
        name="sgu",
    )(proj, ln_g, ln_b, w, bt)


def _hgrn_kernel(x_ref, lb_ref, ng_ref, o_ref, state_ref, kb_ref, aq_ref, *, tt, hb):
    C, SUB = HGRN_CHUNK, HGRN_SUB
    width = hb * LANES
    col_f, col_i, col_q, col_g = (j * width for j in range(4))

    @pl.when(pl.program_id(2) == 0)
    def _():
        state_ref[...] = jnp.zeros_like(state_ref)

    ng = ng_ref[...]
    r64 = lax.broadcasted_iota(jnp.int32, (C, C), 0)
    c64 = lax.broadcasted_iota(jnp.int32, (C, C), 1)
    tril = jnp.where(r64 >= c64, 1.0, 0.0).astype(BF16)
    t_i = lax.broadcasted_iota(jnp.int32, (C, LANES), 0)
    l_i = lax.broadcasted_iota(jnp.int32, (C, LANES), 1)
    own_half = ((l_i // C) == ((t_i // SUB) % 2)) & ((l_i % C) <= t_i)
    valid0 = own_half & (t_i < 2 * SUB)
    valid1 = own_half & (t_i >= 2 * SUB)

    def gates(r, h):
        cs = slice(h * LANES, (h + 1) * LANES)
        lb = lb_ref[:, cs]
        half_span = 0.5 * (1.0 - lb)
        f = (lb + half_span) + half_span * jnp.tanh(0.5 * x_ref[r, col_f + h * LANES:col_f + (h + 1) * LANES].astype(F32))
        lf = jnp.log2(f)
        k = 1.0 - f
        x_q = x_ref[r, col_q + h * LANES:col_q + (h + 1) * LANES].astype(F32)
        qf = x_q * ((0.5 * HGRN_HEAD_DIM ** -0.5) + (0.5 * HGRN_HEAD_DIM ** -0.5) * jnp.tanh(0.5 * x_q))
        hi = lf.astype(BF16)
        lo = (lf - hi.astype(F32)).astype(BF16)
        b2 = jnp.dot(tril, jnp.concatenate([hi, lo], axis=1), preferred_element_type=F32)
        b = b2[:, :LANES] + b2[:, LANES:]
        return k, qf, b

    def scores(k, qf, b):
        refs = [jnp.zeros((1, LANES), F32)] + [b[SUB * i - 1:SUB * i, :] for i in range(1, _exact_div(C, SUB))]
        rel = [qf[SUB * i:SUB * (i + 1)] * jnp.exp2(b[SUB * i:SUB * (i + 1)] - rf) for i, rf in enumerate(refs)]
        q_rel = jnp.concatenate([t.astype(BF16) for t in rel], axis=0)
        q_abs = jnp.concatenate(
            [(t if i == 0 else t * jnp.exp2(refs[i])).astype(BF16) for i, t in enumerate(rel)], axis=0)
        blocks = []
        for i, rf in enumerate(refs):
            rows = SUB * (i + 1)
            blocks.append((k[:rows] * jnp.exp2(rf - b[:rows])).astype(BF16))
            if rows < C:
                blocks.append(jnp.zeros((C - rows, LANES), BF16))
        kcat = jnp.concatenate(blocks, axis=0)
        sc = lax.dot_general(q_rel, kcat, (((1,), (1,)), ((), ())), preferred_element_type=F32)
        a = jnp.where(valid0, sc[:, :LANES], jnp.where(valid1, sc[:, LANES:], 0.0))
        return a.astype(BF16), q_abs

    def outputs(r, h, k, b, a, q_abs):
        cs = slice(h * LANES, (h + 1) * LANES)
        v = x_ref[r, col_i + h * LANES:col_i + (h + 1) * LANES]
        b_last = b[C - 1:C, :]
        vv = jnp.concatenate([v, v], axis=0)
        st = state_ref[h]
        o = (lax.dot_general(q_abs, st.astype(BF16), (((1,), (1,)), ((), ())), preferred_element_type=F32)
             + jnp.dot(a, vv, preferred_element_type=F32))
        k_end = (k * jnp.exp2(b_last - b)).astype(BF16)
        state_ref[h] = st * jnp.exp2(b_last) + lax.dot_general(
            v, k_end, (((0,), (0,)), ((), ())), preferred_element_type=F32)
        on = o * lax.rsqrt(jnp.mean(o * o, axis=-1, keepdims=True) + EPS) * ng
        x_g = x_ref[r, col_g + h * LANES:col_g + (h + 1) * LANES].astype(F32)
        o_ref[r, cs] = (on * (x_g * (0.5 + 0.5 * jnp.tanh(0.5 * x_g)))).astype(o_ref.dtype)

    def rows(c):
        return pl.ds(pl.multiple_of(c * C, C), C)

    def stage_scores(r):
        kqb = [gates(r, h) for h in range(hb)]
        aq = [scores(*t) for t in kqb]
        for h in range(hb):
            kb_ref[0, h] = kqb[h][0]
            kb_ref[1, h] = kqb[h][2]
            aq_ref[0, h], aq_ref[1, h] = aq[h]

    n_chunks = _exact_div(tt, C)
    stage_scores(rows(0))

    def chunk(c, carry):
        staged = [(kb_ref[0, h], kb_ref[1, h], aq_ref[0, h], aq_ref[1, h]) for h in range(hb)]
        for h in range(hb):
            outputs(rows(c), h, *staged[h])
        stage_scores(rows(jnp.minimum(c + 1, n_chunks - 1)))
        return carry

    lax.fori_loop(0, n_chunks, chunk, 0)


def _hgrn(proj, lb_l, norm_g, layer, *, batch, seq, tt, hb):
    n = proj.shape[0]
    nt = _exact_div(seq, tt)
    width = HGRN_HEADS * HGRN_HEAD_DIM
    bw = hb * LANES

    assert hb == HGRN_HEADS and (COL_IC, COL_QC, COL_GC) == (COL_FC + bw, COL_FC + 2 * bw, COL_FC + 3 * bw)
    return pl.pallas_call(
        functools.partial(_hgrn_kernel, tt=tt, hb=hb),
        grid=(batch, _exact_div(HGRN_HEADS, hb), nt),
        in_specs=[pl.BlockSpec((tt, 4 * bw), lambda b, h, t: (b * nt + t, _exact_div(COL_FC, 4 * bw))),
                  pl.BlockSpec((1, bw), lambda b, h, t: (0, h)),
                  pl.BlockSpec((None, 1, LANES), lambda b, h, t: (layer, 0, 0))],
        out_specs=pl.BlockSpec((tt, bw), lambda b, h, t: (b * nt + t, h)),
        out_shape=jax.ShapeDtypeStruct((n, width), BF16),
        scratch_shapes=[pltpu.VMEM((hb, HGRN_HEAD_DIM, HGRN_HEAD_DIM), F32),
                        pltpu.VMEM((2, hb, HGRN_CHUNK, LANES), F32),
                        pltpu.VMEM((2, hb, HGRN_CHUNK, LANES), BF16)],
        compiler_params=_params(("parallel", "parallel", "arbitrary")),
        name="hgrn2",
    )(proj, lb_l, norm_g)


def _merge_kernel(x_ref, ya_ref, yb_ref, yc_ref, ga_ref, gb_ref, gc_ref,
                  wa_ref, wb_ref, wc_ref, wo_ref, o_ref, *, tk):
    d = o_ref.shape[1]
    ya, yb, yc = ya_ref[...], yb_ref[...], yc_ref[...]

    def merged(c):
        cs = slice(c * tk, (c + 1) * tk)
        m = (ga_ref[:, cs].astype(F32) * jnp.dot(ya, wa_ref[:, cs], preferred_element_type=F32)
             + gb_ref[:, cs].astype(F32) * jnp.dot(yb, wb_ref[:, cs], preferred_element_type=F32)
             + gc_ref[:, cs].astype(F32) * jnp.dot(yc, wc_ref[:, cs], preferred_element_type=F32))
        return m.astype(BF16)

    n_groups = _exact_div(d, tk)
    acc = x_ref[...]
    m_next = merged(0)
    for c in range(n_groups):
        m_cur = m_next
        if c + 1 < n_groups:
            m_next = merged(c + 1)
        acc = acc + jnp.dot(m_cur, wo_ref[c * tk:(c + 1) * tk, :], preferred_element_type=F32)
    o_ref[...] = acc


def _merge(x, ya, yb, yc, proj, wa, wb, wc, wo, layer, *, tm, tk):
    n, d = x.shape
    wbr = ya.shape[1]
    g0 = _exact_div(COL_GATE, d)

    def gate(br):
        return pl.BlockSpec((tm, d), lambda i: (i, g0 + br))

    def resident(rows):
        return pl.BlockSpec((None, rows, d), lambda i: (layer, 0, 0), pipeline_mode=pl.Buffered(1))

    y_spec = pl.BlockSpec((tm, wbr), lambda i: (i, 0))
    return pl.pallas_call(
        functools.partial(_merge_kernel, tk=tk),
        grid=(_exact_div(n, tm),),
        in_specs=[pl.BlockSpec((tm, d), lambda i: (i, 0)),
                  y_spec, y_spec, y_spec, gate(0), gate(1), gate(2),
                  resident(wbr), resident(wbr), resident(wbr), resident(d)],
        out_specs=pl.BlockSpec((tm, d), lambda i: (i, 0)),
        out_shape=jax.ShapeDtypeStruct((n, d), F32),
        compiler_params=_params(("parallel",), vmem_limit=V7X_VMEM_LIMIT_MERGE),
        name="merge",
    )(x, ya, yb, yc, proj, proj, proj, wa, wb, wc, wo)


def _xattn_kernel(x_ref, wq_ref, kv_ref, wo_ref, o_ref, h_ref, *, tm):
    _rms_rows(x_ref, None, h_ref, tm)
    q = jnp.dot(h_ref[...], wq_ref[...], preferred_element_type=F32).astype(BF16)
    xw = X_HEADS * X_HEAD_DIM
    scale = X_HEAD_DIM ** -0.5 * LOG2E

    def score(hd):
        cs = slice(hd * X_HEAD_DIM, (hd + 1) * X_HEAD_DIM)
        return lax.dot_general(q[:, cs], kv_ref[:, cs], (((1,), (1,)), ((), ())),
                               preferred_element_type=F32) * scale

    def attend(hd, s):
        vh = kv_ref[:, xw + hd * X_HEAD_DIM:xw + (hd + 1) * X_HEAD_DIM]
        mx = jnp.max(s, axis=-1, keepdims=True)
        pe = jnp.exp2(s - mx)
        inv = 1.0 / jnp.sum(pe, axis=-1, keepdims=True)
        return (jnp.dot(pe.astype(BF16), vh, preferred_element_type=F32) * inv).astype(BF16)

    outs = []
    s_next = score(0)
    for hd in range(X_HEADS):
        s_cur = s_next
        if hd + 1 < X_HEADS:
            s_next = score(hd + 1)
        outs.append(attend(hd, s_cur))
    oc = jnp.concatenate(outs, axis=1)
    o_ref[...] = x_ref[...] + jnp.dot(oc, wo_ref[...], preferred_element_type=F32)


def _xattn(x, wq, kv, wo, layer, *, seq, mem_len, tm):
    n, d = x.shape
    xw = X_HEADS * X_HEAD_DIM
    tiles_per_seq = _exact_div(seq, tm)
    return pl.pallas_call(
        functools.partial(_xattn_kernel, tm=tm),
        grid=(_exact_div(n, tm),),
        in_specs=[pl.BlockSpec((tm, d), lambda i: (i, 0)),
                  pl.BlockSpec((None, d, xw), lambda i: (layer, 0, 0)),
                  pl.BlockSpec((mem_len, 2 * xw), lambda i: (i // tiles_per_seq, layer)),
                  pl.BlockSpec((None, xw, d), lambda i: (layer, 0, 0))],
        out_specs=pl.BlockSpec((tm, d), lambda i: (i, 0)),
        out_shape=jax.ShapeDtypeStruct((n, d), F32),
        scratch_shapes=[pltpu.VMEM((tm, d), BF16)],
        compiler_params=_params(("parallel",)),
        name="xattn",
    )(x, wq, kv, wo)


def _ffn_kernel(x_ref, fg_ref, wg_ref, wu_ref, wo_ref, o_ref, h_ref, *, tm, final_norm):
    def step(first):
        h = h_ref[...]
        gt = jnp.dot(h, wg_ref[...], preferred_element_type=F32)
        up = jnp.dot(h, wu_ref[...], preferred_element_type=F32)
        a = (jax.nn.silu(gt) * up).astype(BF16)
        down = jnp.dot(a, wo_ref[...], preferred_element_type=F32)
        o_ref[...] = (x_ref[...] if first else o_ref[...]) + down

    @pl.when(pl.program_id(1) == 0)
    def _():
        _rms_rows(x_ref, None, h_ref, tm)
        step(True)

    pl.when(pl.program_id(1) != 0)(functools.partial(step, False))

    if final_norm:
        @pl.when(pl.program_id(1) == pl.num_programs(1) - 1)
        def _():
            _rms_rows(o_ref, fg_ref, o_ref, tm)


def _ffn(x, final_g, w_in, w_out, layer, *, tm, tf, final_norm):
    n, d = x.shape
    nf = _exact_div(w_out.shape[1], tf)
    return pl.pallas_call(
        functools.partial(_ffn_kernel, tm=tm, final_norm=final_norm),
        grid=(_exact_div(n, tm), nf),
        in_specs=[pl.BlockSpec((tm, d), lambda i, f: (i, 0)),
                  pl.BlockSpec((1, d), lambda i, f: (0, 0)),
                  pl.BlockSpec((None, d, tf), lambda i, f: (layer, 0, f)),
                  pl.BlockSpec((None, d, tf), lambda i, f: (layer, 0, nf + f)),
                  pl.BlockSpec((None, tf, d), lambda i, f: (layer, f, 0))],
        out_specs=pl.BlockSpec((tm, d), lambda i, f: (i, 0)),
        out_shape=jax.ShapeDtypeStruct((n, d), F32),
        scratch_shapes=[pltpu.VMEM((tm, d), BF16)],
        compiler_params=_params(("parallel", "arbitrary")),
        name="ffn",
    )(x, final_g, w_in, w_in, w_out)


def kernel(x, mem, norm_mix, w_in, w_gate, sinks, sgu_ln_g, sgu_ln_b, sgu_w, sgu_b, hgrn_lb, hgrn_norm,
           w_br_a, w_br_b, w_br_c, w_out, norm_x, mem_norm, w_xq, w_xkv, w_xo, norm_ffn, w_ffn_in,
           w_ffn_out, final_norm):
    batch, seq, d = x.shape
    depth = w_in.shape[0]
    mem_len = mem.shape[1]
    n = batch * seq

    kv_end = (ATTN_Q_HEADS + 2 * ATTN_KV_HEADS) * ATTN_HEAD_DIM
    g_mix = norm_mix[:, :, None]
    w_all = jnp.concatenate(
        [(w * g_mix).astype(BF16)
         for w in (w_gate, w_in[..., kv_end:], w_in[..., :kv_end])], axis=-1)
    assert w_all.shape[-1] == COL_END
    wa, wb, wc, wo = (w.astype(BF16) for w in (w_br_a, w_br_b, w_br_c, w_out))
    wxq, wxo = (w_xq * norm_x[:, :, None]).astype(BF16), w_xo.astype(BF16)
    wxkv = jnp.concatenate([w_xkv[l] * mem_norm[:, None] for l in range(depth)], axis=-1).astype(BF16)
    wfi, wfo = (w_ffn_in * norm_ffn[:, :, None]).astype(BF16), w_ffn_out.astype(BF16)
    sgu_bt = jnp.swapaxes(sgu_b, 1, 2)
    sm = jax.nn.softmax(hgrn_lb.astype(F32), axis=0)
    lb_all = jnp.cumsum(sm, axis=0) - sm[0:1]
    sgu_ln_g, sgu_ln_b, hgrn_norm = (p[:, None, :] for p in (sgu_ln_g, sgu_ln_b, hgrn_norm))

    xf = x.reshape(n, d)
    kv_all = _norm_matmul(mem.reshape(batch * mem_len, d), wxkv, 0, tm=1024, tn=512)
    for l in range(depth):
        proj = _norm_matmul(xf, w_all, l, tm=1024, tn=1536, activate=True)
        ya = _swa(proj, sinks[l], batch=batch, seq=seq, tq=512)
        yb = _sgu(proj, sgu_ln_g, sgu_ln_b, sgu_w, sgu_bt, l, ts=1024)
        yc = _hgrn(proj, lb_all[l:l + 1], hgrn_norm, l, batch=batch, seq=seq, tt=seq, hb=8)
        xf = _merge(xf, ya, yb, yc, proj, wa, wb, wc, wo, l, tm=512, tk=256)
        xf = _xattn(xf, wxq, kv_all, wxo, l, seq=seq, mem_len=mem_len, tm=1024)
        xf = _ffn(xf, final_norm.reshape(1, d), wfi, wfo, l, tm=1024, tf=512,
                  final_norm=(l == depth - 1))
    return xf.reshape(batch, seq, d)
```
